```python
import jax, jax.numpy as jnp
from jax import lax
import numpy as np

D_MODEL = 1024
BATCH = 32
SEQ = 256
DEPTH = 2
DEC_BATCH = 8
DEC_SEQ = 4096
PAST_LEN = 512

GRID_W = 64
HEAD_DIM = 64
N_BRANCH = 4
BRANCH_W = D_MODEL // 4
HGRN_HEADS = BRANCH_W // HEAD_DIM
HGRN_DK = HEAD_DIM
HGRN_DV = HEAD_DIM
HGRN_CHUNK = 64
NA_HEADS = BRANCH_W // HEAD_DIM
NA_ROWS = 8
NA_COLS = 16
SMLP_GROUPS = 4
SMLP_CHUNK = 128
SWA_HEADS = BRANCH_W // HEAD_DIM
SWA_KV_HEADS = SWA_HEADS // 2
SWA_WINDOW = 128
SWA_BLOCK = 128
ROPE_THETA = 10000.0
N_GROUPS = 4
EXPERTS_PER_GROUP = 4
N_EXPERTS = N_GROUPS * EXPERTS_PER_GROUP
EXPERT_FF = 256
TOP_K_INNER = 2
ADA_CHUNKS = 6
EPS = 1e-6
TINY = 1e-30
IN_SPLIT_SIZES = (BRANCH_W, BRANCH_W, BRANCH_W, BRANCH_W, BRANCH_W,
                  BRANCH_W, BRANCH_W, BRANCH_W,
                  BRANCH_W, BRANCH_W,
                  SWA_HEADS * HEAD_DIM, SWA_KV_HEADS * HEAD_DIM, SWA_KV_HEADS * HEAD_DIM,
                  N_BRANCH * D_MODEL)
P_IN = sum(IN_SPLIT_SIZES)

kernel_name = "hybrid_diffusion_ctx_prefix_step"


def rmsnorm(x, g=None):
    x32 = x.astype(jnp.float32)
    y = x32 * lax.rsqrt(jnp.mean(x32 * x32, axis=-1, keepdims=True) + EPS)
    if g is not None:
        y = y * g.astype(jnp.float32)
    return y.astype(x.dtype)


def adaln(cond, w, b):
    m = jax.nn.silu(cond) @ w + b
    return jnp.split(m[:, None, :], ADA_CHUNKS, axis=-1)


def split_cols(z):
    offs = [int(o) for o in np.cumsum(IN_SPLIT_SIZES)[:-1]]
    return jnp.split(z, offs, axis=-1)


def rope_2d(x):
    T, hd = x.shape[1], x.shape[-1]
    half = hd // 2
    t = jnp.arange(T)
    rows = (t // GRID_W).astype(jnp.float32)
    cols = (t % GRID_W).astype(jnp.float32)
    inv = 1.0 / (ROPE_THETA ** (jnp.arange(0, half, 2, dtype=jnp.float32) / half))

    def rot(xa, pos):
        ang = pos[:, None] * inv[None, :]
        cos = jnp.cos(ang)[None, :, None, :]
        sin = jnp.sin(ang)[None, :, None, :]
        x1, x2 = jnp.split(xa, 2, axis=-1)
        return jnp.concatenate([x1 * cos - x2 * sin, x1 * sin + x2 * cos], axis=-1)

    x32 = x.astype(jnp.float32)
    return jnp.concatenate([rot(x32[..., :half], rows), rot(x32[..., half:], cols)], axis=-1).astype(x.dtype)


def hgrn_scan(q, k, v, logf, s0):
    B, L, H, _ = q.shape
    n = L // HGRN_CHUNK

    def to_chunks(a):
        return jnp.moveaxis(a.reshape(B, n, HGRN_CHUNK, *a.shape[2:]), 1, 0)

    tri = jnp.tril(jnp.ones((HGRN_CHUNK, HGRN_CHUNK), dtype=bool))

    def step(S, inp):
        qc, kc, vc, lc = inp
        b = jnp.cumsum(lc, axis=1)
        inter = jnp.einsum('bthd,bhde->bthe', qc * jnp.exp(b), S)
        diff = b[:, :, None] - b[:, None, :]
        decay = jnp.exp(jnp.where(tri[None, :, :, None, None], diff, -jnp.inf))
        scores = jnp.einsum('btshd,bshd->bhts', qc[:, :, None] * decay, kc)
        intra = jnp.einsum('bhts,bshe->bthe', scores, vc)
        bl = b[:, -1]
        S = jnp.exp(bl)[..., None] * S + jnp.einsum('bshd,bshe->bhde', kc * jnp.exp(bl[:, None] - b), vc)
        return S, inter + intra

    S, o = lax.scan(step, s0, (to_chunks(q), to_chunks(k), to_chunks(v), to_chunks(logf)))
    o = jnp.moveaxis(o, 0, 1).reshape(B, L, H, v.shape[-1])
    return o, S


def hgrn_mixer(hq, hi, hff, hfb, hog, lb, onorm_g, s0):
    B, L, _ = hq.shape
    heads = lambda a: a.astype(jnp.float32).reshape(B, L, HGRN_HEADS, -1)
    flip = lambda a: a[:, ::-1]
    q = jax.nn.silu(heads(hq))
    v = heads(hi)
    lb32 = lb.astype(jnp.float32).reshape(2, HGRN_HEADS, HGRN_DK)

    def forget(z, lbd):
        f = lbd + (1.0 - lbd) * jax.nn.sigmoid(z)
        return jnp.log(jnp.maximum(f, TINY)), 1.0 - f

    lf_f, k_f = forget(heads(hff), lb32[0])
    lf_b, k_b = forget(heads(hfb), lb32[1])
    s0 = s0.astype(jnp.float32)
    o_f, s_f = hgrn_scan(q, k_f, v, lf_f, s0[:, 0])
    o_b, s_b = hgrn_scan(flip(q), flip(k_b), flip(v), flip(lf_b), s0[:, 1])
    o = o_f + flip(o_b)
    o = o * lax.rsqrt(jnp.mean(o * o, axis=-1, keepdims=True) + EPS) * onorm_g.astype(jnp.float32).reshape(HGRN_HEADS, HGRN_DV)
    out = o.reshape(B, L, -1) * jax.nn.silu(hog.astype(jnp.float32))
    return out.astype(hq.dtype), jnp.stack([s_f, s_b], axis=1)


def ctx_attention(q, k, v, sink):
    B, L = q.shape[:2]
    s = jnp.einsum('bqhgd,bkhd->bhgqk', q, k).astype(jnp.float32) * (HEAD_DIM ** -0.5)
    if sink is not None:
        s_sink = jnp.broadcast_to(sink.astype(jnp.float32)[None, :, :, None, None], s.shape[:-1] + (1,))
        s = jnp.concatenate([s, s_sink], axis=-1)
    p = jax.nn.softmax(s, axis=-1)
    if sink is not None:
        p = p[..., :-1]
    out = jnp.einsum('bhgqk,bkhd->bqhgd', p.astype(v.dtype), v)
    return out.reshape(B, L, -1)


def na_latent(q, k, v, kc, vc, rpb):
    B, T = q.shape[:2]
    R = T // GRID_W
    WR = min(NA_ROWS, R)
    grid = lambda a: a.reshape(B, R, GRID_W, NA_HEADS, HEAD_DIM)
    q, k, v = grid(q), grid(k), grid(v)
    r = jnp.arange(R)
    row_start = jnp.clip(r - WR // 2, 0, R - WR)
    row_idx = row_start[:, None] + jnp.arange(WR)[None, :]
    kg = k[:, row_idx]
    vg = v[:, row_idx]
    col = jnp.arange(GRID_W)
    col_start = jnp.clip(col - NA_COLS // 2, 0, GRID_W - NA_COLS)
    col_mask = (col[None, :] >= col_start[:, None]) & (col[None, :] < col_start[:, None] + NA_COLS)
    d_row = row_idx - r[:, None] + (NA_ROWS - 1)
    d_col = jnp.clip(col[None, :] - col[:, None], -(NA_COLS - 1), NA_COLS - 1) + (NA_COLS - 1)
    bias = rpb[:, d_row[:, None, :, None], d_col[None, :, None, :]]
    scale = HEAD_DIM ** -0.5
    s_lat = jnp.einsum('brqhd,brikhd->bhrqik', q, kg).astype(jnp.float32) * scale + bias.astype(jnp.float32)
    s_lat = jnp.where(col_mask[:, None, :], s_lat, -jnp.inf)
    s_ctx = jnp.einsum('brqhd,bchd->bhrqc', q, kc).astype(jnp.float32) * scale
    n_lat = WR * GRID_W
    p = jax.nn.softmax(jnp.concatenate([s_lat.reshape(B, NA_HEADS, R, GRID_W, n_lat), s_ctx], axis=-1), axis=-1).astype(v.dtype)
    p_lat = p[..., :n_lat].reshape(s_lat.shape)
    p_ctx = p[..., n_lat:]
    out = jnp.einsum('bhrqik,brikhd->brqhd', p_lat, vg) + jnp.einsum('bhrqc,bchd->brqhd', p_ctx, vc)
    return out.reshape(B, T, -1)


def chunk_mlp(mu, mv, ws, b):
    B, L, _ = mu.shape
    n = L // SMLP_CHUNK
    v = rmsnorm(mv.reshape(B, n, SMLP_CHUNK, SMLP_GROUPS, -1))
    mixed = jnp.einsum('gts,bnsgc->bntgc', ws, v) + b.T[None, None, :, :, None]
    return (mu.reshape(B, n, SMLP_CHUNK, SMLP_GROUPS, -1) * mixed).reshape(B, L, -1)


def swa_latent(q, k, v, kc, vc, sink):
    B, T = q.shape[:2]
    nb = T // SWA_BLOCK
    G = SWA_HEADS // SWA_KV_HEADS
    qb = q.reshape(B, nb, SWA_BLOCK, SWA_KV_HEADS, G, HEAD_DIM)

    def band(a):
        ab = a.reshape(B, nb, SWA_BLOCK, SWA_KV_HEADS, HEAD_DIM)
        ap = jnp.pad(ab, ((0, 0), (1, 1), (0, 0), (0, 0), (0, 0)))
        return jnp.concatenate([ap[:, :-2], ap[:, 1:-1], ap[:, 2:]], axis=2)

    kb, vb = band(k), band(v)
    qpos = jnp.arange(nb)[:, None] * SWA_BLOCK + jnp.arange(SWA_BLOCK)[None, :]
    kpos = (jnp.arange(nb)[:, None] - 1) * SWA_BLOCK + jnp.arange(3 * SWA_BLOCK)[None, :]
    rel = kpos[:, None, :] - qpos[:, :, None]
    mask = (jnp.abs(rel) <= SWA_WINDOW) & (kpos[:, None, :] >= 0) & (kpos[:, None, :] < T)
    scale = HEAD_DIM ** -0.5
    s_band = jnp.einsum('bnqhgd,bnjhd->bhgnqj', qb, kb).astype(jnp.float32) * scale
    s_band = jnp.where(mask, s_band, -jnp.inf)
    s_ctx = jnp.einsum('bnqhgd,bchd->bhgnqc', qb, kc).astype(jnp.float32) * scale
    s_sink = jnp.broadcast_to(sink.astype(jnp.float32).reshape(SWA_KV_HEADS, G)[None, :, :, None, None, None], s_band.shape[:-1] + (1,))
    p = jax.nn.softmax(jnp.concatenate([s_band, s_ctx, s_sink], axis=-1), axis=-1).astype(v.dtype)
    nj = 3 * SWA_BLOCK
    C = kc.shape[1]
    out = jnp.einsum('bhgnqj,bnjhd->bnqhgd', p[..., :nj], vb) + jnp.einsum('bhgnqc,bchd->bnqhgd', p[..., nj:nj + C], vc)
    return out.reshape(B, T, -1)


def merge_branches(branches, gates, w_branch, w_out):
    br = jnp.stack(branches, axis=2)
    proj = jnp.einsum('blnc,ncd->blnd', br, w_branch)
    g = jax.nn.sigmoid(gates.reshape(*gates.shape[:2], N_BRANCH, D_MODEL))
    return jnp.sum(g * proj, axis=2) @ w_out


def hier_moe(h, w_rg, b_rg, w_re, b_re, w_gate, w_up, w_down):
    g_logits = (h @ w_rg + b_rg).astype(jnp.float32)
    g_prob = jax.nn.softmax(g_logits, axis=-1)
    g_top_p, g_idx = lax.top_k(g_prob, 1)
    e_logits = (h @ w_re + b_re).astype(jnp.float32).reshape(*h.shape[:-1], N_GROUPS, EXPERTS_PER_GROUP)
    e_sel = jnp.sum(jax.nn.one_hot(g_idx[..., 0], N_GROUPS, dtype=jnp.float32)[..., None] * e_logits, axis=-2)
    e_top, e_idx = lax.top_k(e_sel, TOP_K_INNER)
    w = jax.nn.softmax(e_top, axis=-1) * g_top_p
    expert = g_idx * EXPERTS_PER_GROUP + e_idx
    combine = jnp.sum(jax.nn.one_hot(expert, N_EXPERTS, dtype=jnp.float32) * w[..., None], axis=-2).astype(h.dtype)
    a = jax.nn.silu(jnp.einsum('...d,edf->...ef', h, w_gate)) * jnp.einsum('...d,edf->...ef', h, w_up)
    return jnp.einsum('...ef,efd->...d', a * combine[..., None], w_down)


def pre_mix(x, cond, lw):
    sh1, sc1, g1, sh2, sc2, g2 = adaln(cond, lw['w_ada'], lw['b_ada'])
    h = rmsnorm(x, lw['norm1_g']) * (1 + sc1) + sh1
    return split_cols(h @ lw['w_in']), (g1, sh2, sc2, g2)


def post_mix(x, branches, gates, mods, lw):
    g1, sh2, sc2, g2 = mods
    x = x + g1 * merge_branches(branches, gates, lw['w_branch'], lw['w_out'])
    h = rmsnorm(x, lw['norm2_g']) * (1 + sc2) + sh2
    return x + g2 * hier_moe(h, lw['router_g_w'], lw['router_g_b'], lw['router_e_w'], lw['router_e_b'],
                             lw['moe_w_gate'], lw['moe_w_up'], lw['moe_w_down'])


def context_layer(x, c_ctx, lw):
    parts, mods = pre_mix(x, c_ctx[None, :], lw)
    hq, hi, hff, hfb, hog, naq, nak, nav, mu, mv, sq, sk, sv, gates = parts
    B, L = x.shape[:2]
    heads = lambda a, n: a.reshape(B, L, n, HEAD_DIM)
    s_zero = jnp.zeros((B, 2, HGRN_HEADS, HGRN_DK, HGRN_DV), jnp.float32)
    a_out, s_hgrn = hgrn_mixer(hq, hi, hff, hfb, hog, lw['lb'], lw['hgrn_onorm_g'], s_zero)
    na_k, na_v = heads(nak, NA_HEADS), heads(nav, NA_HEADS)
    b_out = ctx_attention(naq.reshape(B, L, NA_HEADS, 1, HEAD_DIM), na_k, na_v, None)
    c_out = chunk_mlp(mu, mv, lw['smlp_ws'], lw['smlp_b'])
    G = SWA_HEADS // SWA_KV_HEADS
    swa_k, swa_v = heads(sk, SWA_KV_HEADS), heads(sv, SWA_KV_HEADS)
    d_out = ctx_attention(sq.reshape(B, L, SWA_KV_HEADS, G, HEAD_DIM), swa_k, swa_v, lw['swa_sink'].reshape(SWA_KV_HEADS, G))
    x = post_mix(x, (a_out, b_out, c_out, d_out), gates, mods, lw)
    return x, na_k, na_v, swa_k, swa_v, s_hgrn.astype(x.dtype)


def latent_layer(x, c, lw, ck_na, cv_na, ck_swa, cv_swa, s_hgrn):
    parts, mods = pre_mix(x, c, lw)
    hq, hi, hff, hfb, hog, naq, nak, nav, mu, mv, sq, sk, sv, gates = parts
    B, T = x.shape[:2]
    heads = lambda a, n: a.reshape(B, T, n, HEAD_DIM)
    a_out, _ = hgrn_mixer(hq, hi, hff, hfb, hog, lw['lb'], lw['hgrn_onorm_g'], s_hgrn)
    b_out = na_latent(heads(naq, NA_HEADS), heads(nak, NA_HEADS), heads(nav, NA_HEADS), ck_na, cv_na, lw['na_rpb'])
    c_out = chunk_mlp(mu, mv, lw['smlp_ws'], lw['smlp_b'])
    d_out = swa_latent(rope_2d(heads(sq, SWA_HEADS)), rope_2d(heads(sk, SWA_KV_HEADS)), heads(sv, SWA_KV_HEADS),
                       ck_swa, cv_swa, lw['swa_sink'])
    return post_mix(x, (a_out, b_out, c_out, d_out), gates, mods, lw)


def setup_inputs(seed: int = 0) -> dict:
    key = jax.random.key(seed)
    ks = jax.random.split(key, 40)
    nrm = lambda k, shape, s: jax.random.normal(k, shape, jnp.float32) * s
    return {
        "x_prompt": nrm(ks[0], (BATCH, SEQ, D_MODEL), 1.0),
        "x_sample": nrm(ks[1], (DEC_BATCH, DEC_SEQ, D_MODEL), 1.0),
        "c": nrm(ks[2], (DEC_BATCH, D_MODEL), 1.0),
        "cache_na_k": nrm(ks[3], (DEC_BATCH, DEPTH, PAST_LEN, NA_HEADS, HEAD_DIM), 1.0),
        "cache_na_v": nrm(ks[4], (DEC_BATCH, DEPTH, PAST_LEN, NA_HEADS, HEAD_DIM), 1.0),
        "cache_swa_k": nrm(ks[5], (DEC_BATCH, DEPTH, PAST_LEN, SWA_KV_HEADS, HEAD_DIM), 1.0),
        "cache_swa_v": nrm(ks[6], (DEC_BATCH, DEPTH, PAST_LEN, SWA_KV_HEADS, HEAD_DIM), 1.0),
        "state_hgrn": nrm(ks[7], (DEC_BATCH, DEPTH, 2, HGRN_HEADS, HGRN_DK, HGRN_DV), 0.5),
        "c_ctx": nrm(ks[8], (D_MODEL,), 1.0),
        "w_ada": nrm(ks[9], (DEPTH, D_MODEL, ADA_CHUNKS * D_MODEL), 0.5 * D_MODEL ** -0.5),
        "b_ada": nrm(ks[10], (DEPTH, ADA_CHUNKS * D_MODEL), 0.02),
        "norm1_g": 1.0 + nrm(ks[11], (DEPTH, D_MODEL), 0.02),
        "norm2_g": 1.0 + nrm(ks[12], (DEPTH, D_MODEL), 0.02),
        "w_in": nrm(ks[13], (DEPTH, D_MODEL, P_IN), D_MODEL ** -0.5),
        "hgrn_lb": nrm(ks[14], (DEPTH, 2, BRANCH_W), 1.0),
        "hgrn_onorm_g": 1.0 + nrm(ks[15], (DEPTH, BRANCH_W), 0.02),
        "na_rpb": nrm(ks[16], (DEPTH, NA_HEADS, 2 * NA_ROWS - 1, 2 * NA_COLS - 1), 0.1),
        "smlp_ws": nrm(ks[17], (DEPTH, SMLP_GROUPS, SMLP_CHUNK, SMLP_CHUNK), SMLP_CHUNK ** -0.5),
        "smlp_b": 1.0 + nrm(ks[18], (DEPTH, SMLP_GROUPS, SMLP_CHUNK), 0.02),
        "swa_sink": nrm(ks[19], (DEPTH, SWA_HEADS), 0.5),
        "w_branch": nrm(ks[20], (DEPTH, N_BRANCH, BRANCH_W, D_MODEL), BRANCH_W ** -0.5),
        "w_out": nrm(ks[21], (DEPTH, D_MODEL, D_MODEL), D_MODEL ** -0.5),
        "router_g_w": nrm(ks[22], (DEPTH, D_MODEL, N_GROUPS), D_MODEL ** -0.5),
        "router_g_b": nrm(ks[23], (DEPTH, N_GROUPS), 0.01),
        "router_e_w": nrm(ks[24], (DEPTH, D_MODEL, N_EXPERTS), D_MODEL ** -0.5),
        "router_e_b": nrm(ks[25], (DEPTH, N_EXPERTS), 0.01),
        "moe_w_gate": nrm(ks[26], (DEPTH, N_EXPERTS, D_MODEL, EXPERT_FF), D_MODEL ** -0.5),
        "moe_w_up": nrm(ks[27], (DEPTH, N_EXPERTS, D_MODEL, EXPERT_FF), D_MODEL ** -0.5),
        "moe_w_down": nrm(ks[28], (DEPTH, N_EXPERTS, EXPERT_FF, D_MODEL), EXPERT_FF ** -0.5),
        "final_g": 1.0 + nrm(ks[29], (D_MODEL,), 0.02),
    }


def reference(x_prompt, x_sample, c, cache_na_k, cache_na_v, cache_swa_k, cache_swa_v, state_hgrn,
              c_ctx, w_ada, b_ada, norm1_g, norm2_g, w_in, hgrn_lb, hgrn_onorm_g, na_rpb, smlp_ws, smlp_b,
              swa_sink, w_branch, w_out, router_g_w, router_g_b, router_e_w, router_e_b,
              moe_w_gate, moe_w_up, moe_w_down, final_g):
    lb_soft = jax.nn.softmax(hgrn_lb.astype(jnp.float32), axis=0)
    lb_all = jnp.cumsum(lb_soft, axis=0) - lb_soft[0:1]
    lws = [dict(w_ada=w_ada[l], b_ada=b_ada[l], norm1_g=norm1_g[l], norm2_g=norm2_g[l], w_in=w_in[l],
                lb=lb_all[l], hgrn_onorm_g=hgrn_onorm_g[l], na_rpb=na_rpb[l], smlp_ws=smlp_ws[l],
                smlp_b=smlp_b[l], swa_sink=swa_sink[l], w_branch=w_branch[l], w_out=w_out[l],
                router_g_w=router_g_w[l], router_g_b=router_g_b[l], router_e_w=router_e_w[l],
                router_e_b=router_e_b[l], moe_w_gate=moe_w_gate[l], moe_w_up=moe_w_up[l],
                moe_w_down=moe_w_down[l]) for l in range(DEPTH)]

    xp = x_prompt
    na_ks, na_vs, swa_ks, swa_vs, hgrn_ss = [], [], [], [], []
    for l in range(DEPTH):
        xp, nk, nv, sk, sv, ss = context_layer(xp, c_ctx, lws[l])
        na_ks.append(nk); na_vs.append(nv); swa_ks.append(sk); swa_vs.append(sv); hgrn_ss.append(ss)
    y_prompt = rmsnorm(xp, final_g)
    new_na_k = jnp.stack(na_ks, axis=1)
    new_na_v = jnp.stack(na_vs, axis=1)
    new_swa_k = jnp.stack(swa_ks, axis=1)
    new_swa_v = jnp.stack(swa_vs, axis=1)
    new_hgrn_state = jnp.stack(hgrn_ss, axis=1)

    xs = x_sample
    for l in range(DEPTH):
        xs = latent_layer(xs, c, lws[l], cache_na_k[:, l], cache_na_v[:, l], cache_swa_k[:, l],
                          cache_swa_v[:, l], state_hgrn[:, l])
    y_sample = rmsnorm(xs, final_g)
    return (y_prompt, y_sample, new_na_k, new_na_v, new_swa_k, new_swa_v, new_hgrn_state)
```

```python
import functools

import numpy as np
import jax
import jax.numpy as jnp
from jax import lax
from jax.experimental import pallas as pl
from jax.experimental.pallas import tpu as pltpu

D_MODEL = 1024
DEPTH = 2
GRID_W = 64
HEAD_DIM = 64
N_BRANCH = 4
BRANCH_W = 256
N_HEADS = 4
HGRN_CHUNK = 64
NA_ROWS = 8
NA_COLS = 16
SMLP_GROUPS = 4
SMLP_CHUNK = 128
SWA_KV_HEADS = 2
SWA_WINDOW = 128
SWA_BLOCK = 128
ROPE_THETA = 10000.0
N_GROUPS = 4
EXPERTS_PER_GROUP = 4
N_EXPERTS = 16
EXPERT_FF = 256
ADA_CHUNKS = 6
EPS = 1e-6
TINY = 1e-30
P_IN = 7168
ATT_SCALE = HEAD_DIM ** -0.5

COL_HQ, COL_HI, COL_HFF, COL_HFB, COL_HOG = 0, 1, 2, 3, 4
COL_NAQ, COL_NAK, COL_NAV = 5, 6, 7
COL_MU, COL_MV = 8, 9
COL_SQ = 10
COL_SK128, COL_SV128 = 22, 23
COL_GATES1024 = 3

MOD_ROWS = 16
VMEM_LIMIT = 56 * 1024 * 1024

F32 = jnp.float32
BF16 = jnp.bfloat16
HGRN_LEVELS = (0, 1, 2, 4, 8, 16, 32)


def _bf(x):
    return x.astype(BF16)


def _dot(a, b):
    return jnp.dot(a, b, preferred_element_type=F32)


def _dot_nt(a, b):
    return lax.dot_general(a, b, (((1,), (1,)), ((), ())), preferred_element_type=F32)


def _dot_tn(a, b):
    return lax.dot_general(a, b, (((0,), (0,)), ((), ())), preferred_element_type=F32)


def _split3(x):
    hi = _bf(x)
    r1 = x - hi.astype(F32)
    mid = _bf(r1)
    lo = _bf(r1 - mid.astype(F32))
    return hi, mid, lo


def _dot01_left(m01, x):
    hi, mid, lo = _split3(x)
    return _dot(m01, hi) + _dot(m01, mid) + _dot(m01, lo)


def _dot01_right(x, m01):
    hi, mid, lo = _split3(x)
    return _dot(hi, m01) + _dot(mid, m01) + _dot(lo, m01)


def _silu(x):
    return x * jax.nn.sigmoid(x)


def _params(*sem):
    return pltpu.CompilerParams(dimension_semantics=sem, vmem_limit_bytes=VMEM_LIMIT)


def _const_spec(shape):
    n = len(shape)
    return pl.BlockSpec(shape, lambda *_: (0,) * n)


def _ada_kernel(cond_ref, w_ref, b_ref, o_ref):
    s = _silu(cond_ref[...])
    o_ref[0] = _dot(_bf(s), _bf(w_ref[0])) + b_ref[0]


def _ada_mods(cond, w_ada, b_ada):
    tn = 1536
    n = ADA_CHUNKS * D_MODEL
    return pl.pallas_call(
        _ada_kernel,
        grid=(DEPTH, n // tn),
        in_specs=[
            pl.BlockSpec((MOD_ROWS, D_MODEL), lambda l, j: (0, 0)),
            pl.BlockSpec((1, D_MODEL, tn), lambda l, j: (l, 0, j)),
            pl.BlockSpec((1, 1, tn), lambda l, j: (l, 0, j)),
        ],
        out_specs=pl.BlockSpec((1, MOD_ROWS, tn), lambda l, j: (l, 0, j)),
        out_shape=jax.ShapeDtypeStruct((DEPTH, MOD_ROWS, n), F32),
        compiler_params=_params("arbitrary", "arbitrary"),
        name="ada_mods",
    )(cond, w_ada, b_ada.reshape(DEPTH, 1, n))


def _in_kernel(x_ref, sh_ref, sc_ref, g_ref, w_ref, z_ref, h_ref):
    @pl.when(pl.program_id(1) == 0)
    def _():
        x = x_ref[...]
        ms = jnp.mean(x * x, axis=-1, keepdims=True)
        h = x * lax.rsqrt(ms + EPS) * g_ref[...]
        h_ref[...] = _bf(h * (1.0 + sc_ref[0]) + sh_ref[0])

    z_ref[...] = _dot(h_ref[...], w_ref[...])


def _in_proj(x, mods3, mod_row, norm_g, w_in_bf, tm):
    n = x.shape[0]
    tn = 1024
    return pl.pallas_call(
        _in_kernel,
        grid=(n // tm, P_IN // tn),
        in_specs=[
            pl.BlockSpec((tm, D_MODEL), lambda i, j: (i, 0)),
            pl.BlockSpec((1, 1, D_MODEL), lambda i, j: (mod_row(i, tm), 0, 0)),
            pl.BlockSpec((1, 1, D_MODEL), lambda i, j: (mod_row(i, tm), 0, 1)),
            pl.BlockSpec((1, D_MODEL), lambda i, j: (0, 0)),
            pl.BlockSpec((D_MODEL, tn), lambda i, j: (0, j)),
        ],
        out_specs=pl.BlockSpec((tm, tn), lambda i, j: (i, j)),
        out_shape=jax.ShapeDtypeStruct((n, P_IN), F32),
        scratch_shapes=[pltpu.VMEM((tm, D_MODEL), BF16)],
        compiler_params=_params("arbitrary", "arbitrary"),
        name="in_proj",
    )(x, mods3, mods3, norm_g.reshape(1, D_MODEL), w_in_bf)


def _hgrn_tables():
    c = HGRN_CHUNK
    t = np.arange(c)
    tsel = np.zeros((2, len(HGRN_LEVELS) * c, c), np.float32)
    pmask = np.zeros((2, len(HGRN_LEVELS), N_HEADS * c, c), np.float32)
    for rev in (0, 1):
        cum = (t[None, :] >= t[:, None]) if rev else (t[None, :] <= t[:, None])
        tsel[rev, :c] = cum
        pmask[rev, 0] = np.tile(np.eye(c, dtype=np.float32), (N_HEADS, 1))
        for li, m in enumerate(HGRN_LEVELS[1:], start=1):
            start = (t // (2 * m)) * (2 * m)
            refrow = start + (m - 1 if rev else m)
            tsel[rev, li * c:(li + 1) * c] = cum[refrow]
            same = (t[:, None] // (2 * m)) == (t[None, :] // (2 * m))
            q_half = ((t & m) == 0) if rev else ((t & m) != 0)
            k_half = ~q_half
            pmask[rev, li] = np.tile((same & q_half[:, None] & k_half[None, :]).astype(np.float32), (N_HEADS, 1))
    return jnp.asarray(tsel, BF16), jnp.asarray(pmask, F32)


def _hgrn_chunk(q_raw, v, f_raw, lb, st, tsel, pmask_ref, rev, head_mask):
    c = HGRN_CHUNK
    qq = _silu(q_raw)
    f = lb + (1.0 - lb) * jax.nn.sigmoid(f_raw)
    lf = jnp.log(jnp.maximum(f, TINY))
    k = 1.0 - f
    ball = _dot01_left(tsel, lf)
    b = ball[:c]
    bl = b[0:1] if rev else b[c - 1:c]
    vb = _bf(v)

    o = _dot_nt(_bf(qq * jnp.exp(b)), _bf(st))

    qqb = _bf(qq)
    kb = _bf(k)
    p = None
    for li, m in enumerate(HGRN_LEVELS):
        if m == 0:
            qe, ke = qqb, kb
        else:
            e = jnp.exp(-jnp.abs(b - ball[li * c:(li + 1) * c]))
            qe, ke = _bf(qq * e), _bf(k * e)
        qx = jnp.where(head_mask, jnp.concatenate([qe] * N_HEADS, axis=0), jnp.zeros((), BF16))
        s = _dot_nt(qx, ke) * pmask_ref[li]
        p = s if p is None else p + s
    r = jnp.where(head_mask, _dot(_bf(p), vb), 0.0)
    for h in range(N_HEADS):
        o = o + r[h * c:(h + 1) * c]

    ke_state = _bf(k * jnp.exp(bl - b))
    st_new = st * jnp.exp(bl) + jnp.where(head_mask, _dot_tn(vb, ke_state), 0.0)
    return o, st_new


def _hgrn_kernel(qf_ref, vf_ref, ff_ref, qb_ref, vb_ref, fb_ref, lb_ref, st0_ref, tsel_ref, pmask_ref,
                 of_ref, ob_ref, st_ref, *, n_sub):
    c = HGRN_CHUNK

    @pl.when(pl.program_id(1) == 0)
    def _():
        st_ref[...] = st0_ref[...]

    row = lax.broadcasted_iota(jnp.int32, (N_HEADS * c, BRANCH_W), 0)
    lane = lax.broadcasted_iota(jnp.int32, (N_HEADS * c, BRANCH_W), 1)
    head_mask = (row // c) == (lane // HEAD_DIM)
    st_f = st_ref[0, 0]
    st_b = st_ref[0, 1]
    for j in range(n_sub):
        rf = slice(j * c, (j + 1) * c)
        rb = slice((n_sub - 1 - j) * c, (n_sub - j) * c)
        o_f, st_f = _hgrn_chunk(qf_ref[rf, :], vf_ref[rf, :], ff_ref[rf, :], lb_ref[0:1, :], st_f,
                                tsel_ref[0], pmask_ref.at[0], False, head_mask)
        o_b, st_b = _hgrn_chunk(qb_ref[rb, :], vb_ref[rb, :], fb_ref[rb, :], lb_ref[1:2, :], st_b,
                                tsel_ref[1], pmask_ref.at[1], True, head_mask)
        of_ref[rf, :] = o_f
        ob_ref[rb, :] = o_b
    st_ref[0, 0] = st_f
    st_ref[0, 1] = st_b


def _hgrn_scan(z, lb2, st0, batch, seqlen, n_sub):
    n = batch * seqlen
    tb = n_sub * HGRN_CHUNK
    nblk = seqlen // tb
    tsel, pmask = _hgrn_tables()
    fwd = lambda col: pl.BlockSpec((tb, BRANCH_W), lambda b, c: (b * nblk + c, col))
    bwd = lambda col: pl.BlockSpec((tb, BRANCH_W), lambda b, c: (b * nblk + nblk - 1 - c, col))
    return pl.pallas_call(
        functools.partial(_hgrn_kernel, n_sub=n_sub),
        grid=(batch, nblk),
        in_specs=[
            fwd(COL_HQ), fwd(COL_HI), fwd(COL_HFF),
            bwd(COL_HQ), bwd(COL_HI), bwd(COL_HFB),
            _const_spec((2, BRANCH_W)),
            pl.BlockSpec((1, 2, BRANCH_W, BRANCH_W), lambda b, c: (b, 0, 0, 0)),
            _const_spec(tsel.shape),
            _const_spec(pmask.shape),
        ],
        out_specs=[
            pl.BlockSpec((tb, BRANCH_W), lambda b, c: (b * nblk + c, 0)),
            pl.BlockSpec((tb, BRANCH_W), lambda b, c: (b * nblk + nblk - 1 - c, 0)),
            pl.BlockSpec((1, 2, BRANCH_W, BRANCH_W), lambda b, c: (b, 0, 0, 0)),
        ],
        out_shape=[
            jax.ShapeDtypeStruct((n, BRANCH_W), F32),
            jax.ShapeDtypeStruct((n, BRANCH_W), F32),
            jax.ShapeDtypeStruct((batch, 2, BRANCH_W, BRANCH_W), F32),
        ],
        compiler_params=_params("arbitrary", "arbitrary"),
        name="hgrn_scan",
    )(z, z, z, z, z, z, lb2, st0, tsel, pmask)


def _group_ones():
    g = np.arange(BRANCH_W) // HEAD_DIM
    return jnp.asarray((g[:, None] == g[None, :]).astype(np.float32), BF16)


def _hgrn_post_kernel(of_ref, ob_ref, og_ref, g_ref, ones_ref, a_ref):
    o = of_ref[...] + ob_ref[...]
    ms = _dot01_right(o * o, ones_ref[...]) * (1.0 / HEAD_DIM)
    a_ref[...] = o * lax.rsqrt(ms + EPS) * g_ref[...] * _silu(og_ref[...])


def _hgrn_post(o_f, o_b, z, onorm_g, tm):
    n = o_f.shape[0]
    return pl.pallas_call(
        _hgrn_post_kernel,
        grid=(n // tm,),
        in_specs=[
            pl.BlockSpec((tm, BRANCH_W), lambda i: (i, 0)),
            pl.BlockSpec((tm, BRANCH_W), lambda i: (i, 0)),
            pl.BlockSpec((tm, BRANCH_W), lambda i: (i, COL_HOG)),
            _const_spec((1, BRANCH_W)),
            _const_spec((BRANCH_W, BRANCH_W)),
        ],
        out_specs=pl.BlockSpec((tm, BRANCH_W), lambda i: (i, 0)),
        out_shape=jax.ShapeDtypeStruct((n, BRANCH_W), F32),
        compiler_params=_params("arbitrary"),
        name="hgrn_post",
    )(o_f, o_b, z, onorm_g.reshape(1, BRANCH_W), _group_ones())


def _ctx_attn_kernel(*refs, n_q, n_kv, has_sink):
    if has_sink:
        sink_ref, q_ref, k_ref, v_ref, o_ref = refs
    else:
        q_ref, k_ref, v_ref, o_ref = refs
    group = n_q // n_kv
    for hq in range(n_q):
        hk = hq // group
        q = _bf(q_ref[:, hq * HEAD_DIM:(hq + 1) * HEAD_DIM])
        k = _bf(k_ref[:, hk * HEAD_DIM:(hk + 1) * HEAD_DIM])
        v = _bf(v_ref[:, hk * HEAD_DIM:(hk + 1) * HEAD_DIM])
        s = _dot_nt(q, k) * ATT_SCALE
        m = jnp.max(s, axis=-1, keepdims=True)
        if has_sink:
            sink = sink_ref[hq]
            m = jnp.maximum(m, sink)
        p = jnp.exp(s - m)
        l = jnp.sum(p, axis=-1, keepdims=True)
        if has_sink:
            l = l + jnp.exp(sink - m)
        o_ref[:, hq * HEAD_DIM:(hq + 1) * HEAD_DIM] = _dot(_bf(p), v) / l


def _ctx_attn(z, batch, seqlen, q_col, k_col, v_col, kv_width, n_kv, sink):
    n = batch * seqlen
    n_q = N_HEADS
    has_sink = sink is not None
    in_specs = [
        pl.BlockSpec((seqlen, BRANCH_W), lambda b: (b, q_col)),
        pl.BlockSpec((seqlen, kv_width), lambda b: (b, k_col)),
        pl.BlockSpec((seqlen, kv_width), lambda b: (b, v_col)),
    ]
    args = [z, z, z]
    if has_sink:
        in_specs = [pl.BlockSpec(memory_space=pltpu.SMEM)] + in_specs
        args = [sink.astype(F32)] + args
    return pl.pallas_call(
        functools.partial(_ctx_attn_kernel, n_q=n_q, n_kv=n_kv, has_sink=has_sink),
        grid=(batch,),
        in_specs=in_specs,
        out_specs=pl.BlockSpec((seqlen, BRANCH_W), lambda b: (b, 0)),
        out_shape=jax.ShapeDtypeStruct((n, BRANCH_W), F32),
        compiler_params=_params("arbitrary"),
        name="ctx_attn_sink" if has_sink else "ctx_attn",
    )(*args)


def _na_bias_table(rpb, n_rows):
    wr = min(NA_ROWS, n_rows)
    col = np.arange(GRID_W)
    col_start = np.clip(col - NA_COLS // 2, 0, GRID_W - NA_COLS)
    col_mask = (col[None, :] >= col_start[:, None]) & (col[None, :] < col_start[:, None] + NA_COLS)
    d_col = np.clip(col[None, :] - col[:, None], -(NA_COLS - 1), NA_COLS - 1) + (NA_COLS - 1)
    tabs = []
    for first in range(NA_ROWS):
        d_row = np.minimum(first + np.arange(wr), 2 * NA_ROWS - 2)
        bias = rpb[:, d_row[:, None, None], d_col[None, :, :]]
        bias = jnp.where(jnp.asarray(col_mask)[None, None], bias.astype(F32), -jnp.inf)
        tabs.append(jnp.transpose(bias, (0, 2, 1, 3)).reshape(rpb.shape[0], GRID_W, wr * GRID_W))
    return jnp.stack(tabs, axis=0)


def _na_lat_kernel(q_ref, k_ref, v_ref, kc_ref, vc_ref, bias_ref, o_ref, *, rows_per_step, n_rows):
    wr = min(NA_ROWS, n_rows)
    nk = wr * GRID_W
    r0 = pl.program_id(1) * rows_per_step

    def body(i, carry):
        r = r0 + i
        row_start = jnp.clip(r - wr // 2, 0, n_rows - wr)
        first = row_start - r + (NA_ROWS - 1)
        k0 = pl.multiple_of(row_start * GRID_W, GRID_W)
        q0 = pl.multiple_of(i * GRID_W, GRID_W)
        kw = k_ref[pl.ds(k0, nk), :]
        vw = v_ref[pl.ds(k0, nk), :]
        qr = q_ref[pl.ds(q0, GRID_W), :]
        outs = []
        for h in range(N_HEADS):
            hs = slice(h * HEAD_DIM, (h + 1) * HEAD_DIM)
            qh = _bf(qr[:, hs])
            s_lat = _dot_nt(qh, _bf(kw[:, hs])) * ATT_SCALE + bias_ref[first, h]
            s_ctx = _dot_nt(qh, _bf(kc_ref[0, :, hs])) * ATT_SCALE
            m = jnp.maximum(jnp.max(s_lat, axis=-1, keepdims=True), jnp.max(s_ctx, axis=-1, keepdims=True))
            p_lat = jnp.exp(s_lat - m)
            p_ctx = jnp.exp(s_ctx - m)
            l = jnp.sum(p_lat, axis=-1, keepdims=True) + jnp.sum(p_ctx, axis=-1, keepdims=True)
            acc = _dot(_bf(p_lat), _bf(vw[:, hs])) + _dot(_bf(p_ctx), _bf(vc_ref[0, :, hs]))
            outs.append(acc / l)
        o_ref[pl.ds(q0, GRID_W), :] = jnp.concatenate(outs, axis=-1)
        return carry

    lax.fori_loop(0, rows_per_step, body, 0)


def _na_latent(z, kc, vc, bias_tab, batch, seqlen):
    n = batch * seqlen
    n_rows = seqlen // GRID_W
    rows_per_step = 8
    steps = n_rows // rows_per_step
    tq = rows_per_step * GRID_W
    n_ctx = kc.shape[1]
    return pl.pallas_call(
        functools.partial(_na_lat_kernel, rows_per_step=rows_per_step, n_rows=n_rows),
        grid=(batch, steps),
        in_specs=[
            pl.BlockSpec((tq, BRANCH_W), lambda b, j: (b * steps + j, COL_NAQ)),
            pl.BlockSpec((seqlen, BRANCH_W), lambda b, j: (b, COL_NAK)),
            pl.BlockSpec((seqlen, BRANCH_W), lambda b, j: (b, COL_NAV)),
            pl.BlockSpec((1, n_ctx, BRANCH_W), lambda b, j: (b, 0, 0)),
            pl.BlockSpec((1, n_ctx, BRANCH_W), lambda b, j: (b, 0, 0)),
            _const_spec(bias_tab.shape),
        ],
        out_specs=pl.BlockSpec((tq, BRANCH_W), lambda b, j: (b * steps + j, 0)),
        out_shape=jax.ShapeDtypeStruct((n, BRANCH_W), F32),
        compiler_params=_params("arbitrary", "arbitrary"),
        name="na_latent",
    )(z, z, z, kc, vc, bias_tab)


def _rope_tables(seqlen):
    half = HEAD_DIM // 2
    t = np.arange(seqlen)
    rows = (t // GRID_W).astype(np.float32)
    cols = (t % GRID_W).astype(np.float32)
    inv = (1.0 / (np.float32(ROPE_THETA) ** (np.arange(0, half, 2, dtype=np.float32) / np.float32(half)))).astype(np.float32)
    ang_r = rows[:, None] * inv[None, :]
    ang_c = cols[:, None] * inv[None, :]
    cos = np.concatenate([np.cos(ang_r), np.cos(ang_r), np.cos(ang_c), np.cos(ang_c)], axis=-1)
    sin = np.concatenate([-np.sin(ang_r), np.sin(ang_r), -np.sin(ang_c), np.sin(ang_c)], axis=-1)
    cos = np.tile(cos.astype(np.float32), (1, N_HEADS))
    sin = np.tile(sin.astype(np.float32), (1, N_HEADS))
    return jnp.asarray(cos), jnp.asarray(sin)


def _rope(x, cos, sin_signed):
    w = x.shape[-1]
    lane = lax.broadcasted_iota(jnp.int32, x.shape, 1)
    partner = jnp.where((lane % 32) < 16, pltpu.roll(x, w - 16, 1), pltpu.roll(x, 16, 1))
    return x * cos + partner * sin_signed


def _swa_lat_kernel(sink_ref, q_ref, k_ref, v_ref, kc_ref, vc_ref, cos_ref, sin_ref, o_ref, *, seqlen):
    blk = SWA_BLOCK
    nwin = 3 * blk
    kvw = SWA_KV_HEADS * HEAD_DIM
    group = N_HEADS // SWA_KV_HEADS
    j = pl.program_id(1)
    q0 = pl.multiple_of(j * blk, blk)
    k0 = pl.multiple_of(jnp.clip((j - 1) * blk, 0, seqlen - nwin), blk)
    q = _rope(q_ref[...], cos_ref[pl.ds(q0, blk), :], sin_ref[pl.ds(q0, blk), :])
    kw = _rope(k_ref[pl.ds(k0, nwin), :], cos_ref[pl.ds(k0, nwin), 0:kvw], sin_ref[pl.ds(k0, nwin), 0:kvw])
    vw = v_ref[pl.ds(k0, nwin), :]
    qpos = q0 + lax.broadcasted_iota(jnp.int32, (group * blk, nwin), 0) % blk
    kpos = k0 + lax.broadcasted_iota(jnp.int32, (group * blk, nwin), 1)
    band = jnp.abs(kpos - qpos) <= SWA_WINDOW
    outs = []
    for hk in range(SWA_KV_HEADS):
        ks = slice(hk * HEAD_DIM, (hk + 1) * HEAD_DIM)
        q2 = jnp.concatenate([q[:, (hk * group + g) * HEAD_DIM:(hk * group + g + 1) * HEAD_DIM] for g in range(group)], axis=0)
        q2 = _bf(q2)
        row_g = lax.broadcasted_iota(jnp.int32, (group * blk, 1), 0) // blk
        sink = jnp.zeros((group * blk, 1), F32)
        for g in range(group):
            sink = jnp.where(row_g == g, sink_ref[hk * group + g], sink)
        s_band = jnp.where(band, _dot_nt(q2, _bf(kw[:, ks])) * ATT_SCALE, -jnp.inf)
        s_ctx = _dot_nt(q2, _bf(kc_ref[0, :, ks])) * ATT_SCALE
        m = jnp.maximum(jnp.maximum(jnp.max(s_band, axis=-1, keepdims=True), jnp.max(s_ctx, axis=-1, keepdims=True)), sink)
        p_band = jnp.exp(s_band - m)
        p_ctx = jnp.exp(s_ctx - m)
        l = jnp.sum(p_band, axis=-1, keepdims=True) + jnp.sum(p_ctx, axis=-1, keepdims=True) + jnp.exp(sink - m)
        acc = (_dot(_bf(p_band), _bf(vw[:, ks])) + _dot(_bf(p_ctx), _bf(vc_ref[0, :, ks]))) / l
        for g in range(group):
            outs.append(acc[g * blk:(g + 1) * blk])
    o_ref[...] = jnp.concatenate(outs, axis=-1)


def _swa_latent(z, kc, vc, sink, batch, seqlen):
    n = batch * seqlen
    nb = seqlen // SWA_BLOCK
    kvw = SWA_KV_HEADS * HEAD_DIM
    n_ctx = kc.shape[1]
    cos, sin = _rope_tables(seqlen)
    return pl.pallas_call(
        functools.partial(_swa_lat_kernel, seqlen=seqlen),
        grid=(batch, nb),
        in_specs=[
            pl.BlockSpec(memory_space=pltpu.SMEM),
            pl.BlockSpec((SWA_BLOCK, BRANCH_W), lambda b, j: (b * nb + j, COL_SQ)),
            pl.BlockSpec((seqlen, kvw), lambda b, j: (b, COL_SK128)),
            pl.BlockSpec((seqlen, kvw), lambda b, j: (b, COL_SV128)),
            pl.BlockSpec((1, n_ctx, kvw), lambda b, j: (b, 0, 0)),
            pl.BlockSpec((1, n_ctx, kvw), lambda b, j: (b, 0, 0)),
            _const_spec(cos.shape),
            _const_spec(sin.shape),
        ],
        out_specs=pl.BlockSpec((SWA_BLOCK, BRANCH_W), lambda b, j: (b * nb + j, 0)),
        out_shape=jax.ShapeDtypeStruct((n, BRANCH_W), F32),
        compiler_params=_params("arbitrary", "arbitrary"),
        name="swa_latent",
    )(sink.astype(F32), z, z, z, kc, vc, cos, sin)


def _smlp_kernel(u_ref, v_ref, ws_ref, bias_ref, ones_ref, o_ref):
    v = v_ref[...]
    ms = _dot01_right(v * v, ones_ref[...]) * (1.0 / HEAD_DIM)
    vn = _bf(v * lax.rsqrt(ms + EPS))
    lane_g = lax.broadcasted_iota(jnp.int32, v.shape, 1) // HEAD_DIM
    mixed = bias_ref[...]
    for g in range(SMLP_GROUPS):
        mixed = mixed + jnp.where(lane_g == g, _dot(ws_ref[g], vn), 0.0)
    o_ref[...] = u_ref[...] * mixed


def _smlp(z, ws, b):
    n = z.shape[0]
    bias = jnp.repeat(b.astype(F32).T, BRANCH_W // SMLP_GROUPS, axis=1)
    return pl.pallas_call(
        _smlp_kernel,
        grid=(n // SMLP_CHUNK,),
        in_specs=[
            pl.BlockSpec((SMLP_CHUNK, BRANCH_W), lambda i: (i, COL_MU)),
            pl.BlockSpec((SMLP_CHUNK, BRANCH_W), lambda i: (i, COL_MV)),
            _const_spec((SMLP_GROUPS, SMLP_CHUNK, SMLP_CHUNK)),
            _const_spec((SMLP_CHUNK, BRANCH_W)),
            _const_spec((BRANCH_W, BRANCH_W)),
        ],
        out_specs=pl.BlockSpec((SMLP_CHUNK, BRANCH_W), lambda i: (i, 0)),
        out_shape=jax.ShapeDtypeStruct((n, BRANCH_W), F32),
        compiler_params=_params("arbitrary"),
        name="smlp",
    )(z, z, _bf(ws), bias, _group_ones())


def _merge_kernel(a_ref, b_ref, c_ref, d_ref, g0_ref, g1_ref, g2_ref, g3_ref, x_ref, gate_ref, sh_ref, sc_ref,
                  ng_ref, wb_ref, wo_ref, wr_ref, br_ref, x1_ref, h_ref, comb_ref):
    mix = None
    for br, gt, i in ((a_ref, g0_ref, 0), (b_ref, g1_ref, 1), (c_ref, g2_ref, 2), (d_ref, g3_ref, 3)):
        t = jax.nn.sigmoid(gt[...]) * _dot(_bf(br[...]), wb_ref[i])
        mix = t if mix is None else mix + t
    x1 = x_ref[...] + gate_ref[0] * _dot(_bf(mix), wo_ref[...])
    x1_ref[...] = x1
    ms = jnp.mean(x1 * x1, axis=-1, keepdims=True)
    h = x1 * lax.rsqrt(ms + EPS) * ng_ref[...]
    h = h * (1.0 + sc_ref[0]) + sh_ref[0]
    h_ref[...] = _bf(h)

    hh, hm, hl = _split3(h)
    wh, wm, wl = wr_ref[0], wr_ref[1], wr_ref[2]
    logits = (_dot(hh, wh) + _dot(hh, wm) + _dot(hm, wh) + _dot(hh, wl) + _dot(hm, wm) + _dot(hl, wh)) + br_ref[...]
    lane_i = lax.broadcasted_iota(jnp.int32, logits.shape, 1)
    lane = lane_i.astype(F32)
    lane_grp = (lane_i // EXPERTS_PER_GROUP).astype(F32)
    neg = -jnp.inf
    far = float(4 * N_EXPERTS)
    is_g = (lane_i >= N_EXPERTS) & (lane_i < N_EXPERTS + N_GROUPS)
    gl = jnp.where(is_g, logits, neg)
    gmax = jnp.max(gl, axis=-1, keepdims=True)
    gsum = jnp.sum(jnp.exp(gl - gmax), axis=-1, keepdims=True)
    g_top_p = 1.0 / gsum
    g_idx = jnp.min(jnp.where(is_g & (gl == gmax), lane, far), axis=-1, keepdims=True) - float(N_EXPERTS)
    in_grp = (lane_i < N_EXPERTS) & (lane_grp == g_idx)
    e_l = jnp.where(in_grp, logits, neg)
    e1 = jnp.max(e_l, axis=-1, keepdims=True)
    i1 = jnp.min(jnp.where(in_grp & (e_l == e1), lane, far), axis=-1, keepdims=True)
    e_l2 = jnp.where(lane == i1, neg, e_l)
    e2 = jnp.max(e_l2, axis=-1, keepdims=True)
    i2 = jnp.min(jnp.where(in_grp & (lane != i1) & (e_l2 == e2), lane, far), axis=-1, keepdims=True)
    t2 = jnp.exp(e2 - e1)
    w1 = g_top_p / (1.0 + t2)
    w2 = w1 * t2
    comb_ref[...] = jnp.where(lane == i1, w1, 0.0) + jnp.where(lane == i2, w2, 0.0)


def _router_tables(w_rg, b_rg, w_re, b_re):
    w = jnp.zeros((D_MODEL, 128), F32)
    w = w.at[:, :N_EXPERTS].set(w_re.astype(F32)).at[:, N_EXPERTS:N_EXPERTS + N_GROUPS].set(w_rg.astype(F32))
    b = jnp.zeros((1, 128), F32)
    b = b.at[0, :N_EXPERTS].set(b_re.astype(F32)).at[0, N_EXPERTS:N_EXPERTS + N_GROUPS].set(b_rg.astype(F32))
    return jnp.stack(_split3(w), axis=0), b


def _merge(branches, z, x, mods3, mod_row, norm_g, w_branch_bf, w_out_bf, wr3, br, tm):
    n = x.shape[0]
    row = lambda w: pl.BlockSpec((tm, w), lambda i: (i, 0))
    gate = lambda k: pl.BlockSpec((tm, D_MODEL), lambda i: (i, COL_GATES1024 + k))
    mod = lambda k: pl.BlockSpec((1, 1, D_MODEL), lambda i: (mod_row(i, tm), 0, k))
    return pl.pallas_call(
        _merge_kernel,
        grid=(n // tm,),
        in_specs=[
            row(BRANCH_W), row(BRANCH_W), row(BRANCH_W), row(BRANCH_W),
            gate(0), gate(1), gate(2), gate(3),
            row(D_MODEL),
            mod(2), mod(3), mod(4),
            _const_spec((1, D_MODEL)),
            _const_spec((N_BRANCH, BRANCH_W, D_MODEL)),
            _const_spec((D_MODEL, D_MODEL)),
            _const_spec((3, D_MODEL, 128)),
            _const_spec((1, 128)),
        ],
        out_specs=[row(D_MODEL), row(D_MODEL), row(128)],
        out_shape=[
            jax.ShapeDtypeStruct((n, D_MODEL), F32),
            jax.ShapeDtypeStruct((n, D_MODEL), BF16),
            jax.ShapeDtypeStruct((n, 128), F32),
        ],
        compiler_params=_params("arbitrary"),
        name="merge",
    )(*branches, z, z, z, z, x, mods3, mods3, mods3, norm_g.reshape(1, D_MODEL), w_branch_bf, w_out_bf, wr3, br)


def _moe_kernel(h_ref, comb_ref, x1_ref, gate_ref, fg_ref, wgu_ref, wd_ref, *out_refs_and_acc, final):
    if final:
        x2_ref, y_ref, acc_ref = out_refs_and_acc
    else:
        x2_ref, acc_ref = out_refs_and_acc
    e = pl.program_id(1)

    @pl.when(e == 0)
    def _():
        acc_ref[...] = jnp.zeros_like(acc_ref)

    gu = _dot(h_ref[...], wgu_ref[0])
    comb = comb_ref[...]
    lane = lax.broadcasted_iota(jnp.int32, comb.shape, 1)
    w = jnp.sum(jnp.where(lane == e, comb, 0.0), axis=-1, keepdims=True)
    a = _silu(gu[:, :EXPERT_FF]) * gu[:, EXPERT_FF:] * w
    acc_ref[...] += _dot(_bf(a), wd_ref[0])

    @pl.when(e == N_EXPERTS - 1)
    def _():
        x2 = x1_ref[...] + gate_ref[0] * acc_ref[...]
        x2_ref[...] = x2
        if final:
            ms = jnp.mean(x2 * x2, axis=-1, keepdims=True)
            y_ref[...] = x2 * lax.rsqrt(ms + EPS) * fg_ref[...]


def _moe(h, comb, x1, mods3, mod_row, final_g, wgu_bf, wd_bf, tm, final):
    n = x1.shape[0]
    row = lambda w: pl.BlockSpec((tm, w), lambda i, e: (i, 0))
    out_specs = [row(D_MODEL)]
    out_shape = [jax.ShapeDtypeStruct((n, D_MODEL), F32)]
    if final:
        out_specs.append(row(D_MODEL))
        out_shape.append(jax.ShapeDtypeStruct((n, D_MODEL), F32))
    return pl.pallas_call(
        functools.partial(_moe_kernel, final=final),
        grid=(n // tm, N_EXPERTS),
        in_specs=[
            row(D_MODEL), row(128), row(D_MODEL),
            pl.BlockSpec((1, 1, D_MODEL), lambda i, e: (mod_row(i, tm), 0, 5)),
            pl.BlockSpec((1, D_MODEL), lambda i, e: (0, 0)),
            pl.BlockSpec((1, D_MODEL, 2 * EXPERT_FF), lambda i, e: (e, 0, 0)),
            pl.BlockSpec((1, EXPERT_FF, D_MODEL), lambda i, e: (e, 0, 0)),
        ],
        out_specs=out_specs,
        out_shape=out_shape,
        scratch_shapes=[pltpu.VMEM((tm, D_MODEL), F32)],
        compiler_params=_params("arbitrary", "arbitrary"),
        name="moe_final" if final else "moe",
    )(h, comb, x1, mods3, final_g.reshape(1, D_MODEL), wgu_bf, wd_bf)


def _state_to_blockdiag_t(s):
    b = s.shape[0]
    st = jnp.swapaxes(s.astype(F32), -1, -2)
    eye = jnp.eye(N_HEADS, dtype=F32)
    full = st[:, :, :, :, None, :] * eye[None, None, :, None, :, None]
    return full.reshape(b, 2, BRANCH_W, BRANCH_W)


def _blockdiag_t_to_state(st):
    b = st.shape[0]
    full = st.reshape(b, 2, N_HEADS, HEAD_DIM, N_HEADS, HEAD_DIM)
    diag = jnp.stack([full[:, :, h, :, h, :] for h in range(N_HEADS)], axis=2)
    return jnp.swapaxes(diag, -1, -2)


def _layer(x, lw, mods3, mod_row, batch, seqlen, latent, caches, final, final_g):
    tm = 1024
    z = _in_proj(x, mods3, mod_row, lw["norm1_g"], lw["w_in"], tm)
    o_f, o_b, st = _hgrn_scan(z, lw["lb"], lw["st0"], batch, seqlen, n_sub=4)
    a_out = _hgrn_post(o_f, o_b, z, lw["onorm_g"], tm)
    c_out = _smlp(z, lw["smlp_ws"], lw["smlp_b"])
    if latent:
        ck_na, cv_na, ck_swa, cv_swa = caches
        b_out = _na_latent(z, ck_na, cv_na, lw["na_bias"], batch, seqlen)
        d_out = _swa_latent(z, ck_swa, cv_swa, lw["swa_sink"], batch, seqlen)
    else:
        b_out = _ctx_attn(z, batch, seqlen, COL_NAQ, COL_NAK, COL_NAV, BRANCH_W, N_HEADS, None)
        d_out = _ctx_attn(z, batch, seqlen, COL_SQ, COL_SK128, COL_SV128, SWA_KV_HEADS * HEAD_DIM, SWA_KV_HEADS,
                          lw["swa_sink"])
    x1, h2, comb = _merge((a_out, b_out, c_out, d_out), z, x, mods3, mod_row, lw["norm2_g"], lw["w_branch"],
                          lw["w_out"], lw["wr3"], lw["br"], 512)
    out = _moe(h2, comb, x1, mods3, mod_row, final_g, lw["wgu"], lw["wd"], tm, final)
    return out, z, st


def kernel(x_prompt, x_sample, c, cache_na_k, cache_na_v, cache_swa_k, cache_swa_v, state_hgrn, c_ctx, w_ada, b_ada, norm1_g, norm2_g, w_in, hgrn_lb, hgrn_onorm_g, na_rpb, smlp_ws, smlp_b, swa_sink, w_branch, w_out, router_g_w, router_g_b, router_e_w, router_e_b, moe_w_gate, moe_w_up, moe_w_down, final_g):
    bc, lc, _ = x_prompt.shape
    bl, ll, _ = x_sample.shape
    n_ctx_tok = bc * lc

    cond = jnp.zeros((MOD_ROWS, D_MODEL), F32).at[0].set(c_ctx.astype(F32)).at[1:1 + bl].set(c.astype(F32))
    mods = _ada_mods(cond, w_ada, b_ada)

    lb_soft = jax.nn.softmax(hgrn_lb.astype(F32), axis=0)
    lb_all = jnp.cumsum(lb_soft, axis=0) - lb_soft[0:1]

    def layer_weights(l):
        wr3, br = _router_tables(router_g_w[l], router_g_b[l], router_e_w[l], router_e_b[l])
        return dict(
            norm1_g=norm1_g[l], norm2_g=norm2_g[l], w_in=_bf(w_in[l]), lb=lb_all[l], onorm_g=hgrn_onorm_g[l],
            smlp_ws=smlp_ws[l], smlp_b=smlp_b[l], swa_sink=swa_sink[l], w_branch=_bf(w_branch[l]), w_out=_bf(w_out[l]),
            wr3=wr3, br=br, wgu=_bf(jnp.concatenate([moe_w_gate[l], moe_w_up[l]], axis=-1)), wd=_bf(moe_w_down[l]),
            na_bias=_na_bias_table(na_rpb[l], ll // GRID_W),
        )

    lws = [layer_weights(l) for l in range(DEPTH)]

    ctx_row = lambda i, tm: 0
    lat_row = lambda i, tm: 1 + i // (ll // tm)
    xp = x_prompt.reshape(n_ctx_tok, D_MODEL)
    na_ks, na_vs, swa_ks, swa_vs, states = [], [], [], [], []
    y_prompt = None
    for l in range(DEPTH):
        lw = dict(lws[l], st0=jnp.zeros((bc, 2, BRANCH_W, BRANCH_W), F32))
        final = l == DEPTH - 1
        out, z, st = _layer(xp, lw, mods[l].reshape(MOD_ROWS, 1, -1), ctx_row, bc, lc, False, None, final, final_g)
        if final:
            xp, y_prompt = out
        else:
            xp = out[0]
        na_ks.append(z[:, COL_NAK * 256:(COL_NAK + 1) * 256].reshape(bc, lc, N_HEADS, HEAD_DIM))
        na_vs.append(z[:, COL_NAV * 256:(COL_NAV + 1) * 256].reshape(bc, lc, N_HEADS, HEAD_DIM))
        swa_ks.append(z[:, COL_SK128 * 128:(COL_SK128 + 1) * 128].reshape(bc, lc, SWA_KV_HEADS, HEAD_DIM))
        swa_vs.append(z[:, COL_SV128 * 128:(COL_SV128 + 1) * 128].reshape(bc, lc, SWA_KV_HEADS, HEAD_DIM))
        states.append(_blockdiag_t_to_state(st))

    xs =x_sample.reshape(bl * ll, D_MODEL)
    y_sample = None
    n_past = cache_na_k.shape[2]
    for l in range(DEPTH):
        lw = dict(lws[l], st0=_state_to_blockdiag_t(state_hgrn[:, l]))
        caches = (cache_na_k[:, l].reshape(bl, n_past, BRANCH_W), cache_na_v[:, l].reshape(bl, n_past, BRANCH_W),
                  cache_swa_k[:, l].reshape(bl, n_past, SWA_KV_HEADS * HEAD_DIM),
                  cache_swa_v[:, l].reshape(bl, n_past, SWA_KV_HEADS * HEAD_DIM))
        final = l == DEPTH - 1
        out, _, _ = _layer(xs, lw, mods[l].reshape(MOD_ROWS, 1, -1), lat_row, bl, ll, True, caches, final, final_g)
        if final:
            xs, y_sample = out
        else:
            xs = out[0]

    return (y_prompt.reshape(bc, lc, D_MODEL), y_sample.reshape(bl, ll, D_MODEL),
            jnp.stack(na_ks, axis=1), jnp.stack(na_vs, axis=1), jnp.stack(swa_ks, axis=1), jnp.stack(swa_vs, axis=1),
            jnp.stack(states, axis=1))
```

```python
import functools

import numpy as np
import jax
import jax.numpy as jnp
from jax import lax
from jax.experimental import pallas as pl
from jax.experimental.pallas import tpu as pltpu

D_MODEL = 1024
DEPTH = 2
GRID_W = 64
HEAD_DIM = 64
N_BRANCH = 4
BRANCH_W = 256
N_HEADS = 4
HGRN_CHUNK = 64
NA_ROWS = 8
NA_COLS = 16
SMLP_GROUPS = 4
SMLP_CHUNK = 128
SWA_KV_HEADS = 2
SWA_WINDOW = 128
SWA_BLOCK = 128
ROPE_THETA = 10000.0
N_GROUPS = 4
EXPERTS_PER_GROUP = 4
N_EXPERTS = 16
EXPERT_FF = 256
ADA_CHUNKS = 6
EPS = 1e-6
TINY = 1e-30
P_IN = 7168
ATT_SCALE = HEAD_DIM ** -0.5

ZF_W = 1024
ZR_W = P_IN - ZF_W
COL_HQ, COL_HI, COL_HFF, COL_HFB = 0, 1, 2, 3
COL_HOG, COL_NAQ, COL_NAK, COL_NAV = 0, 1, 2, 3
COL_MU, COL_MV, COL_SQ = 4, 5, 6
COL_SK128, COL_SV128 = 14, 15
COL_GATES1024 = 2
KV_W = 768

MOD_ROWS = 16
VMEM_LIMIT = 56 * 1024 * 1024

F32 = jnp.float32
BF16 = jnp.bfloat16
HGRN_LEVELS = (0, 1, 2, 4, 8, 16, 32)
SUBLANES = 8
HGRN_MXU_REF_LEVELS = ()


def _bf(x):
    return x.astype(BF16)


def _dot(a, b):
    return jnp.dot(a, b, preferred_element_type=F32)


def _dot_nt(a, b):
    return lax.dot_general(a, b, (((1,), (1,)), ((), ())), preferred_element_type=F32)


def _dot_tn(a, b):
    return lax.dot_general(a, b, (((0,), (0,)), ((), ())), preferred_element_type=F32)


def _split3(x):
    hi = _bf(x)
    r1 = x - hi.astype(F32)
    mid = _bf(r1)
    lo = _bf(r1 - mid.astype(F32))
    return hi, mid, lo


def _dot01_left(m01, x):
    hi, mid, lo = _split3(x)
    return _dot(m01, hi) + _dot(m01, mid) + _dot(m01, lo)


def _dot01_right(x, m01):
    hi, mid, lo = _split3(x)
    return _dot(hi, m01) + _dot(mid, m01) + _dot(lo, m01)


def _sigmoid(x):
    return 0.5 * jnp.tanh(0.5 * x) + 0.5


def _silu(x):
    return x * _sigmoid(x)


def _params(*sem):
    return pltpu.CompilerParams(dimension_semantics=sem, vmem_limit_bytes=VMEM_LIMIT)


def _const_spec(shape):
    n = len(shape)
    return pl.BlockSpec(shape, lambda *_: (0,) * n)


def _head_mask():
    row = lax.broadcasted_iota(jnp.int32, (N_HEADS * HEAD_DIM, BRANCH_W), 0)
    lane = lax.broadcasted_iota(jnp.int32, (N_HEADS * HEAD_DIM, BRANCH_W), 1)
    return (row // HEAD_DIM) == (lane // HEAD_DIM)


def _expand_heads(x, head_mask):
    return jnp.where(head_mask, jnp.concatenate([x] * N_HEADS, axis=0), jnp.zeros((), x.dtype))


def _collapse_heads(r, head_mask):
    r = jnp.where(head_mask, r, 0.0)
    n = HEAD_DIM
    return (r[0:n] + r[n:2 * n]) + (r[2 * n:3 * n] + r[3 * n:4 * n])


def _ada_kernel(cond_ref, w_ref, b_ref, o_ref):
    s = _silu(cond_ref[...])
    o_ref[0] = _dot(_bf(s), _bf(w_ref[0])) + b_ref[0]


def _ada_mods(cond, w_ada, b_ada):
    tn = 1536
    n = ADA_CHUNKS * D_MODEL
    return pl.pallas_call(
        _ada_kernel,
        grid=(DEPTH, n // tn),
        in_specs=[
            pl.BlockSpec((MOD_ROWS, D_MODEL), lambda l, j: (0, 0)),
            pl.BlockSpec((1, D_MODEL, tn), lambda l, j: (l, 0, j)),
            pl.BlockSpec((1, 1, tn), lambda l, j: (l, 0, j)),
        ],
        out_specs=pl.BlockSpec((1, MOD_ROWS, tn), lambda l, j: (l, 0, j)),
        out_shape=jax.ShapeDtypeStruct((DEPTH, MOD_ROWS, n), F32),
        compiler_params=_params("arbitrary", "arbitrary"),
        name="ada_mods",
    )(cond, w_ada, b_ada.reshape(DEPTH, 1, n))


def _in_kernel(x_ref, sh_ref, sc_ref, g_ref, w_ref, zf_ref, zr_ref, *rest, want_kv):
    if want_kv:
        kv_ref, h_ref = rest
    else:
        (h_ref,) = rest
    j = pl.program_id(1)

    @pl.when(j == 0)
    def _():
        x = x_ref[...]
        ms = jnp.mean(x * x, axis=-1, keepdims=True)
        h = x * lax.rsqrt(ms + EPS) * g_ref[...]
        h_ref[...] = _bf(h * (1.0 + sc_ref[0]) + sh_ref[0])

    acc = _dot(h_ref[...], w_ref[...])

    @pl.when(j == 0)
    def _():
        zf_ref[...] = acc

    @pl.when(j > 0)
    def _():
        zr_ref[...] = _bf(acc)

    if want_kv:
        @pl.when(j == 1)
        def _():
            kv_ref[:, 0:512] = acc[:, 512:1024]

        @pl.when(j == 2)
        def _():
            kv_ref[:, 512:768] = acc[:, 768:1024]


def _in_proj(x, mods3, mod_row, norm_g, w_in_bf, tm, want_kv):
    n = x.shape[0]
    tn = 1024
    out_specs = [
        pl.BlockSpec((tm, tn), lambda i, j: (i, 0)),
        pl.BlockSpec((tm, tn), lambda i, j: (i, jnp.maximum(j - 1, 0))),
    ]
    out_shape = [jax.ShapeDtypeStruct((n, ZF_W), F32), jax.ShapeDtypeStruct((n, ZR_W), BF16)]
    if want_kv:
        out_specs.append(pl.BlockSpec((tm, KV_W), lambda i, j: (i, 0)))
        out_shape.append(jax.ShapeDtypeStruct((n, KV_W), F32))
    return pl.pallas_call(
        functools.partial(_in_kernel, want_kv=want_kv),
        grid=(n // tm, P_IN // tn),
        in_specs=[
            pl.BlockSpec((tm, D_MODEL), lambda i, j: (i, 0)),
            pl.BlockSpec((1, 1, D_MODEL), lambda i, j: (mod_row(i, tm), 0, 0)),
            pl.BlockSpec((1, 1, D_MODEL), lambda i, j: (mod_row(i, tm), 0, 1)),
            pl.BlockSpec((1, D_MODEL), lambda i, j: (0, 0)),
            pl.BlockSpec((D_MODEL, tn), lambda i, j: (0, j)),
        ],
        out_specs=out_specs,
        out_shape=out_shape,
        scratch_shapes=[pltpu.VMEM((tm, D_MODEL), BF16)],
        compiler_params=_params("arbitrary", "arbitrary"),
        name="in_proj_kv" if want_kv else "in_proj",
    )(x, mods3, mods3, norm_g.reshape(1, D_MODEL), w_in_bf)


def _hgrn_tables():
    c = HGRN_CHUNK
    t = np.arange(c)
    tsel = np.zeros((2, (1 + len(HGRN_MXU_REF_LEVELS)) * c, c), np.float32)
    pmask = np.zeros((2, len(HGRN_LEVELS), c, N_HEADS * c), np.float32)
    for rev in (0, 1):
        cum = (t[None, :] >= t[:, None]) if rev else (t[None, :] <= t[:, None])
        tsel[rev, :c] = cum
        pmask[rev, 0] = np.tile(np.eye(c, dtype=np.float32), (1, N_HEADS))
        for li, m in enumerate(HGRN_LEVELS[1:], start=1):
            if m in HGRN_MXU_REF_LEVELS:
                slot = 1 + HGRN_MXU_REF_LEVELS.index(m)
                tsel[rev, slot * c:(slot + 1) * c] = cum[(t // (2 * m)) * (2 * m) + (m - 1 if rev else m)]
            same = (t[:, None] // (2 * m)) == (t[None, :] // (2 * m))
            q_half = ((t & m) == 0) if rev else ((t & m) != 0)
            k_half = ~q_half
            pmask[rev, li] = np.tile((same & q_half[:, None] & k_half[None, :]).astype(np.float32), (1, N_HEADS))
    return jnp.asarray(tsel, BF16), jnp.asarray(pmask, F32)


def _ref_rows(b_ref, m, rev):
    c = HGRN_CHUNK
    off = (m - 1) if rev else m
    row = lambda r, n: jnp.broadcast_to(b_ref[pl.ds(r, 1), :], (n, BRANCH_W))
    if 2 * m >= SUBLANES:
        return jnp.concatenate([row(s + off, 2 * m) for s in range(0, c, 2 * m)], axis=0)
    sub = lax.broadcasted_iota(jnp.int32, (SUBLANES, BRANCH_W), 0)
    tiles = []
    for t0 in range(0, c, SUBLANES):
        cur = row(t0 + off, SUBLANES)
        for s in range(2 * m, SUBLANES, 2 * m):
            cur = jnp.where(sub >= s, row(t0 + s + off, SUBLANES), cur)
        tiles.append(cur)
    return jnp.concatenate(tiles, axis=0)


def _hgrn_chunk(q_raw, v, f_raw, lb, st, tsel, pmask_ref, rev, head_mask, b_ref):
    c = HGRN_CHUNK
    qq = _silu(q_raw)
    f = lb + (1.0 - lb) * _sigmoid(f_raw)
    lf = jnp.log(jnp.maximum(f, TINY))
    k = 1.0 - f
    ball = _dot01_left(tsel, lf)
    b = ball[:c]
    b_ref[...] = b
    bl = b[0:1] if rev else b[c - 1:c]
    vb = _bf(v)
    v_x = _expand_heads(vb, head_mask)

    o = _dot_nt(_bf(qq * jnp.exp(b)), _bf(st))

    p = None
    qb, kb = _bf(qq), _bf(k)
    for li, m in enumerate(HGRN_LEVELS):
        if m == 0:
            qe, ke = qb, kb
        else:
            if m in HGRN_MXU_REF_LEVELS:
                slot = 1 + HGRN_MXU_REF_LEVELS.index(m)
                ref = ball[slot * c:(slot + 1) * c]
            else:
                ref = _ref_rows(b_ref, m, rev)
            e = _bf(jnp.exp(-jnp.abs(b - ref)))
            qe, ke = qb * e, kb * e
        s = _dot_nt(qe, _expand_heads(ke, head_mask)) * pmask_ref[li]
        p = s if p is None else p + s
    o = o + _dot(_bf(p), v_x)

    ke_state = _bf(k * jnp.exp(bl - b))
    st_new = st * jnp.exp(bl) + jnp.where(head_mask, _dot_tn(vb, ke_state), 0.0)
    return o, st_new


def _hgrn_kernel(qf_ref, vf_ref, ff_ref, qb_ref, vb_ref, fb_ref, lb_ref, st0_ref, tsel_ref, pmask_ref,
                 of_ref, ob_ref, st_ref, b_scr, *, n_sub):
    c = HGRN_CHUNK

    @pl.when(pl.program_id(1) == 0)
    def _():
        st_ref[...] = st0_ref[...]

    head_mask = _head_mask()
    st_f = st_ref[0, 0]
    st_b = st_ref[0, 1]
    for j in range(n_sub):
        rf = slice(j * c, (j + 1) * c)
        rb = slice((n_sub - 1 - j) * c, (n_sub - j) * c)
        o_f, st_f = _hgrn_chunk(qf_ref[rf, :], vf_ref[rf, :], ff_ref[rf, :], lb_ref[0:1, :], st_f,
                                tsel_ref[0], pmask_ref.at[0], False, head_mask, b_scr.at[2 * j])
        o_b, st_b = _hgrn_chunk(qb_ref[rb, :], vb_ref[rb, :], fb_ref[rb, :], lb_ref[1:2, :], st_b,
                                tsel_ref[1], pmask_ref.at[1], True, head_mask, b_scr.at[2 * j + 1])
        of_ref[rf, :] = o_f
        ob_ref[rb, :] = o_b
    st_ref[0, 0] = st_f
    st_ref[0, 1] = st_b


def _hgrn_scan(zf, lb2, st0, batch, seqlen, n_sub):
    n = batch * seqlen
    tb = n_sub * HGRN_CHUNK
    nblk = seqlen // tb
    tsel, pmask = _hgrn_tables()
    fwd = lambda col: pl.BlockSpec((tb, BRANCH_W), lambda b, c: (b * nblk + c, col))
    bwd = lambda col: pl.BlockSpec((tb, BRANCH_W), lambda b, c: (b * nblk + nblk - 1 - c, col))
    return pl.pallas_call(
        functools.partial(_hgrn_kernel, n_sub=n_sub),
        grid=(batch, nblk),
        in_specs=[
            fwd(COL_HQ), fwd(COL_HI), fwd(COL_HFF),
            bwd(COL_HQ), bwd(COL_HI), bwd(COL_HFB),
            _const_spec((2, BRANCH_W)),
            pl.BlockSpec((1, 2, BRANCH_W, BRANCH_W), lambda b, c: (b, 0, 0, 0)),
            _const_spec(tsel.shape),
            _const_spec(pmask.shape),
        ],
        out_specs=[
            pl.BlockSpec((tb, BRANCH_W), lambda b, c: (b * nblk + c, 0)),
            pl.BlockSpec((tb, BRANCH_W), lambda b, c: (b * nblk + nblk - 1 - c, 0)),
            pl.BlockSpec((1, 2, BRANCH_W, BRANCH_W), lambda b, c: (b, 0, 0, 0)),
        ],
        out_shape=[
            jax.ShapeDtypeStruct((n, BRANCH_W), F32),
            jax.ShapeDtypeStruct((n, BRANCH_W), F32),
            jax.ShapeDtypeStruct((batch, 2, BRANCH_W, BRANCH_W), F32),
        ],
        scratch_shapes=[pltpu.VMEM((2 * n_sub, HGRN_CHUNK, BRANCH_W), F32)],
        compiler_params=_params("arbitrary", "arbitrary"),
        name="hgrn_scan",
    )(zf, zf, zf, zf, zf, zf, lb2, st0, tsel, pmask)


def _group_ones():
    g = np.arange(BRANCH_W) // HEAD_DIM
    return jnp.asarray((g[:, None] == g[None, :]).astype(np.float32), BF16)


def _hgrn_post_kernel(of_ref, ob_ref, og_ref, g_ref, ones_ref, a_ref):
    o = of_ref[...] + ob_ref[...]
    ms = _dot01_right(o * o, ones_ref[...]) * (1.0 / HEAD_DIM)
    a_ref[...] = _bf(o * lax.rsqrt(ms + EPS) * g_ref[...] * _silu(og_ref[...].astype(F32)))


def _hgrn_post(o_f, o_b, zr, onorm_g, tm):
    n = o_f.shape[0]
    return pl.pallas_call(
        _hgrn_post_kernel,
        grid=(n // tm,),
        in_specs=[
            pl.BlockSpec((tm, BRANCH_W), lambda i: (i, 0)),
            pl.BlockSpec((tm, BRANCH_W), lambda i: (i, 0)),
            pl.BlockSpec((tm, BRANCH_W), lambda i: (i, COL_HOG)),
            _const_spec((1, BRANCH_W)),
            _const_spec((BRANCH_W, BRANCH_W)),
        ],
        out_specs=pl.BlockSpec((tm, BRANCH_W), lambda i: (i, 0)),
        out_shape=jax.ShapeDtypeStruct((n, BRANCH_W), BF16),
        compiler_params=_params("arbitrary"),
        name="hgrn_post",
    )(o_f, o_b, zr, onorm_g.reshape(1, BRANCH_W), _group_ones())


def _ctx_attn_kernel(*refs, n_q, n_kv, has_sink):
    if has_sink:
        sink_ref, q_ref, k_ref, v_ref, o_ref = refs
    else:
        q_ref, k_ref, v_ref, o_ref = refs
    group = n_q // n_kv
    scale = jnp.asarray(ATT_SCALE, BF16)
    for hq in range(n_q):
        hk = hq // group
        q = q_ref[:, hq * HEAD_DIM:(hq + 1) * HEAD_DIM] * scale
        k = k_ref[:, hk * HEAD_DIM:(hk + 1) * HEAD_DIM]
        v = v_ref[:, hk * HEAD_DIM:(hk + 1) * HEAD_DIM]
        s = _dot_nt(q, k)
        m = jnp.max(s, axis=-1, keepdims=True)
        if has_sink:
            sink = sink_ref[hq]
            m = jnp.maximum(m, sink)
        p = jnp.exp(s - m)
        l = jnp.sum(p, axis=-1, keepdims=True)
        if has_sink:
            l = l + jnp.exp(sink - m)
        o_ref[:, hq * HEAD_DIM:(hq + 1) * HEAD_DIM] = _bf(_dot(_bf(p), v) / l)


def _ctx_attn(zr, batch, seqlen, q_col, k_col, v_col, kv_width, n_kv, sink):
    n = batch * seqlen
    n_q = N_HEADS
    has_sink = sink is not None
    in_specs = [
        pl.BlockSpec((seqlen, BRANCH_W), lambda b: (b, q_col)),
        pl.BlockSpec((seqlen, kv_width), lambda b: (b, k_col)),
        pl.BlockSpec((seqlen, kv_width), lambda b: (b, v_col)),
    ]
    args = [zr, zr, zr]
    if has_sink:
        in_specs = [pl.BlockSpec(memory_space=pltpu.SMEM)] + in_specs
        args = [sink.astype(F32)] + args
    return pl.pallas_call(
        functools.partial(_ctx_attn_kernel, n_q=n_q, n_kv=n_kv, has_sink=has_sink),
        grid=(batch,),
        in_specs=in_specs,
        out_specs=pl.BlockSpec((seqlen, BRANCH_W), lambda b: (b, 0)),
        out_shape=jax.ShapeDtypeStruct((n, BRANCH_W), BF16),
        compiler_params=_params("arbitrary"),
        name="ctx_attn_sink" if has_sink else "ctx_attn",
    )(*args)


def _na_bias_kernel(rpb_ref, onehot_ref, mask_ref, o_ref):
    o_ref[...] = _dot01_right(rpb_ref[...], onehot_ref[...]) + mask_ref[...]


def _na_bias_tables(na_rpb):
    n_dr, n_dc = 2 * NA_ROWS - 1, 2 * NA_COLS - 1
    col = np.arange(GRID_W)
    col_start = np.clip(col - NA_COLS // 2, 0, GRID_W - NA_COLS)
    col_mask = (col[None, :] >= col_start[:, None]) & (col[None, :] < col_start[:, None] + NA_COLS)
    d_col = np.clip(col[None, :] - col[:, None], -(NA_COLS - 1), NA_COLS - 1) + (NA_COLS - 1)
    onehot = (np.arange(128)[:, None] == d_col.reshape(1, -1)).astype(np.float32)
    mask_add = np.where(col_mask.reshape(1, -1), 0.0, -np.inf).astype(np.float32)
    rows = DEPTH * N_HEADS * n_dr
    rpb2 = jnp.zeros((128, 128), F32).at[:rows, :n_dc].set(na_rpb.astype(F32).reshape(rows, n_dc))
    tab = pl.pallas_call(
        _na_bias_kernel,
        grid=(1,),
        in_specs=[_const_spec((128, 128)), _const_spec((128, GRID_W * GRID_W)), _const_spec((1, GRID_W * GRID_W))],
        out_specs=_const_spec((128, GRID_W * GRID_W)),
        out_shape=jax.ShapeDtypeStruct((128, GRID_W * GRID_W), F32),
        compiler_params=_params("arbitrary"),
        name="na_bias",
    )(rpb2, jnp.asarray(onehot, BF16), jnp.asarray(mask_add))
    tab = tab[:rows].reshape(DEPTH, N_HEADS, n_dr, GRID_W, GRID_W)
    slabs = [jnp.transpose(tab[:, :, first:first + NA_ROWS], (0, 1, 3, 2, 4)).reshape(DEPTH, N_HEADS * GRID_W, NA_ROWS * GRID_W)
             for first in range(NA_ROWS)]
    return jnp.stack(slabs, axis=1)


def _na_lat_kernel(q_ref, k_ref, v_ref, kc_ref, vc_ref, bias_ref, o_ref, *, rows_per_step, n_rows):
    nk = NA_ROWS * GRID_W
    r0 = pl.program_id(1) * rows_per_step
    head_mask = _head_mask()
    scale = jnp.asarray(ATT_SCALE, BF16)

    def body(i, carry):
        r = r0 + i
        row_start = jnp.clip(r - NA_ROWS // 2, 0, n_rows - NA_ROWS)
        first = row_start - r + (NA_ROWS - 1)
        k0 = pl.multiple_of(row_start * GRID_W, GRID_W)
        q0 = pl.multiple_of(i * GRID_W, GRID_W)
        qx = _expand_heads(q_ref[pl.ds(q0, GRID_W), :] * scale, head_mask)
        s_lat = _dot_nt(qx, k_ref[pl.ds(k0, nk), :]) + bias_ref[first]
        s_ctx = _dot_nt(qx, kc_ref[0])
        m = jnp.maximum(jnp.max(s_lat, axis=-1, keepdims=True), jnp.max(s_ctx, axis=-1, keepdims=True))
        p_lat = jnp.exp(s_lat - m)
        p_ctx = jnp.exp(s_ctx - m)
        l = jnp.sum(p_lat, axis=-1, keepdims=True) + jnp.sum(p_ctx, axis=-1, keepdims=True)
        acc = _dot(_bf(p_lat), v_ref[pl.ds(k0, nk), :]) + _dot(_bf(p_ctx), vc_ref[0])
        o_ref[pl.ds(q0, GRID_W), :] = _bf(_collapse_heads(acc / l, head_mask))
        return carry

    lax.fori_loop(0, rows_per_step, body, 0, unroll=2)


def _na_latent(zr, kc, vc, bias_tab, batch, seqlen):
    n = batch * seqlen
    n_rows = seqlen // GRID_W
    assert n_rows >= NA_ROWS
    rows_per_step = 8
    steps = n_rows // rows_per_step
    tq = rows_per_step * GRID_W
    n_ctx = kc.shape[1]
    return pl.pallas_call(
        functools.partial(_na_lat_kernel, rows_per_step=rows_per_step, n_rows=n_rows),
        grid=(batch, steps),
        in_specs=[
            pl.BlockSpec((tq, BRANCH_W), lambda b, j: (b * steps + j, COL_NAQ)),
            pl.BlockSpec((seqlen, BRANCH_W), lambda b, j: (b, COL_NAK)),
            pl.BlockSpec((seqlen, BRANCH_W), lambda b, j: (b, COL_NAV)),
            pl.BlockSpec((1, n_ctx, BRANCH_W), lambda b, j: (b, 0, 0)),
            pl.BlockSpec((1, n_ctx, BRANCH_W), lambda b, j: (b, 0, 0)),
            _const_spec(bias_tab.shape),
        ],
        out_specs=pl.BlockSpec((tq, BRANCH_W), lambda b, j: (b * steps + j, 0)),
        out_shape=jax.ShapeDtypeStruct((n, BRANCH_W), BF16),
        compiler_params=_params("arbitrary", "arbitrary"),
        name="na_latent",
    )(zr, zr, zr, kc, vc, bias_tab)


def _rope_tables(seqlen):
    half = HEAD_DIM // 2
    t = np.arange(seqlen)
    rows = (t // GRID_W).astype(np.float32)
    cols = (t % GRID_W).astype(np.float32)
    inv = (1.0 / (np.float32(ROPE_THETA) ** (np.arange(0, half, 2, dtype=np.float32) / np.float32(half)))).astype(np.float32)
    ang_r = rows[:, None] * inv[None, :]
    ang_c = cols[:, None] * inv[None, :]
    cos = np.concatenate([np.cos(ang_r), np.cos(ang_r), np.cos(ang_c), np.cos(ang_c)], axis=-1)
    sin = np.concatenate([-np.sin(ang_r), np.sin(ang_r), -np.sin(ang_c), np.sin(ang_c)], axis=-1)
    cos = np.tile(cos.astype(np.float32), (1, N_HEADS))
    sin = np.tile(sin.astype(np.float32), (1, N_HEADS))
    return jnp.asarray(cos), jnp.asarray(sin)


def _rope(x, cos, sin_signed):
    w = x.shape[-1]
    lane = lax.broadcasted_iota(jnp.int32, x.shape, 1)
    partner = jnp.where((lane % 32) < 16, pltpu.roll(x, w - 16, 1), pltpu.roll(x, 16, 1))
    return x * cos + partner * sin_signed


def _swa_lat_kernel(sink_ref, q_ref, k_ref, v_ref, kc_ref, vc_ref, cos_ref, sin_ref, o_ref, *, seqlen):
    blk = SWA_BLOCK
    nwin = 3 * blk
    kvw = SWA_KV_HEADS * HEAD_DIM
    group = N_HEADS // SWA_KV_HEADS
    j = pl.program_id(1)
    q0 = pl.multiple_of(j * blk, blk)
    k0 = pl.multiple_of(jnp.clip((j - 1) * blk, 0, seqlen - nwin), blk)
    q = _rope(q_ref[...].astype(F32), cos_ref[pl.ds(q0, blk), :], sin_ref[pl.ds(q0, blk), :]) * ATT_SCALE
    kw = _rope(k_ref[pl.ds(k0, nwin), :].astype(F32), cos_ref[pl.ds(k0, nwin), 0:kvw], sin_ref[pl.ds(k0, nwin), 0:kvw])
    kw = _bf(kw)
    vw = v_ref[pl.ds(k0, nwin), :]
    qpos = q0 + lax.broadcasted_iota(jnp.int32, (group * blk, nwin), 0) % blk
    kpos = k0 + lax.broadcasted_iota(jnp.int32, (group * blk, nwin), 1)
    band = jnp.abs(kpos - qpos) <= SWA_WINDOW
    outs = []
    for hk in range(SWA_KV_HEADS):
        ks = slice(hk * HEAD_DIM, (hk + 1) * HEAD_DIM)
        q2 = jnp.concatenate([q[:, (hk * group + g) * HEAD_DIM:(hk * group + g + 1) * HEAD_DIM] for g in range(group)], axis=0)
        q2 = _bf(q2)
        row_g = lax.broadcasted_iota(jnp.int32, (group * blk, 1), 0) // blk
        sink = jnp.zeros((group * blk, 1), F32)
        for g in range(group):
            sink = jnp.where(row_g == g, sink_ref[hk * group + g], sink)
        s_band = jnp.where(band, _dot_nt(q2, kw[:, ks]), -jnp.inf)
        s_ctx = _dot_nt(q2, kc_ref[0, :, ks])
        m = jnp.maximum(jnp.maximum(jnp.max(s_band, axis=-1, keepdims=True), jnp.max(s_ctx, axis=-1, keepdims=True)), sink)
        p_band = jnp.exp(s_band - m)
        p_ctx = jnp.exp(s_ctx - m)
        l = jnp.sum(p_band, axis=-1, keepdims=True) + jnp.sum(p_ctx, axis=-1, keepdims=True) + jnp.exp(sink - m)
        acc = (_dot(_bf(p_band), vw[:, ks]) + _dot(_bf(p_ctx), vc_ref[0, :, ks])) / l
        for g in range(group):
            outs.append(acc[g * blk:(g + 1) * blk])
    o_ref[...] = _bf(jnp.concatenate(outs, axis=-1))


def _swa_latent(zr, kc, vc, sink, batch, seqlen):
    n = batch * seqlen
    nb = seqlen // SWA_BLOCK
    kvw = SWA_KV_HEADS * HEAD_DIM
    n_ctx = kc.shape[1]
    cos, sin = _rope_tables(seqlen)
    return pl.pallas_call(
        functools.partial(_swa_lat_kernel, seqlen=seqlen),
        grid=(batch, nb),
        in_specs=[
            pl.BlockSpec(memory_space=pltpu.SMEM),
            pl.BlockSpec((SWA_BLOCK, BRANCH_W), lambda b, j: (b * nb + j, COL_SQ)),
            pl.BlockSpec((seqlen, kvw), lambda b, j: (b, COL_SK128)),
            pl.BlockSpec((seqlen, kvw), lambda b, j: (b, COL_SV128)),
            pl.BlockSpec((1, n_ctx, kvw), lambda b, j: (b, 0, 0)),
            pl.BlockSpec((1, n_ctx, kvw), lambda b, j: (b, 0, 0)),
            _const_spec(cos.shape),
            _const_spec(sin.shape),
        ],
        out_specs=pl.BlockSpec((SWA_BLOCK, BRANCH_W), lambda b, j: (b * nb + j, 0)),
        out_shape=jax.ShapeDtypeStruct((n, BRANCH_W), BF16),
        compiler_params=_params("arbitrary", "arbitrary"),
        name="swa_latent",
    )(sink.astype(F32), zr, zr, zr, kc, vc, cos, sin)


def _smlp_kernel(u_ref, v_ref, ws_ref, bias_ref, ones_ref, o_ref, *, n_chunks):
    v = v_ref[...].astype(F32)
    ms = _dot01_right(v * v, ones_ref[...]) * (1.0 / HEAD_DIM)
    vn = _bf(v * lax.rsqrt(ms + EPS))
    lane_g = lax.broadcasted_iota(jnp.int32, (SMLP_CHUNK, BRANCH_W), 1) // HEAD_DIM
    for ci in range(n_chunks):
        rows = slice(ci * SMLP_CHUNK, (ci + 1) * SMLP_CHUNK)
        mixed = bias_ref[...]
        for g in range(SMLP_GROUPS):
            mixed = mixed + jnp.where(lane_g == g, _dot(ws_ref[g], vn[rows]), 0.0)
        o_ref[rows, :] = _bf(u_ref[rows, :].astype(F32) * mixed)


def _smlp(zr, ws, b, n_chunks):
    n = zr.shape[0]
    tm = n_chunks * SMLP_CHUNK
    bias = jnp.repeat(b.astype(F32).T, BRANCH_W // SMLP_GROUPS, axis=1)
    return pl.pallas_call(
        functools.partial(_smlp_kernel, n_chunks=n_chunks),
        grid=(n // tm,),
        in_specs=[
            pl.BlockSpec((tm, BRANCH_W), lambda i: (i, COL_MU)),
            pl.BlockSpec((tm, BRANCH_W), lambda i: (i, COL_MV)),
            _const_spec((SMLP_GROUPS, SMLP_CHUNK, SMLP_CHUNK)),
            _const_spec((SMLP_CHUNK, BRANCH_W)),
            _const_spec((BRANCH_W, BRANCH_W)),
        ],
        out_specs=pl.BlockSpec((tm, BRANCH_W), lambda i: (i, 0)),
        out_shape=jax.ShapeDtypeStruct((n, BRANCH_W), BF16),
        compiler_params=_params("arbitrary"),
        name="smlp",
    )(zr, zr, _bf(ws), bias, _group_ones())


def _merge_kernel(a_ref, b_ref, c_ref, d_ref, g0_ref, g1_ref, g2_ref, g3_ref, x_ref, gate_ref, sh_ref, sc_ref,
                  ng_ref, wb_ref, wo_ref, wr_ref, br_ref, x1_ref, h_ref, comb_ref):
    mix = None
    for br, gt, i in ((a_ref, g0_ref, 0), (b_ref, g1_ref, 1), (c_ref, g2_ref, 2), (d_ref, g3_ref, 3)):
        t = _sigmoid(gt[...].astype(F32)) * _dot(br[...], wb_ref[i])
        mix = t if mix is None else mix + t
    x1 = x_ref[...] + gate_ref[0] * _dot(_bf(mix), wo_ref[...])
    x1_ref[...] = x1
    ms = jnp.mean(x1 * x1, axis=-1, keepdims=True)
    h = x1 * lax.rsqrt(ms + EPS) * ng_ref[...]
    h = h * (1.0 + sc_ref[0]) + sh_ref[0]
    h_ref[...] = _bf(h)

    hh, hm, hl = _split3(h)
    wh, wm, wl = wr_ref[0], wr_ref[1], wr_ref[2]
    logits = (_dot(hh, wh) + _dot(hh, wm) + _dot(hm, wh) + _dot(hh, wl) + _dot(hm, wm) + _dot(hl, wh)) + br_ref[...]
    lane_i = lax.broadcasted_iota(jnp.int32, logits.shape, 1)
    lane = lane_i.astype(F32)
    lane_grp = (lane_i // EXPERTS_PER_GROUP).astype(F32)
    neg = -jnp.inf
    far = float(4 * N_EXPERTS)
    is_g = (lane_i >= N_EXPERTS) & (lane_i < N_EXPERTS + N_GROUPS)
    gl = jnp.where(is_g, logits, neg)
    gmax = jnp.max(gl, axis=-1, keepdims=True)
    gsum = jnp.sum(jnp.exp(gl - gmax), axis=-1, keepdims=True)
    g_top_p = 1.0 / gsum
    g_idx = jnp.min(jnp.where(is_g & (gl == gmax), lane, far), axis=-1, keepdims=True) - float(N_EXPERTS)
    in_grp = (lane_i < N_EXPERTS) & (lane_grp == g_idx)
    e_l = jnp.where(in_grp, logits, neg)
    e1 = jnp.max(e_l, axis=-1, keepdims=True)
    i1 = jnp.min(jnp.where(in_grp & (e_l == e1), lane, far), axis=-1, keepdims=True)
    e_l2 = jnp.where(lane == i1, neg, e_l)
    e2 = jnp.max(e_l2, axis=-1, keepdims=True)
    i2 = jnp.min(jnp.where(in_grp & (lane != i1) & (e_l2 == e2), lane, far), axis=-1, keepdims=True)
    t2 = jnp.exp(e2 - e1)
    w1 = g_top_p / (1.0 + t2)
    w2 = w1 * t2
    comb_ref[...] = jnp.where(lane == i1, w1, 0.0) + jnp.where(lane == i2, w2, 0.0)


def _router_tables(w_rg, b_rg, w_re, b_re):
    w = jnp.zeros((D_MODEL, 128), F32)
    w = w.at[:, :N_EXPERTS].set(w_re.astype(F32)).at[:, N_EXPERTS:N_EXPERTS + N_GROUPS].set(w_rg.astype(F32))
    b = jnp.zeros((1, 128), F32)
    b = b.at[0, :N_EXPERTS].set(b_re.astype(F32)).at[0, N_EXPERTS:N_EXPERTS + N_GROUPS].set(b_rg.astype(F32))
    return jnp.stack(_split3(w), axis=0), b


def _merge(branches, zr, x, mods3, mod_row, norm_g, w_branch_bf, w_out_bf, wr3, br, tm):
    n = x.shape[0]
    row = lambda w: pl.BlockSpec((tm, w), lambda i: (i, 0))
    gate = lambda k: pl.BlockSpec((tm, D_MODEL), lambda i: (i, COL_GATES1024 + k))
    mod = lambda k: pl.BlockSpec((1, 1, D_MODEL), lambda i: (mod_row(i, tm), 0, k))
    return pl.pallas_call(
        _merge_kernel,
        grid=(n // tm,),
        in_specs=[
            row(BRANCH_W), row(BRANCH_W), row(BRANCH_W), row(BRANCH_W),
            gate(0), gate(1), gate(2), gate(3),
            row(D_MODEL),
            mod(2), mod(3), mod(4),
            _const_spec((1, D_MODEL)),
            _const_spec((N_BRANCH, BRANCH_W, D_MODEL)),
            _const_spec((D_MODEL, D_MODEL)),
            _const_spec((3, D_MODEL, 128)),
            _const_spec((1, 128)),
        ],
        out_specs=[row(D_MODEL), row(D_MODEL), row(128)],
        out_shape=[
            jax.ShapeDtypeStruct((n, D_MODEL), F32),
            jax.ShapeDtypeStruct((n, D_MODEL), BF16),
            jax.ShapeDtypeStruct((n, 128), F32),
        ],
        compiler_params=_params("arbitrary"),
        name="merge",
    )(*branches, zr, zr, zr, zr, x, mods3, mods3, mods3, norm_g.reshape(1, D_MODEL), w_branch_bf, w_out_bf, wr3, br)


def _moe_kernel(h_ref, comb_ref, x1_ref, gate_ref, fg_ref, wgu_ref, wd_ref, *out_refs_and_acc, final):
    if final:
        x2_ref, y_ref, acc_ref = out_refs_and_acc
    else:
        x2_ref, acc_ref = out_refs_and_acc
    e = pl.program_id(1)

    @pl.when(e == 0)
    def _():
        acc_ref[...] = jnp.zeros_like(acc_ref)

    gu = _dot(h_ref[...], wgu_ref[0])
    comb = comb_ref[...]
    lane = lax.broadcasted_iota(jnp.int32, comb.shape, 1)
    w = jnp.sum(jnp.where(lane == e, comb, 0.0), axis=-1, keepdims=True)
    a = _silu(gu[:, :EXPERT_FF]) * gu[:, EXPERT_FF:] * w
    acc_ref[...] += _dot(_bf(a), wd_ref[0])

    @pl.when(e == N_EXPERTS - 1)
    def _():
        x2 = x1_ref[...] + gate_ref[0] * acc_ref[...]
        x2_ref[...] = x2
        if final:
            ms = jnp.mean(x2 * x2, axis=-1, keepdims=True)
            y_ref[...] = x2 * lax.rsqrt(ms + EPS) * fg_ref[...]


def _moe(h, comb, x1, mods3, mod_row, final_g, wgu_bf, wd_bf, tm, final):
    n = x1.shape[0]
    row = lambda w: pl.BlockSpec((tm, w), lambda i, e: (i, 0))
    out_specs = [row(D_MODEL)]
    out_shape = [jax.ShapeDtypeStruct((n, D_MODEL), F32)]
    if final:
        out_specs.append(row(D_MODEL))
        out_shape.append(jax.ShapeDtypeStruct((n, D_MODEL), F32))
    return pl.pallas_call(
        functools.partial(_moe_kernel, final=final),
        grid=(n // tm, N_EXPERTS),
        in_specs=[
            row(D_MODEL), row(128), row(D_MODEL),
            pl.BlockSpec((1, 1, D_MODEL), lambda i, e: (mod_row(i, tm), 0, 5)),
            pl.BlockSpec((1, D_MODEL), lambda i, e: (0, 0)),
            pl.BlockSpec((1, D_MODEL, 2 * EXPERT_FF), lambda i, e: (e, 0, 0)),
            pl.BlockSpec((1, EXPERT_FF, D_MODEL), lambda i, e: (e, 0, 0)),
        ],
        out_specs=out_specs,
        out_shape=out_shape,
        scratch_shapes=[pltpu.VMEM((tm, D_MODEL), F32)],
        compiler_params=_params("arbitrary", "arbitrary"),
        name="moe_final" if final else "moe",
    )(h, comb, x1, mods3, final_g.reshape(1, D_MODEL), wgu_bf, wd_bf)


def _state_to_blockdiag_t(s):
    b = s.shape[0]
    st = jnp.swapaxes(s.astype(F32), -1, -2)
    eye = jnp.eye(N_HEADS, dtype=F32)
    full = st[:, :, :, :, None, :] * eye[None, None, :, None, :, None]
    return full.reshape(b, 2, BRANCH_W, BRANCH_W)


def _blockdiag_t_to_state(st):
    b = st.shape[0]
    full = st.reshape(b, 2, N_HEADS, HEAD_DIM, N_HEADS, HEAD_DIM)
    diag = jnp.stack([full[:, :, h, :, h, :] for h in range(N_HEADS)], axis=2)
    return jnp.swapaxes(diag, -1, -2)


def _layer(x, lw, mods3, mod_row, batch, seqlen, latent, caches, final, final_g):
    tm = 1024
    proj = _in_proj(x, mods3, mod_row, lw["norm1_g"], lw["w_in"], tm, not latent)
    zf, zr = proj[0], proj[1]
    kv = None if latent else proj[2]
    o_f, o_b, st = _hgrn_scan(zf, lw["lb"], lw["st0"], batch, seqlen, n_sub=4)
    a_out = _hgrn_post(o_f, o_b, zr, lw["onorm_g"], tm)
    c_out = _smlp(zr, lw["smlp_ws"], lw["smlp_b"], n_chunks=4)
    if latent:
        ck_na, cv_na, ck_swa, cv_swa = caches
        b_out = _na_latent(zr, ck_na, cv_na, lw["na_bias"], batch, seqlen)
        d_out = _swa_latent(zr, ck_swa, cv_swa, lw["swa_sink"], batch, seqlen)
    else:
        b_out = _ctx_attn(zr, batch, seqlen, COL_NAQ, COL_NAK, COL_NAV, BRANCH_W, N_HEADS, None)
        d_out = _ctx_attn(zr, batch, seqlen, COL_SQ, COL_SK128, COL_SV128, SWA_KV_HEADS * HEAD_DIM, SWA_KV_HEADS,
                          lw["swa_sink"])
    x1, h2, comb = _merge((a_out, b_out, c_out, d_out), zr, x, mods3, mod_row, lw["norm2_g"], lw["w_branch"],
                          lw["w_out"], lw["wr3"], lw["br"], 512)
    out = _moe(h2, comb, x1, mods3, mod_row, final_g, lw["wgu"], lw["wd"], tm, final)
    return out, kv, st


def kernel(x_prompt, x_sample, c, cache_na_k, cache_na_v, cache_swa_k, cache_swa_v, state_hgrn, c_ctx, w_ada, b_ada, norm1_g, norm2_g, w_in, hgrn_lb, hgrn_onorm_g, na_rpb, smlp_ws, smlp_b, swa_sink, w_branch, w_out, router_g_w, router_g_b, router_e_w, router_e_b, moe_w_gate, moe_w_up, moe_w_down, final_g):
    bc, lc, _ = x_prompt.shape
    bl, ll, _ = x_sample.shape
    n_ctx_tok = bc * lc

    cond = jnp.zeros((MOD_ROWS, D_MODEL), F32).at[0].set(c_ctx.astype(F32)).at[1:1 + bl].set(c.astype(F32))
    mods = _ada_mods(cond, w_ada, b_ada)

    lb_soft = jax.nn.softmax(hgrn_lb.astype(F32), axis=0)
    lb_all = jnp.cumsum(lb_soft, axis=0) - lb_soft[0:1]
    na_bias = _na_bias_tables(na_rpb)

    def layer_weights(l):
        wr3, br = _router_tables(router_g_w[l], router_g_b[l], router_e_w[l], router_e_b[l])
        return dict(
            norm1_g=norm1_g[l], norm2_g=norm2_g[l], w_in=_bf(w_in[l]), lb=lb_all[l], onorm_g=hgrn_onorm_g[l],
            smlp_ws=smlp_ws[l], smlp_b=smlp_b[l], swa_sink=swa_sink[l], w_branch=_bf(w_branch[l]), w_out=_bf(w_out[l]),
            wr3=wr3, br=br, wgu=_bf(jnp.concatenate([moe_w_gate[l], moe_w_up[l]], axis=-1)), wd=_bf(moe_w_down[l]),
            na_bias=na_bias[l],
        )

    lws = [layer_weights(l) for l in range(DEPTH)]

    ctx_row = lambda i, tm: 0
    lat_row = lambda i, tm: 1 + i // (ll // tm)
    xp = x_prompt.reshape(n_ctx_tok, D_MODEL)
    na_ks, na_vs, swa_ks, swa_vs, states = [], [], [], [], []
    y_prompt = None
    for l in range(DEPTH):
        lw = dict(lws[l], st0=jnp.zeros((bc, 2, BRANCH_W, BRANCH_W), F32))
        final = l == DEPTH - 1
        out, kv, st = _layer(xp, lw, mods[l].reshape(MOD_ROWS, 1, -1), ctx_row, bc, lc, False, None, final, final_g)
        if final:
            xp, y_prompt = out
        else:
            xp = out[0]
        na_ks.append(kv[:, 0:256].reshape(bc, lc, N_HEADS, HEAD_DIM))
        na_vs.append(kv[:, 256:512].reshape(bc, lc, N_HEADS, HEAD_DIM))
        swa_ks.append(kv[:, 512:640].reshape(bc, lc, SWA_KV_HEADS, HEAD_DIM))
        swa_vs.append(kv[:, 640:768].reshape(bc, lc, SWA_KV_HEADS, HEAD_DIM))
        states.append(_blockdiag_t_to_state(st))

    xs = x_sample.reshape(bl * ll, D_MODEL)
    y_sample = None
    n_past = cache_na_k.shape[2]
    for l in range(DEPTH):
        lw = dict(lws[l], st0=_state_to_blockdiag_t(state_hgrn[:, l]))
        caches = (_bf(cache_na_k[:, l]).reshape(bl, n_past, BRANCH_W), _bf(cache_na_v[:, l]).reshape(bl, n_past, BRANCH_W),
                  _bf(cache_swa_k[:, l]).reshape(bl, n_past, SWA_KV_HEADS * HEAD_DIM),
                  _bf(cache_swa_v[:, l]).reshape(bl, n_past, SWA_KV_HEADS * HEAD_DIM))
        final = l == DEPTH - 1
        out, _, _ = _layer(xs, lw, mods[l].reshape(MOD_ROWS, 1, -1), lat_row, bl, ll, True, caches, final, final_g)
        if final:
            xs, y_sample = out
        else:
            xs = out[0]

    return (y_prompt.reshape(bc, lc, D_MODEL), y_sample.reshape(bl, ll, D_MODEL),
            jnp.stack(na_ks, axis=1), jnp.stack(na_vs, axis=1), jnp.stack(swa_ks, axis=1), jnp.stack(swa_vs, axis=1),
            jnp.stack(states, axis=1))
```

```python
import functools

import numpy as np
import jax
import jax.numpy as jnp
from jax import lax
from jax.experimental import pallas as pl
from jax.experimental.pallas import tpu as pltpu

D_MODEL = 1024
DEPTH = 2
GRID_W = 64
HEAD_DIM = 64
N_BRANCH = 4
BRANCH_W = 256
N_HEADS = 4
HGRN_CHUNK = 64
NA_ROWS = 8
NA_COLS = 16
SMLP_GROUPS = 4
SMLP_CHUNK = 128
SWA_KV_HEADS = 2
SWA_WINDOW = 128
SWA_BLOCK = 128
ROPE_THETA = 10000.0
N_GROUPS = 4
EXPERTS_PER_GROUP = 4
N_EXPERTS = 16
EXPERT_FF = 256
ADA_CHUNKS = 6
EPS = 1e-6
TINY = 1e-30
P_IN = 7168
ATT_SCALE = HEAD_DIM ** -0.5

ZF_W = 1024
ZR_W = P_IN - ZF_W
COL_HQ, COL_HI, COL_HFF, COL_HFB = 0, 1, 2, 3
COL_HOG, COL_NAQ, COL_NAK, COL_NAV = 0, 1, 2, 3
COL_MU, COL_MV, COL_SQ = 4, 5, 6
COL_SK128, COL_SV128 = 14, 15
COL_GATES1024 = 2
KV_W = 768
IN_COL_CHUNK = 1024
IN_TILE = 512
ROW_TILE = 1024

MOD_ROWS = 16
VMEM_LIMIT = 56 * 1024 * 1024

F32 = jnp.float32
BF16 = jnp.bfloat16

MOE_TILE = 512
MOE_ROW_ALIGN = 16
MOE_CHUNK = 96
MOE_GATHER_BLK = 128
MOE_SCATTER_BLK = 256
MOE_SEG_ROWS = -(-(2 * MOE_TILE + N_EXPERTS * (MOE_ROW_ALIGN - 1)) // MOE_SCATTER_BLK) * MOE_SCATTER_BLK
MOE_ROWS = -(-(2 * MOE_TILE + N_EXPERTS * (MOE_ROW_ALIGN - 1) + MOE_CHUNK) // MOE_GATHER_BLK) * MOE_GATHER_BLK
HGRN_LEVELS = (0, 1, 2, 4, 8, 16, 32)
SUBLANES = 8
HGRN_MXU_REF_LEVELS = ()


def _bf(x):
    return x.astype(BF16)


def _dot(a, b):
    return jnp.dot(a, b, preferred_element_type=F32)


def _dot_nt(a, b):
    return lax.dot_general(a, b, (((1,), (1,)), ((), ())), preferred_element_type=F32)


def _dot_tn(a, b):
    return lax.dot_general(a, b, (((0,), (0,)), ((), ())), preferred_element_type=F32)


def _split3(x):
    hi = _bf(x)
    r1 = x - hi.astype(F32)
    mid = _bf(r1)
    lo = _bf(r1 - mid.astype(F32))
    return hi, mid, lo


def _dot01_left(m01, x):
    hi, mid, lo = _split3(x)
    return _dot(m01, hi) + _dot(m01, mid) + _dot(m01, lo)


def _dot01_right(x, m01):
    hi, mid, lo = _split3(x)
    return _dot(hi, m01) + _dot(mid, m01) + _dot(lo, m01)


def _sigmoid(x):
    return 0.5 * jnp.tanh(0.5 * x) + 0.5


def _silu(x):
    return x * _sigmoid(x)


def _params(*sem):
    return pltpu.CompilerParams(dimension_semantics=sem, vmem_limit_bytes=VMEM_LIMIT)


def _const_spec(shape):
    n = len(shape)
    return pl.BlockSpec(shape, lambda *_: (0,) * n)


def _head_mask():
    row = lax.broadcasted_iota(jnp.int32, (N_HEADS * HEAD_DIM, BRANCH_W), 0)
    lane = lax.broadcasted_iota(jnp.int32, (N_HEADS * HEAD_DIM, BRANCH_W), 1)
    return (row // HEAD_DIM) == (lane // HEAD_DIM)


def _expand_heads(x, head_mask):
    return jnp.where(head_mask, jnp.concatenate([x] * N_HEADS, axis=0), jnp.zeros((), x.dtype))


def _collapse_heads(r, head_mask):
    r = jnp.where(head_mask, r, 0.0)
    n = HEAD_DIM
    return (r[0:n] + r[n:2 * n]) + (r[2 * n:3 * n] + r[3 * n:4 * n])


def _ada_kernel(cond_ref, w_ref, b_ref, o_ref):
    s = _silu(cond_ref[...])
    o_ref[0] = _dot(_bf(s), _bf(w_ref[0])) + b_ref[0]


def _ada_mods(cond, w_ada, b_ada):
    tn = 1536
    n = ADA_CHUNKS * D_MODEL
    return pl.pallas_call(
        _ada_kernel,
        grid=(DEPTH, n // tn),
        in_specs=[
            pl.BlockSpec((MOD_ROWS, D_MODEL), lambda l, j: (0, 0)),
            pl.BlockSpec((1, D_MODEL, tn), lambda l, j: (l, 0, j)),
            pl.BlockSpec((1, 1, tn), lambda l, j: (l, 0, j)),
        ],
        out_specs=pl.BlockSpec((1, MOD_ROWS, tn), lambda l, j: (l, 0, j)),
        out_shape=jax.ShapeDtypeStruct((DEPTH, MOD_ROWS, n), F32),
        compiler_params=_params("arbitrary", "arbitrary"),
        name="ada_mods",
    )(cond, w_ada, b_ada.reshape(DEPTH, 1, n))


def _in_kernel(x_ref, sh_ref, sc_ref, g_ref, w_ref, zf_ref, zr_ref, *rest, want_kv):
    x = x_ref[...]
    ms = jnp.mean(x * x, axis=-1, keepdims=True)
    h = x * lax.rsqrt(ms + EPS) * g_ref[...]
    h = _bf(h * (1.0 + sc_ref[0]) + sh_ref[0])
    tn = IN_COL_CHUNK
    for j in range(P_IN // tn):
        acc = _dot(h, w_ref[:, j * tn:(j + 1) * tn])
        if j == 0:
            zf_ref[...] = acc
        else:
            zr_ref[:, (j - 1) * tn:j * tn] = _bf(acc)
        if want_kv and j == 1:
            rest[0][:, 0:512] = acc[:, 512:1024]
        if want_kv and j == 2:
            rest[0][:, 512:768] = acc[:, 768:1024]


def _in_proj(x, mods3, mod_row, norm_g, w_in_bf, tm, want_kv):
    n = x.shape[0]
    assert ZF_W == IN_COL_CHUNK
    row = lambda w: pl.BlockSpec((tm, w), lambda i: (i, 0))
    out_specs = [row(ZF_W), row(ZR_W)]
    out_shape = [jax.ShapeDtypeStruct((n, ZF_W), F32), jax.ShapeDtypeStruct((n, ZR_W), BF16)]
    if want_kv:
        out_specs.append(row(KV_W))
        out_shape.append(jax.ShapeDtypeStruct((n, KV_W), F32))
    return pl.pallas_call(
        functools.partial(_in_kernel, want_kv=want_kv),
        grid=(n // tm,),
        in_specs=[
            row(D_MODEL),
            pl.BlockSpec((1, 1, D_MODEL), lambda i: (mod_row(i, tm), 0, 0)),
            pl.BlockSpec((1, 1, D_MODEL), lambda i: (mod_row(i, tm), 0, 1)),
            _const_spec((1, D_MODEL)),
            pl.BlockSpec((D_MODEL, P_IN), lambda i: (0, 0), pipeline_mode=pl.Buffered(1)),
        ],
        out_specs=out_specs,
        out_shape=out_shape,
        compiler_params=_params("arbitrary"),
        name="in_proj_kv" if want_kv else "in_proj",
    )(x, mods3, mods3, norm_g.reshape(1, D_MODEL), w_in_bf)


def _hgrn_tables():
    c = HGRN_CHUNK
    t = np.arange(c)
    tsel = np.zeros((2, (1 + len(HGRN_MXU_REF_LEVELS)) * c, c), np.float32)
    pmask = np.zeros((2, len(HGRN_LEVELS), c, N_HEADS * c), np.float32)
    for rev in (0, 1):
        cum = (t[None, :] >= t[:, None]) if rev else (t[None, :] <= t[:, None])
        tsel[rev, :c] = cum
        pmask[rev, 0] = np.tile(np.eye(c, dtype=np.float32), (1, N_HEADS))
        for li, m in enumerate(HGRN_LEVELS[1:], start=1):
            if m in HGRN_MXU_REF_LEVELS:
                slot = 1 + HGRN_MXU_REF_LEVELS.index(m)
                tsel[rev, slot * c:(slot + 1) * c] = cum[(t // (2 * m)) * (2 * m) + (m - 1 if rev else m)]
            same = (t[:, None] // (2 * m)) == (t[None, :] // (2 * m))
            q_half = ((t & m) == 0) if rev else ((t & m) != 0)
            k_half = ~q_half
            pmask[rev, li] = np.tile((same & q_half[:, None] & k_half[None, :]).astype(np.float32), (1, N_HEADS))
    return jnp.asarray(tsel, BF16), jnp.asarray(pmask, F32)


def _ref_rows(b_ref, m, rev):
    c = HGRN_CHUNK
    off = (m - 1) if rev else m
    row = lambda r, n: jnp.broadcast_to(b_ref[pl.ds(r, 1), :], (n, BRANCH_W))
    if 2 * m >= SUBLANES:
        return jnp.concatenate([row(s + off, 2 * m) for s in range(0, c, 2 * m)], axis=0)
    sub = lax.broadcasted_iota(jnp.int32, (SUBLANES, BRANCH_W), 0)
    tiles = []
    for t0 in range(0, c, SUBLANES):
        cur = row(t0 + off, SUBLANES)
        for s in range(2 * m, SUBLANES, 2 * m):
            cur = jnp.where(sub >= s, row(t0 + s + off, SUBLANES), cur)
        tiles.append(cur)
    return jnp.concatenate(tiles, axis=0)


def _hgrn_chunk(q_raw, v, f_raw, lb, st, tsel, pmask_ref, rev, head_mask, b_ref):
    c = HGRN_CHUNK
    qq = _silu(q_raw)
    f = lb + (1.0 - lb) * _sigmoid(f_raw)
    lf = jnp.log(jnp.maximum(f, TINY))
    k = 1.0 - f
    ball = _dot01_left(tsel, lf)
    b = ball[:c]
    b_ref[...] = b
    bl = b[0:1] if rev else b[c - 1:c]
    vb = _bf(v)
    v_x = _expand_heads(vb, head_mask)

    o = _dot_nt(_bf(qq * jnp.exp(b)), _bf(st))

    p = None
    qb, kb = _bf(qq), _bf(k)
    for li, m in enumerate(HGRN_LEVELS):
        if m == 0:
            qe, ke = qb, kb
        else:
            if m in HGRN_MXU_REF_LEVELS:
                slot = 1 + HGRN_MXU_REF_LEVELS.index(m)
                ref = ball[slot * c:(slot + 1) * c]
            else:
                ref = _ref_rows(b_ref, m, rev)
            e = _bf(jnp.exp(-jnp.abs(b - ref)))
            qe, ke = qb * e, kb * e
        s = _dot_nt(qe, _expand_heads(ke, head_mask)) * pmask_ref[li]
        p = s if p is None else p + s
    o = o + _dot(_bf(p), v_x)

    ke_state = _bf(k * jnp.exp(bl - b))
    st_new = st * jnp.exp(bl) + jnp.where(head_mask, _dot_tn(vb, ke_state), 0.0)
    return o, st_new


def _hgrn_kernel(qf_ref, vf_ref, ff_ref, qb_ref, vb_ref, fb_ref, lb_ref, st0_ref, tsel_ref, pmask_ref,
                 of_ref, ob_ref, st_ref, b_scr, *, n_sub):
    c = HGRN_CHUNK

    @pl.when(pl.program_id(1) == 0)
    def _():
        st_ref[...] = st0_ref[...]

    head_mask = _head_mask()
    st_f = st_ref[0, 0]
    st_b = st_ref[0, 1]
    for j in range(n_sub):
        rf = slice(j * c, (j + 1) * c)
        rb = slice((n_sub - 1 - j) * c, (n_sub - j) * c)
        o_f, st_f = _hgrn_chunk(qf_ref[rf, :], vf_ref[rf, :], ff_ref[rf, :], lb_ref[0:1, :], st_f,
                                tsel_ref[0], pmask_ref.at[0], False, head_mask, b_scr.at[2 * j])
        o_b, st_b = _hgrn_chunk(qb_ref[rb, :], vb_ref[rb, :], fb_ref[rb, :], lb_ref[1:2, :], st_b,
                                tsel_ref[1], pmask_ref.at[1], True, head_mask, b_scr.at[2 * j + 1])
        of_ref[rf, :] = o_f
        ob_ref[rb, :] = o_b
    st_ref[0, 0] = st_f
    st_ref[0, 1] = st_b


def _hgrn_scan(zf, lb2, st0, batch, seqlen, n_sub):
    n = batch * seqlen
    tb = n_sub * HGRN_CHUNK
    nblk = seqlen // tb
    tsel, pmask = _hgrn_tables()
    fwd = lambda col: pl.BlockSpec((tb, BRANCH_W), lambda b, c: (b * nblk + c, col))
    bwd = lambda col: pl.BlockSpec((tb, BRANCH_W), lambda b, c: (b * nblk + nblk - 1 - c, col))
    return pl.pallas_call(
        functools.partial(_hgrn_kernel, n_sub=n_sub),
        grid=(batch, nblk),
        in_specs=[
            fwd(COL_HQ), fwd(COL_HI), fwd(COL_HFF),
            bwd(COL_HQ), bwd(COL_HI), bwd(COL_HFB),
            _const_spec((2, BRANCH_W)),
            pl.BlockSpec((1, 2, BRANCH_W, BRANCH_W), lambda b, c: (b, 0, 0, 0)),
            _const_spec(tsel.shape),
            _const_spec(pmask.shape),
        ],
        out_specs=[
            pl.BlockSpec((tb, BRANCH_W), lambda b, c: (b * nblk + c, 0)),
            pl.BlockSpec((tb, BRANCH_W), lambda b, c: (b * nblk + nblk - 1 - c, 0)),
            pl.BlockSpec((1, 2, BRANCH_W, BRANCH_W), lambda b, c: (b, 0, 0, 0)),
        ],
        out_shape=[
            jax.ShapeDtypeStruct((n, BRANCH_W), F32),
            jax.ShapeDtypeStruct((n, BRANCH_W), F32),
            jax.ShapeDtypeStruct((batch, 2, BRANCH_W, BRANCH_W), F32),
        ],
        scratch_shapes=[pltpu.VMEM((2 * n_sub, HGRN_CHUNK, BRANCH_W), F32)],
        compiler_params=_params("arbitrary", "arbitrary"),
        name="hgrn_scan",
    )(zf, zf, zf, zf, zf, zf, lb2, st0, tsel, pmask)


def _group_ones():
    g = np.arange(BRANCH_W) // HEAD_DIM
    return jnp.asarray((g[:, None] == g[None, :]).astype(np.float32), BF16)


def _hgrn_post_kernel(of_ref, ob_ref, og_ref, g_ref, ones_ref, a_ref):
    o = of_ref[...] + ob_ref[...]
    ms = _dot01_right(o * o, ones_ref[...]) * (1.0 / HEAD_DIM)
    a_ref[...] = _bf(o * lax.rsqrt(ms + EPS) * g_ref[...] * _silu(og_ref[...].astype(F32)))


def _hgrn_post(o_f, o_b, zr, onorm_g, tm):
    n = o_f.shape[0]
    return pl.pallas_call(
        _hgrn_post_kernel,
        grid=(n // tm,),
        in_specs=[
            pl.BlockSpec((tm, BRANCH_W), lambda i: (i, 0)),
            pl.BlockSpec((tm, BRANCH_W), lambda i: (i, 0)),
            pl.BlockSpec((tm, BRANCH_W), lambda i: (i, COL_HOG)),
            _const_spec((1, BRANCH_W)),
            _const_spec((BRANCH_W, BRANCH_W)),
        ],
        out_specs=pl.BlockSpec((tm, BRANCH_W), lambda i: (i, 0)),
        out_shape=jax.ShapeDtypeStruct((n, BRANCH_W), BF16),
        compiler_params=_params("arbitrary"),
        name="hgrn_post",
    )(o_f, o_b, zr, onorm_g.reshape(1, BRANCH_W), _group_ones())


def _ctx_attn_kernel(*refs, n_q, n_kv, has_sink):
    if has_sink:
        sink_ref, q_ref, k_ref, v_ref, o_ref = refs
    else:
        q_ref, k_ref, v_ref, o_ref = refs
    group = n_q // n_kv
    scale = jnp.asarray(ATT_SCALE, BF16)
    for hq in range(n_q):
        hk = hq // group
        q = q_ref[:, hq * HEAD_DIM:(hq + 1) * HEAD_DIM] * scale
        k = k_ref[:, hk * HEAD_DIM:(hk + 1) * HEAD_DIM]
        v = v_ref[:, hk * HEAD_DIM:(hk + 1) * HEAD_DIM]
        s = _dot_nt(q, k)
        m = jnp.max(s, axis=-1, keepdims=True)
        if has_sink:
            sink = sink_ref[hq]
            m = jnp.maximum(m, sink)
        p = jnp.exp(s - m)
        l = jnp.sum(p, axis=-1, keepdims=True)
        if has_sink:
            l = l + jnp.exp(sink - m)
        o_ref[:, hq * HEAD_DIM:(hq + 1) * HEAD_DIM] = _bf(_dot(_bf(p), v) / l)


def _ctx_attn(zr, batch, seqlen, q_col, k_col, v_col, kv_width, n_kv, sink):
    n = batch * seqlen
    n_q = N_HEADS
    has_sink = sink is not None
    in_specs = [
        pl.BlockSpec((seqlen, BRANCH_W), lambda b: (b, q_col)),
        pl.BlockSpec((seqlen, kv_width), lambda b: (b, k_col)),
        pl.BlockSpec((seqlen, kv_width), lambda b: (b, v_col)),
    ]
    args = [zr, zr, zr]
    if has_sink:
        in_specs = [pl.BlockSpec(memory_space=pltpu.SMEM)] + in_specs
        args = [sink.astype(F32)] + args
    return pl.pallas_call(
        functools.partial(_ctx_attn_kernel, n_q=n_q, n_kv=n_kv, has_sink=has_sink),
        grid=(batch,),
        in_specs=in_specs,
        out_specs=pl.BlockSpec((seqlen, BRANCH_W), lambda b: (b, 0)),
        out_shape=jax.ShapeDtypeStruct((n, BRANCH_W), BF16),
        compiler_params=_params("arbitrary"),
        name="ctx_attn_sink" if has_sink else "ctx_attn",
    )(*args)


def _na_bias_kernel(rpb_ref, onehot_ref, mask_ref, o_ref):
    o_ref[...] = _dot01_right(rpb_ref[...], onehot_ref[...]) + mask_ref[...]


def _na_bias_tables(na_rpb):
    n_dr, n_dc = 2 * NA_ROWS - 1, 2 * NA_COLS - 1
    col = np.arange(GRID_W)
    col_start = np.clip(col - NA_COLS // 2, 0, GRID_W - NA_COLS)
    col_mask = (col[None, :] >= col_start[:, None]) & (col[None, :] < col_start[:, None] + NA_COLS)
    d_col = np.clip(col[None, :] - col[:, None], -(NA_COLS - 1), NA_COLS - 1) + (NA_COLS - 1)
    onehot = (np.arange(128)[:, None] == d_col.reshape(1, -1)).astype(np.float32)
    mask_add = np.where(col_mask.reshape(1, -1), 0.0, -np.inf).astype(np.float32)
    rows = DEPTH * N_HEADS * n_dr
    rpb2 = jnp.zeros((128, 128), F32).at[:rows, :n_dc].set(na_rpb.astype(F32).reshape(rows, n_dc))
    tab = pl.pallas_call(
        _na_bias_kernel,
        grid=(1,),
        in_specs=[_const_spec((128, 128)), _const_spec((128, GRID_W * GRID_W)), _const_spec((1, GRID_W * GRID_W))],
        out_specs=_const_spec((128, GRID_W * GRID_W)),
        out_shape=jax.ShapeDtypeStruct((128, GRID_W * GRID_W), F32),
        compiler_params=_params("arbitrary"),
        name="na_bias",
    )(rpb2, jnp.asarray(onehot, BF16), jnp.asarray(mask_add))
    tab = tab[:rows].reshape(DEPTH, N_HEADS, n_dr, GRID_W, GRID_W)
    slabs = [jnp.transpose(tab[:, :, first:first + NA_ROWS], (0, 1, 3, 2, 4)).reshape(DEPTH, N_HEADS * GRID_W, NA_ROWS * GRID_W)
             for first in range(NA_ROWS)]
    return jnp.stack(slabs, axis=1)


def _na_lat_kernel(q_ref, k_ref, v_ref, kc_ref, vc_ref, bias_ref, o_ref, *, rows_per_step, n_rows):
    nk = NA_ROWS * GRID_W
    r0 = pl.program_id(1) * rows_per_step
    head_mask = _head_mask()
    scale = jnp.asarray(ATT_SCALE, BF16)

    def body(i, carry):
        r = r0 + i
        row_start = jnp.clip(r - NA_ROWS // 2, 0, n_rows - NA_ROWS)
        first = row_start - r + (NA_ROWS - 1)
        k0 = pl.multiple_of(row_start * GRID_W, GRID_W)
        q0 = pl.multiple_of(i * GRID_W, GRID_W)
        qx = _expand_heads(q_ref[pl.ds(q0, GRID_W), :] * scale, head_mask)
        s_lat = _dot_nt(qx, k_ref[pl.ds(k0, nk), :]) + bias_ref[first]
        s_ctx = _dot_nt(qx, kc_ref[0])
        m = jnp.maximum(jnp.max(s_lat, axis=-1, keepdims=True), jnp.max(s_ctx, axis=-1, keepdims=True))
        p_lat = jnp.exp(s_lat - m)
        p_ctx = jnp.exp(s_ctx - m)
        l = jnp.sum(p_lat, axis=-1, keepdims=True) + jnp.sum(p_ctx, axis=-1, keepdims=True)
        acc = _dot(_bf(p_lat), v_ref[pl.ds(k0, nk), :]) + _dot(_bf(p_ctx), vc_ref[0])
        o_ref[pl.ds(q0, GRID_W), :] = _bf(_collapse_heads(acc / l, head_mask))
        return carry

    lax.fori_loop(0, rows_per_step, body, 0, unroll=2)


def _na_latent(zr, kc, vc, bias_tab, batch, seqlen):
    n = batch * seqlen
    n_rows = seqlen // GRID_W
    assert n_rows >= NA_ROWS
    rows_per_step = 8
    steps = n_rows // rows_per_step
    tq = rows_per_step * GRID_W
    n_ctx = kc.shape[1]
    return pl.pallas_call(
        functools.partial(_na_lat_kernel, rows_per_step=rows_per_step, n_rows=n_rows),
        grid=(batch, steps),
        in_specs=[
            pl.BlockSpec((tq, BRANCH_W), lambda b, j: (b * steps + j, COL_NAQ)),
            pl.BlockSpec((seqlen, BRANCH_W), lambda b, j: (b, COL_NAK)),
            pl.BlockSpec((seqlen, BRANCH_W), lambda b, j: (b, COL_NAV)),
            pl.BlockSpec((1, n_ctx, BRANCH_W), lambda b, j: (b, 0, 0)),
            pl.BlockSpec((1, n_ctx, BRANCH_W), lambda b, j: (b, 0, 0)),
            _const_spec(bias_tab.shape),
        ],
        out_specs=pl.BlockSpec((tq, BRANCH_W), lambda b, j: (b * steps + j, 0)),
        out_shape=jax.ShapeDtypeStruct((n, BRANCH_W), BF16),
        compiler_params=_params("arbitrary", "arbitrary"),
        name="na_latent",
    )(zr, zr, zr, kc, vc, bias_tab)


def _rope_tables(seqlen):
    half = HEAD_DIM // 2
    t = np.arange(seqlen)
    rows = (t // GRID_W).astype(np.float32)
    cols = (t % GRID_W).astype(np.float32)
    inv = (1.0 / (np.float32(ROPE_THETA) ** (np.arange(0, half, 2, dtype=np.float32) / np.float32(half)))).astype(np.float32)
    ang_r = rows[:, None] * inv[None, :]
    ang_c = cols[:, None] * inv[None, :]
    cos = np.concatenate([np.cos(ang_r), np.cos(ang_r), np.cos(ang_c), np.cos(ang_c)], axis=-1)
    sin = np.concatenate([-np.sin(ang_r), np.sin(ang_r), -np.sin(ang_c), np.sin(ang_c)], axis=-1)
    cos = np.tile(cos.astype(np.float32), (1, N_HEADS))
    sin = np.tile(sin.astype(np.float32), (1, N_HEADS))
    return jnp.asarray(cos), jnp.asarray(sin)


def _rope(x, cos, sin_signed):
    w = x.shape[-1]
    lane = lax.broadcasted_iota(jnp.int32, x.shape, 1)
    partner = jnp.where((lane % 32) < 16, pltpu.roll(x, w - 16, 1), pltpu.roll(x, 16, 1))
    return x * cos + partner * sin_signed


def _swa_lat_kernel(sink_ref, q_ref, k_ref, v_ref, kc_ref, vc_ref, cos_ref, sin_ref, o_ref, *, seqlen):
    blk = SWA_BLOCK
    nwin = 3 * blk
    kvw = SWA_KV_HEADS * HEAD_DIM
    group = N_HEADS // SWA_KV_HEADS
    j = pl.program_id(1)
    q0 = pl.multiple_of(j * blk, blk)
    k0 = pl.multiple_of(jnp.clip((j - 1) * blk, 0, seqlen - nwin), blk)
    q = _rope(q_ref[...].astype(F32), cos_ref[pl.ds(q0, blk), :], sin_ref[pl.ds(q0, blk), :]) * ATT_SCALE
    kw = _rope(k_ref[pl.ds(k0, nwin), :].astype(F32), cos_ref[pl.ds(k0, nwin), 0:kvw], sin_ref[pl.ds(k0, nwin), 0:kvw])
    kw = _bf(kw)
    vw = v_ref[pl.ds(k0, nwin), :]
    qpos = q0 + lax.broadcasted_iota(jnp.int32, (group * blk, nwin), 0) % blk
    kpos = k0 + lax.broadcasted_iota(jnp.int32, (group * blk, nwin), 1)
    band = jnp.abs(kpos - qpos) <= SWA_WINDOW
    outs = []
    for hk in range(SWA_KV_HEADS):
        ks = slice(hk * HEAD_DIM, (hk + 1) * HEAD_DIM)
        q2 = jnp.concatenate([q[:, (hk * group + g) * HEAD_DIM:(hk * group + g + 1) * HEAD_DIM] for g in range(group)], axis=0)
        q2 = _bf(q2)
        row_g = lax.broadcasted_iota(jnp.int32, (group * blk, 1), 0) // blk
        sink = jnp.zeros((group * blk, 1), F32)
        for g in range(group):
            sink = jnp.where(row_g == g, sink_ref[hk * group + g], sink)
        s_band = jnp.where(band, _dot_nt(q2, kw[:, ks]), -jnp.inf)
        s_ctx = _dot_nt(q2, kc_ref[0, :, ks])
        m = jnp.maximum(jnp.maximum(jnp.max(s_band, axis=-1, keepdims=True), jnp.max(s_ctx, axis=-1, keepdims=True)), sink)
        p_band = jnp.exp(s_band - m)
        p_ctx = jnp.exp(s_ctx - m)
        l = jnp.sum(p_band, axis=-1, keepdims=True) + jnp.sum(p_ctx, axis=-1, keepdims=True) + jnp.exp(sink - m)
        acc = (_dot(_bf(p_band), vw[:, ks]) + _dot(_bf(p_ctx), vc_ref[0, :, ks])) / l
        for g in range(group):
            outs.append(acc[g * blk:(g + 1) * blk])
    o_ref[...] = _bf(jnp.concatenate(outs, axis=-1))


def _swa_latent(zr, kc, vc, sink, batch, seqlen):
    n = batch * seqlen
    nb = seqlen // SWA_BLOCK
    kvw = SWA_KV_HEADS * HEAD_DIM
    n_ctx = kc.shape[1]
    cos, sin = _rope_tables(seqlen)
    return pl.pallas_call(
        functools.partial(_swa_lat_kernel, seqlen=seqlen),
        grid=(batch, nb),
        in_specs=[
            pl.BlockSpec(memory_space=pltpu.SMEM),
            pl.BlockSpec((SWA_BLOCK, BRANCH_W), lambda b, j: (b * nb + j, COL_SQ)),
            pl.BlockSpec((seqlen, kvw), lambda b, j: (b, COL_SK128)),
            pl.BlockSpec((seqlen, kvw), lambda b, j: (b, COL_SV128)),
            pl.BlockSpec((1, n_ctx, kvw), lambda b, j: (b, 0, 0)),
            pl.BlockSpec((1, n_ctx, kvw), lambda b, j: (b, 0, 0)),
            _const_spec(cos.shape),
            _const_spec(sin.shape),
        ],
        out_specs=pl.BlockSpec((SWA_BLOCK, BRANCH_W), lambda b, j: (b * nb + j, 0)),
        out_shape=jax.ShapeDtypeStruct((n, BRANCH_W), BF16),
        compiler_params=_params("arbitrary", "arbitrary"),
        name="swa_latent",
    )(sink.astype(F32), zr, zr, zr, kc, vc, cos, sin)


def _smlp_kernel(u_ref, v_ref, ws_ref, bias_ref, ones_ref, o_ref, *, n_chunks):
    v = v_ref[...].astype(F32)
    ms = _dot01_right(v * v, ones_ref[...]) * (1.0 / HEAD_DIM)
    vn = _bf(v * lax.rsqrt(ms + EPS))
    lane_g = lax.broadcasted_iota(jnp.int32, (SMLP_CHUNK, BRANCH_W), 1) // HEAD_DIM
    for ci in range(n_chunks):
        rows = slice(ci * SMLP_CHUNK, (ci + 1) * SMLP_CHUNK)
        mixed = bias_ref[...]
        for g in range(SMLP_GROUPS):
            mixed = mixed + jnp.where(lane_g == g, _dot(ws_ref[g], vn[rows]), 0.0)
        o_ref[rows, :] = _bf(u_ref[rows, :].astype(F32) * mixed)


def _smlp(zr, ws, b, n_chunks):
    n = zr.shape[0]
    tm = n_chunks * SMLP_CHUNK
    bias = jnp.repeat(b.astype(F32).T, BRANCH_W // SMLP_GROUPS, axis=1)
    return pl.pallas_call(
        functools.partial(_smlp_kernel, n_chunks=n_chunks),
        grid=(n // tm,),
        in_specs=[
            pl.BlockSpec((tm, BRANCH_W), lambda i: (i, COL_MU)),
            pl.BlockSpec((tm, BRANCH_W), lambda i: (i, COL_MV)),
            _const_spec((SMLP_GROUPS, SMLP_CHUNK, SMLP_CHUNK)),
            _const_spec((SMLP_CHUNK, BRANCH_W)),
            _const_spec((BRANCH_W, BRANCH_W)),
        ],
        out_specs=pl.BlockSpec((tm, BRANCH_W), lambda i: (i, 0)),
        out_shape=jax.ShapeDtypeStruct((n, BRANCH_W), BF16),
        compiler_params=_params("arbitrary"),
        name="smlp",
    )(zr, zr, _bf(ws), bias, _group_ones())


def _merge_kernel(a_ref, b_ref, c_ref, d_ref, g0_ref, g1_ref, g2_ref, g3_ref, x_ref, gate_ref, sh_ref, sc_ref,
                  ng_ref, wb_ref, wo_ref, wr_ref, br_ref, tri_ref, upper_ref, x1_ref, h_ref, route_ref, seg_ref):
    tm = x_ref.shape[0]
    mix = None
    for br, gt, i in ((a_ref, g0_ref, 0), (b_ref, g1_ref, 1), (c_ref, g2_ref, 2), (d_ref, g3_ref, 3)):
        t = _sigmoid(gt[...].astype(F32)) * _dot(br[...], wb_ref[i])
        mix = t if mix is None else mix + t
    x1 = x_ref[...] + gate_ref[0] * _dot(_bf(mix), wo_ref[...])
    x1_ref[...] = x1
    ms = jnp.mean(x1 * x1, axis=-1, keepdims=True)
    h = x1 * lax.rsqrt(ms + EPS) * ng_ref[...]
    h = h * (1.0 + sc_ref[0]) + sh_ref[0]
    h_ref[...] = _bf(h)

    hh = _bf(h)
    hm = _bf(h - hh.astype(F32))
    logits = (_dot(hh, wr_ref[0]) + _dot(hm, wr_ref[0]) + _dot(hh, wr_ref[1])) + br_ref[...]
    lane_i = lax.broadcasted_iota(jnp.int32, logits.shape, 1)
    lane = lane_i.astype(F32)
    lane_grp = (lane_i // EXPERTS_PER_GROUP).astype(F32)
    neg = -jnp.inf
    far = float(4 * N_EXPERTS)
    is_g = (lane_i >= N_EXPERTS) & (lane_i < N_EXPERTS + N_GROUPS)
    gl = jnp.where(is_g, logits, neg)
    gmax = jnp.max(gl, axis=-1, keepdims=True)
    gsum = jnp.sum(jnp.exp(gl - gmax), axis=-1, keepdims=True)
    g_top_p = 1.0 / gsum
    g_idx = jnp.min(jnp.where(is_g & (gl == gmax), lane, far), axis=-1, keepdims=True) - float(N_EXPERTS)
    in_grp = (lane_i < N_EXPERTS) & (lane_grp == g_idx)
    e_l = jnp.where(in_grp, logits, neg)
    e1 = jnp.max(e_l, axis=-1, keepdims=True)
    i1 = jnp.min(jnp.where(in_grp & (e_l == e1), lane, far), axis=-1, keepdims=True)
    e_l2 = jnp.where(lane == i1, neg, e_l)
    e2 = jnp.max(e_l2, axis=-1, keepdims=True)
    i2 = jnp.min(jnp.where(in_grp & (lane != i1) & (e_l2 == e2), lane, far), axis=-1, keepdims=True)
    t2 = jnp.exp(e2 - e1)
    w1 = g_top_p / (1.0 + t2)
    w2 = w1 * t2

    sel = (lane == i1) | (lane == i2)
    sel_f = jnp.where(sel, 1.0, 0.0)
    cum = _dot(tri_ref[...], _bf(sel_f))
    counts = cum[tm - 1:tm, :]
    padded = jnp.floor((counts + (MOE_ROW_ALIGN - 1)) * (1.0 / MOE_ROW_ALIGN)) * MOE_ROW_ALIGN
    seg_start = _dot(_bf(jnp.broadcast_to(padded, (SUBLANES, 128))), upper_ref[...])[0:1, :]
    slot = seg_start + cum - sel_f
    pos1 = jnp.sum(jnp.where(lane == i1, slot, 0.0), axis=-1, keepdims=True)
    pos2 = jnp.sum(jnp.where(lane == i2, slot, 0.0), axis=-1, keepdims=True)
    route_ref[...] = jnp.where(lane_i == 0, pos1, jnp.where(lane_i == 1, pos2, jnp.where(lane_i == 2, w1,
                               jnp.where(lane_i == 3, w2, 0.0))))
    seg = jnp.where(lane_i[0:1] < N_EXPERTS, seg_start, pltpu.roll(jnp.broadcast_to(padded, (SUBLANES, 128)), N_EXPERTS, 1)[0:1])
    seg_ref[0] = jnp.where(lane_i[0:1] < 2 * N_EXPERTS, seg, 0.0).astype(jnp.int32)


def _router_tables(w_rg, b_rg, w_re, b_re):
    w = jnp.zeros((D_MODEL, 128), F32)
    w = w.at[:, :N_EXPERTS].set(w_re.astype(F32)).at[:, N_EXPERTS:N_EXPERTS + N_GROUPS].set(w_rg.astype(F32))
    b = jnp.zeros((1, 128), F32)
    b = b.at[0, :N_EXPERTS].set(b_re.astype(F32)).at[0, N_EXPERTS:N_EXPERTS + N_GROUPS].set(b_rg.astype(F32))
    return jnp.stack(_split3(w)[:2], axis=0), b


def _merge(branches, zr, x, mods3, mod_row, norm_g, w_branch_bf, w_out_bf, wr3, br):
    n = x.shape[0]
    tm = MOE_TILE
    t = np.arange(tm)
    tri = jnp.asarray((t[None, :] <= t[:, None]).astype(np.float32), BF16)
    e = np.arange(128)
    upper = jnp.asarray((e[:, None] < e[None, :]).astype(np.float32), BF16)
    row = lambda w: pl.BlockSpec((tm, w), lambda i: (i, 0))
    gate = lambda k: pl.BlockSpec((tm, D_MODEL), lambda i: (i, COL_GATES1024 + k))
    mod = lambda k: pl.BlockSpec((1, 1, D_MODEL), lambda i: (mod_row(i, tm), 0, k))
    return pl.pallas_call(
        _merge_kernel,
        grid=(n // tm,),
        in_specs=[
            row(BRANCH_W), row(BRANCH_W), row(BRANCH_W), row(BRANCH_W),
            gate(0), gate(1), gate(2), gate(3),
            row(D_MODEL),
            mod(2), mod(3), mod(4),
            _const_spec((1, D_MODEL)),
            _const_spec((N_BRANCH, BRANCH_W, D_MODEL)),
            _const_spec((D_MODEL, D_MODEL)),
            _const_spec((2, D_MODEL, 128)),
            _const_spec((1, 128)),
            _const_spec((tm, tm)),
            _const_spec((128, 128)),
        ],
        out_specs=[row(D_MODEL), row(D_MODEL), row(128), pl.BlockSpec((1, 1, 128), lambda i: (i, 0, 0))],
        out_shape=[
            jax.ShapeDtypeStruct((n, D_MODEL), F32),
            jax.ShapeDtypeStruct((n, D_MODEL), BF16),
            jax.ShapeDtypeStruct((n, 128), F32),
            jax.ShapeDtypeStruct((n // tm, 1, 128), jnp.int32),
        ],
        compiler_params=_params("arbitrary"),
        name="merge",
    )(*branches, zr, zr, zr, zr, x, mods3, mods3, mods3, norm_g.reshape(1, D_MODEL), w_branch_bf, w_out_bf, wr3, br,
      tri, upper)


def _moe_kernel(seg_ref, h_ref, route_ref, x1_ref, gate_ref, fg_ref, wgu_ref, wd_ref, *outs_and_scratch, final):
    if final:
        x2_ref, y_ref, hs_scr, ys_scr = outs_and_scratch
    else:
        x2_ref, hs_scr, ys_scr = outs_and_scratch
    tm = MOE_TILE
    tile = pl.program_id(0)
    route = route_ref[...]
    route_t = route.T
    pos1_row, pos2_row = route_t[0:1, :], route_t[1:2, :]
    h = h_ref[...]
    for rb in range(MOE_ROWS // MOE_GATHER_BLK):
        r = (rb * MOE_GATHER_BLK + lax.broadcasted_iota(jnp.int32, (MOE_GATHER_BLK, tm), 0)).astype(F32)
        p = jnp.where((r == pos1_row) | (r == pos2_row), 1.0, 0.0)
        hs_scr[rb * MOE_GATHER_BLK:(rb + 1) * MOE_GATHER_BLK, :] = _bf(_dot(_bf(p), h))
    ys_scr[...] = jnp.zeros_like(ys_scr)

    row_in_chunk = lax.broadcasted_iota(jnp.int32, (MOE_CHUNK, D_MODEL), 0)

    def expert_chunk(e, r0, end):
        r0 = pl.multiple_of(r0, MOE_ROW_ALIGN)
        gu = _dot(hs_scr[pl.ds(r0, MOE_CHUNK), :], wgu_ref[e])
        a = _silu(gu[:, :EXPERT_FF]) * gu[:, EXPERT_FF:]
        y = _dot(_bf(a), wd_ref[e])
        ys_scr[pl.ds(r0, MOE_CHUNK), :] = jnp.where(row_in_chunk < end - r0, _bf(y), ys_scr[pl.ds(r0, MOE_CHUNK), :])

    starts = [seg_ref[tile * 128 + e] for e in range(N_EXPERTS)]
    ends = [starts[e] + seg_ref[tile * 128 + N_EXPERTS + e] for e in range(N_EXPERTS)]
    for e in range(N_EXPERTS):
        expert_chunk(e, starts[e], ends[e])
    for e in range(N_EXPERTS):
        n_chunks = lax.div(ends[e] - starts[e] + (MOE_CHUNK - 1), MOE_CHUNK)

        def more(c, carry, e=e):
            expert_chunk(e, starts[e] + c * MOE_CHUNK, ends[e])
            return carry

        lax.fori_loop(1, n_chunks, more, 0)

    pos1, pos2, w1, w2 = route[:, 0:1], route[:, 1:2], route[:, 2:3], route[:, 3:4]
    acc = None
    for cb in range(MOE_SEG_ROWS // MOE_SCATTER_BLK):
        r = (cb * MOE_SCATTER_BLK + lax.broadcasted_iota(jnp.int32, (tm, MOE_SCATTER_BLK), 1)).astype(F32)
        q = jnp.where(r == pos1, w1, jnp.where(r == pos2, w2, 0.0))
        part = _dot(_bf(q), ys_scr[cb * MOE_SCATTER_BLK:(cb + 1) * MOE_SCATTER_BLK, :])
        acc = part if acc is None else acc + part
    x2 = x1_ref[...] + gate_ref[0] * acc
    x2_ref[...] = x2
    if final:
        ms = jnp.mean(x2 * x2, axis=-1, keepdims=True)
        y_ref[...] = x2 * lax.rsqrt(ms + EPS) * fg_ref[...]


def _moe(h, route, seg, x1, mods3, mod_row, final_g, wgu_bf, wd_bf, final):
    n = x1.shape[0]
    tm = MOE_TILE
    row = lambda w: pl.BlockSpec((tm, w), lambda i, s: (i, 0))
    resident = lambda shape: pl.BlockSpec(shape, lambda i, s: (0,) * len(shape), pipeline_mode=pl.Buffered(1))
    out_specs = [row(D_MODEL)]
    out_shape = [jax.ShapeDtypeStruct((n, D_MODEL), F32)]
    if final:
        out_specs.append(row(D_MODEL))
        out_shape.append(jax.ShapeDtypeStruct((n, D_MODEL), F32))
    return pl.pallas_call(
        functools.partial(_moe_kernel, final=final),
        grid_spec=pltpu.PrefetchScalarGridSpec(
            num_scalar_prefetch=1,
            grid=(n // tm,),
            in_specs=[
                row(D_MODEL), row(128), row(D_MODEL),
                pl.BlockSpec((1, 1, D_MODEL), lambda i, s: (mod_row(i, tm), 0, 5)),
                pl.BlockSpec((1, D_MODEL), lambda i, s: (0, 0)),
                resident((N_EXPERTS, D_MODEL, 2 * EXPERT_FF)),
                resident((N_EXPERTS, EXPERT_FF, D_MODEL)),
            ],
            out_specs=out_specs,
            scratch_shapes=[pltpu.VMEM((MOE_ROWS, D_MODEL), BF16), pltpu.VMEM((MOE_ROWS, D_MODEL), BF16)],
        ),
        out_shape=out_shape,
        compiler_params=_params("arbitrary"),
        name="moe_final" if final else "moe",
    )(seg.reshape(-1), h, route, x1, mods3, final_g.reshape(1, D_MODEL), wgu_bf, wd_bf)


def _state_to_blockdiag_t(s):
    b = s.shape[0]
    st = jnp.swapaxes(s.astype(F32), -1, -2)
    eye = jnp.eye(N_HEADS, dtype=F32)
    full = st[:, :, :, :, None, :] * eye[None, None, :, None, :, None]
    return full.reshape(b, 2, BRANCH_W, BRANCH_W)


def _blockdiag_t_to_state(st):
    b = st.shape[0]
    full = st.reshape(b, 2, N_HEADS, HEAD_DIM, N_HEADS, HEAD_DIM)
    diag = jnp.stack([full[:, :, h, :, h, :] for h in range(N_HEADS)], axis=2)
    return jnp.swapaxes(diag, -1, -2)


def _layer(x, lw, mods3, mod_row, batch, seqlen, latent, caches, final, final_g):
    proj = _in_proj(x, mods3, mod_row, lw["norm1_g"], lw["w_in"], IN_TILE, not latent)
    zf, zr = proj[0], proj[1]
    kv = None if latent else proj[2]
    o_f, o_b, st = _hgrn_scan(zf, lw["lb"], lw["st0"], batch, seqlen, n_sub=4)
    a_out = _hgrn_post(o_f, o_b, zr, lw["onorm_g"], ROW_TILE)
    c_out = _smlp(zr, lw["smlp_ws"], lw["smlp_b"], n_chunks=4)
    if latent:
        ck_na, cv_na, ck_swa, cv_swa = caches
        b_out = _na_latent(zr, ck_na, cv_na, lw["na_bias"], batch, seqlen)
        d_out = _swa_latent(zr, ck_swa, cv_swa, lw["swa_sink"], batch, seqlen)
    else:
        b_out = _ctx_attn(zr, batch, seqlen, COL_NAQ, COL_NAK, COL_NAV, BRANCH_W, N_HEADS, None)
        d_out = _ctx_attn(zr, batch, seqlen, COL_SQ, COL_SK128, COL_SV128, SWA_KV_HEADS * HEAD_DIM, SWA_KV_HEADS,
                          lw["swa_sink"])
    x1, h2, route, seg = _merge((a_out, b_out, c_out, d_out), zr, x, mods3, mod_row, lw["norm2_g"], lw["w_branch"],
                                lw["w_out"], lw["wr3"], lw["br"])
    out = _moe(h2, route, seg, x1, mods3, mod_row, final_g, lw["wgu"], lw["wd"], final)
    return out, kv, st


def kernel(x_prompt, x_sample, c, cache_na_k, cache_na_v, cache_swa_k, cache_swa_v, state_hgrn, c_ctx, w_ada, b_ada, norm1_g, norm2_g, w_in, hgrn_lb, hgrn_onorm_g, na_rpb, smlp_ws, smlp_b, swa_sink, w_branch, w_out, router_g_w, router_g_b, router_e_w, router_e_b, moe_w_gate, moe_w_up, moe_w_down, final_g):
    bc, lc, _ = x_prompt.shape
    bl, ll, _ = x_sample.shape
    n_ctx_tok = bc * lc

    cond = jnp.zeros((MOD_ROWS, D_MODEL), F32).at[0].set(c_ctx.astype(F32)).at[1:1 + bl].set(c.astype(F32))
    mods = _ada_mods(cond, w_ada, b_ada)

    lb_soft = jax.nn.softmax(hgrn_lb.astype(F32), axis=0)
    lb_all = jnp.cumsum(lb_soft, axis=0) - lb_soft[0:1]
    na_bias = _na_bias_tables(na_rpb)

    def layer_weights(l):
        wr3, br = _router_tables(router_g_w[l], router_g_b[l], router_e_w[l], router_e_b[l])
        return dict(
            norm1_g=norm1_g[l], norm2_g=norm2_g[l], w_in=_bf(w_in[l]), lb=lb_all[l], onorm_g=hgrn_onorm_g[l],
            smlp_ws=smlp_ws[l], smlp_b=smlp_b[l], swa_sink=swa_sink[l], w_branch=_bf(w_branch[l]), w_out=_bf(w_out[l]),
            wr3=wr3, br=br, wgu=_bf(jnp.concatenate([moe_w_gate[l], moe_w_up[l]], axis=-1)), wd=_bf(moe_w_down[l]),
            na_bias=na_bias[l],
        )

    lws = [layer_weights(l) for l in range(DEPTH)]

    ctx_row = lambda i, tm: 0
    lat_row = lambda i, tm: 1 + i // (ll // tm)
    xp = x_prompt.reshape(n_ctx_tok, D_MODEL)
    na_ks, na_vs, swa_ks, swa_vs, states = [], [], [], [], []
    y_prompt = None
    for l in range(DEPTH):
        lw = dict(lws[l], st0=jnp.zeros((bc, 2, BRANCH_W, BRANCH_W), F32))
        final = l == DEPTH - 1
        out, kv, st = _layer(xp, lw, mods[l].reshape(MOD_ROWS, 1, -1), ctx_row, bc, lc, False, None, final, final_g)
        if final:
            xp, y_prompt = out
        else:
            xp = out[0]
        na_ks.append(kv[:, 0:256].reshape(bc, lc, N_HEADS, HEAD_DIM))
        na_vs.append(kv[:, 256:512].reshape(bc, lc, N_HEADS, HEAD_DIM))
        swa_ks.append(kv[:, 512:640].reshape(bc, lc, SWA_KV_HEADS, HEAD_DIM))
        swa_vs.append(kv[:, 640:768].reshape(bc, lc, SWA_KV_HEADS, HEAD_DIM))
        states.append(_blockdiag_t_to_state(st))

    xs = x_sample.reshape(bl * ll, D_MODEL)
    y_sample = None
    n_past = cache_na_k.shape[2]
    for l in range(DEPTH):
        lw = dict(lws[l], st0=_state_to_blockdiag_t(state_hgrn[:, l]))
        caches = (_bf(cache_na_k[:, l]).reshape(bl, n_past, BRANCH_W), _bf(cache_na_v[:, l]).reshape(bl, n_past, BRANCH_W),
                  _bf(cache_swa_k[:, l]).reshape(bl, n_past, SWA_KV_HEADS * HEAD_DIM),
                  _bf(cache_swa_v[:, l]).reshape(bl, n_past, SWA_KV_HEADS * HEAD_DIM))
        final = l == DEPTH - 1
        out, _, _ = _layer(xs, lw, mods[l].reshape(MOD_ROWS, 1, -1), lat_row, bl, ll, True, caches, final, final_g)
        if final:
            xs, y_sample = out
        else:
            xs = out[0]

    return (y_prompt.reshape(bc, lc, D_MODEL), y_sample.reshape(bl, ll, D_MODEL),
            jnp.stack(na_ks, axis=1), jnp.stack(na_vs, axis=1), jnp.stack(swa_ks, axis=1), jnp.stack(swa_vs, axis=1),
            jnp.stack(states, axis=1))
```

```python
import functools

import numpy as np
import jax
import jax.numpy as jnp
from jax import lax
from jax.experimental import pallas as pl
from jax.experimental.pallas import tpu as pltpu

D_MODEL = 1024
DEPTH = 2
GRID_W = 64
HEAD_DIM = 64
N_BRANCH = 4
BRANCH_W = 256
N_HEADS = 4
HGRN_CHUNK = 64
NA_ROWS = 8
NA_COLS = 16
SMLP_GROUPS = 4
SMLP_CHUNK = 128
SWA_KV_HEADS = 2
SWA_HEAD_ORDER = (0, 2, 1, 3)
SWA_WINDOW = 128
SWA_BLOCK = 128
SWA_BLOCKS_PER_STEP = 4
ROPE_THETA = 10000.0
N_GROUPS = 4
EXPERTS_PER_GROUP = 4
N_EXPERTS = 16
EXPERT_FF = 256
ADA_CHUNKS = 6
EPS = 1e-6
TINY = 1e-30
P_IN = 7168
ATT_SCALE = HEAD_DIM ** -0.5
LOG2E = 1.4426950408889634
QK_PRESCALE = ATT_SCALE * LOG2E

ZF_W = 1024
ZR_W = P_IN - ZF_W
COL_HQ, COL_HI, COL_HFF, COL_HFB = 0, 1, 2, 3
COL_HOG, COL_NAQ, COL_NAK, COL_NAV = 0, 1, 2, 3
COL_MU, COL_MV, COL_SQ = 4, 5, 6
COL_SK128, COL_SV128 = 14, 15
COL_GATES1024 = 2
KV_W = 768
IN_COL_CHUNK = 1024
IN_TILE = 512
ROW_TILE = 1024

MOD_ROWS = 16
VMEM_LIMIT = 56 * 1024 * 1024

F32 = jnp.float32
BF16 = jnp.bfloat16

MOE_TILE = 512
MOE_ROW_ALIGN = 16
MOE_CHUNK = 96
MOE_GATHER_BLK = 128
MOE_SCATTER_BLK = 256
MOE_SEG_ROWS = -(-(2 * MOE_TILE + N_EXPERTS * (MOE_ROW_ALIGN - 1)) // MOE_SCATTER_BLK) * MOE_SCATTER_BLK
MOE_ROWS = -(-(2 * MOE_TILE + N_EXPERTS * (MOE_ROW_ALIGN - 1) + MOE_CHUNK) // MOE_GATHER_BLK) * MOE_GATHER_BLK
HGRN_LEVELS = (0, 1, 2, 4, 8, 16, 32)
SUBLANES = 8
HGRN_MXU_REF_LEVELS = ()


def _bf(x):
    return x.astype(BF16)


def _dot(a, b):
    return jnp.dot(a, b, preferred_element_type=F32)


def _dot_nt(a, b):
    return lax.dot_general(a, b, (((1,), (1,)), ((), ())), preferred_element_type=F32)


def _dot_tn(a, b):
    return lax.dot_general(a, b, (((0,), (0,)), ((), ())), preferred_element_type=F32)


def _split3(x):
    hi = _bf(x)
    r1 = x - hi.astype(F32)
    mid = _bf(r1)
    lo = _bf(r1 - mid.astype(F32))
    return hi, mid, lo


def _dot01_left(m01, x):
    hi, mid, lo = _split3(x)
    return _dot(m01, hi) + _dot(m01, mid) + _dot(m01, lo)


def _dot01_right(x, m01):
    hi, mid, lo = _split3(x)
    return _dot(hi, m01) + _dot(mid, m01) + _dot(lo, m01)


def _sigmoid(x):
    return 0.5 * jnp.tanh(0.5 * x) + 0.5


def _silu(x):
    return x * _sigmoid(x)


def _params(*sem):
    return pltpu.CompilerParams(dimension_semantics=sem, vmem_limit_bytes=VMEM_LIMIT)


def _const_spec(shape):
    n = len(shape)
    return pl.BlockSpec(shape, lambda *_: (0,) * n)


def _head_mask():
    row = lax.broadcasted_iota(jnp.int32, (N_HEADS * HEAD_DIM, BRANCH_W), 0)
    lane = lax.broadcasted_iota(jnp.int32, (N_HEADS * HEAD_DIM, BRANCH_W), 1)
    return (row // HEAD_DIM) == (lane // HEAD_DIM)


def _expand_heads(x, head_mask):
    return jnp.where(head_mask, jnp.concatenate([x] * N_HEADS, axis=0), jnp.zeros((), x.dtype))


def _collapse_heads(r, head_mask):
    r = jnp.where(head_mask, r, 0.0)
    n = HEAD_DIM
    return (r[0:n] + r[n:2 * n]) + (r[2 * n:3 * n] + r[3 * n:4 * n])


def _ada_kernel(cond_ref, w_ref, b_ref, o_ref):
    s = _silu(cond_ref[...])
    o_ref[0] = _dot(_bf(s), _bf(w_ref[0])) + b_ref[0]


def _ada_mods(cond, w_ada, b_ada):
    tn = 1536
    n = ADA_CHUNKS * D_MODEL
    return pl.pallas_call(
        _ada_kernel,
        grid=(DEPTH, n // tn),
        in_specs=[
            pl.BlockSpec((MOD_ROWS, D_MODEL), lambda l, j: (0, 0)),
            pl.BlockSpec((1, D_MODEL, tn), lambda l, j: (l, 0, j)),
            pl.BlockSpec((1, 1, tn), lambda l, j: (l, 0, j)),
        ],
        out_specs=pl.BlockSpec((1, MOD_ROWS, tn), lambda l, j: (l, 0, j)),
        out_shape=jax.ShapeDtypeStruct((DEPTH, MOD_ROWS, n), F32),
        compiler_params=_params("arbitrary", "arbitrary"),
        name="ada_mods",
    )(cond, w_ada, b_ada.reshape(DEPTH, 1, n))


def _in_kernel(x_ref, sh_ref, sc_ref, g_ref, w_ref, zf_ref, zr_ref, *rest, want_kv):
    x = x_ref[...]
    ms = jnp.mean(x * x, axis=-1, keepdims=True)
    h = x * lax.rsqrt(ms + EPS) * g_ref[...]
    h = _bf(h * (1.0 + sc_ref[0]) + sh_ref[0])
    tn = IN_COL_CHUNK
    for j in range(P_IN // tn):
        acc = _dot(h, w_ref[:, j * tn:(j + 1) * tn])
        if j == 0:
            zf_ref[...] = acc
        else:
            zr_ref[:, (j - 1) * tn:j * tn] = _bf(acc)
        if want_kv and j == 1:
            rest[0][:, 0:512] = acc[:, 512:1024]
        if want_kv and j == 2:
            rest[0][:, 512:768] = acc[:, 768:1024]


def _in_proj(x, mods3, mod_row, norm_g, w_in_bf, tm, want_kv):
    n = x.shape[0]
    assert ZF_W == IN_COL_CHUNK
    row = lambda w: pl.BlockSpec((tm, w), lambda i: (i, 0))
    out_specs = [row(ZF_W), row(ZR_W)]
    out_shape = [jax.ShapeDtypeStruct((n, ZF_W), F32), jax.ShapeDtypeStruct((n, ZR_W), BF16)]
    if want_kv:
        out_specs.append(row(KV_W))
        out_shape.append(jax.ShapeDtypeStruct((n, KV_W), F32))
    return pl.pallas_call(
        functools.partial(_in_kernel, want_kv=want_kv),
        grid=(n // tm,),
        in_specs=[
            row(D_MODEL),
            pl.BlockSpec((1, 1, D_MODEL), lambda i: (mod_row(i, tm), 0, 0)),
            pl.BlockSpec((1, 1, D_MODEL), lambda i: (mod_row(i, tm), 0, 1)),
            _const_spec((1, D_MODEL)),
            pl.BlockSpec((D_MODEL, P_IN), lambda i: (0, 0), pipeline_mode=pl.Buffered(1)),
        ],
        out_specs=out_specs,
        out_shape=out_shape,
        compiler_params=_params("arbitrary"),
        name="in_proj_kv" if want_kv else "in_proj",
    )(x, mods3, mods3, norm_g.reshape(1, D_MODEL), w_in_bf)


def _hgrn_tables():
    c = HGRN_CHUNK
    t = np.arange(c)
    tsel = np.zeros((2, (1 + len(HGRN_MXU_REF_LEVELS)) * c, c), np.float32)
    pmask = np.zeros((2, len(HGRN_LEVELS), c, N_HEADS * c), np.float32)
    for rev in (0, 1):
        cum = (t[None, :] >= t[:, None]) if rev else (t[None, :] <= t[:, None])
        tsel[rev, :c] = cum
        pmask[rev, 0] = np.tile(np.eye(c, dtype=np.float32), (1, N_HEADS))
        for li, m in enumerate(HGRN_LEVELS[1:], start=1):
            if m in HGRN_MXU_REF_LEVELS:
                slot = 1 + HGRN_MXU_REF_LEVELS.index(m)
                tsel[rev, slot * c:(slot + 1) * c] = cum[(t // (2 * m)) * (2 * m) + (m - 1 if rev else m)]
            same = (t[:, None] // (2 * m)) == (t[None, :] // (2 * m))
            q_half = ((t & m) == 0) if rev else ((t & m) != 0)
            k_half = ~q_half
            pmask[rev, li] = np.tile((same & q_half[:, None] & k_half[None, :]).astype(np.float32), (1, N_HEADS))
    return jnp.asarray(tsel, BF16), jnp.asarray(pmask, F32)


def _ref_rows(b_ref, m, rev):
    c = HGRN_CHUNK
    off = (m - 1) if rev else m
    row = lambda r, n: jnp.broadcast_to(b_ref[pl.ds(r, 1), :], (n, BRANCH_W))
    if 2 * m >= SUBLANES:
        return jnp.concatenate([row(s + off, 2 * m) for s in range(0, c, 2 * m)], axis=0)
    sub = lax.broadcasted_iota(jnp.int32, (SUBLANES, BRANCH_W), 0)
    tiles = []
    for t0 in range(0, c, SUBLANES):
        cur = row(t0 + off, SUBLANES)
        for s in range(2 * m, SUBLANES, 2 * m):
            cur = jnp.where(sub >= s, row(t0 + s + off, SUBLANES), cur)
        tiles.append(cur)
    return jnp.concatenate(tiles, axis=0)


def _hgrn_chunk(q_raw, v, f_raw, lb, st, tsel, pmask_ref, rev, head_mask, b_ref):
    c = HGRN_CHUNK
    qq = _silu(q_raw)
    f = lb + (1.0 - lb) * _sigmoid(f_raw)
    lf = jnp.log2(jnp.maximum(f, TINY))
    k = 1.0 - f
    ball = _dot01_left(tsel, lf)
    b = ball[:c]
    b_ref[...] = b
    bl = b[0:1] if rev else b[c - 1:c]
    vb = _bf(v)
    v_x = _expand_heads(vb, head_mask)

    o = _dot_nt(_bf(qq * jnp.exp2(b)), _bf(st))

    p = None
    qb, kb = _bf(qq), _bf(k)
    for li, m in enumerate(HGRN_LEVELS):
        if m == 0:
            qe, ke = qb, kb
        else:
            if m in HGRN_MXU_REF_LEVELS:
                slot = 1 + HGRN_MXU_REF_LEVELS.index(m)
                ref = ball[slot * c:(slot + 1) * c]
            else:
                ref = _ref_rows(b_ref, m, rev)
            e = _bf(jnp.exp2(-jnp.abs(b - ref)))
            qe, ke = qb * e, kb * e
        s = _dot_nt(qe, _expand_heads(ke, head_mask)) * pmask_ref[li]
        p = s if p is None else p + s
    o = o + _dot(_bf(p), v_x)

    ke_state = _bf(k * jnp.exp2(bl - b))
    st_new = st * jnp.exp2(bl) + jnp.where(head_mask, _dot_tn(vb, ke_state), 0.0)
    return o, st_new


def _hgrn_kernel(qf_ref, vf_ref, ff_ref, qb_ref, vb_ref, fb_ref, lb_ref, st0_ref, tsel_ref, pmask_ref,
                 of_ref, ob_ref, st_ref, b_scr, *, n_sub):
    c = HGRN_CHUNK

    @pl.when(pl.program_id(1) == 0)
    def _():
        st_ref[...] = st0_ref[...]

    head_mask = _head_mask()
    st_f = st_ref[0, 0]
    st_b = st_ref[0, 1]
    for j in range(n_sub):
        rf = slice(j * c, (j + 1) * c)
        rb = slice((n_sub - 1 - j) * c, (n_sub - j) * c)
        o_f, st_f = _hgrn_chunk(qf_ref[rf, :], vf_ref[rf, :], ff_ref[rf, :], lb_ref[0:1, :], st_f,
                                tsel_ref[0], pmask_ref.at[0], False, head_mask, b_scr.at[2 * j])
        o_b, st_b = _hgrn_chunk(qb_ref[rb, :], vb_ref[rb, :], fb_ref[rb, :], lb_ref[1:2, :], st_b,
                                tsel_ref[1], pmask_ref.at[1], True, head_mask, b_scr.at[2 * j + 1])
        of_ref[rf, :] = o_f
        ob_ref[rb, :] = o_b
    st_ref[0, 0] = st_f
    st_ref[0, 1] = st_b


def _hgrn_scan(zf, lb2, st0, batch, seqlen, n_sub):
    n = batch * seqlen
    tb = n_sub * HGRN_CHUNK
    nblk = seqlen // tb
    tsel, pmask = _hgrn_tables()
    fwd = lambda col: pl.BlockSpec((tb, BRANCH_W), lambda b, c: (b * nblk + c, col))
    bwd = lambda col: pl.BlockSpec((tb, BRANCH_W), lambda b, c: (b * nblk + nblk - 1 - c, col))
    return pl.pallas_call(
        functools.partial(_hgrn_kernel, n_sub=n_sub),
        grid=(batch, nblk),
        in_specs=[
            fwd(COL_HQ), fwd(COL_HI), fwd(COL_HFF),
            bwd(COL_HQ), bwd(COL_HI), bwd(COL_HFB),
            _const_spec((2, BRANCH_W)),
            pl.BlockSpec((1, 2, BRANCH_W, BRANCH_W), lambda b, c: (b, 0, 0, 0)),
            _const_spec(tsel.shape),
            _const_spec(pmask.shape),
        ],
        out_specs=[
            pl.BlockSpec((tb, BRANCH_W), lambda b, c: (b * nblk + c, 0)),
            pl.BlockSpec((tb, BRANCH_W), lambda b, c: (b * nblk + nblk - 1 - c, 0)),
            pl.BlockSpec((1, 2, BRANCH_W, BRANCH_W), lambda b, c: (b, 0, 0, 0)),
        ],
        out_shape=[
            jax.ShapeDtypeStruct((n, BRANCH_W), F32),
            jax.ShapeDtypeStruct((n, BRANCH_W), F32),
            jax.ShapeDtypeStruct((batch, 2, BRANCH_W, BRANCH_W), F32),
        ],
        scratch_shapes=[pltpu.VMEM((2 * n_sub, HGRN_CHUNK, BRANCH_W), F32)],
        compiler_params=_params("arbitrary", "arbitrary"),
        name="hgrn_scan",
    )(zf, zf, zf, zf, zf, zf, lb2, st0, tsel, pmask)


def _group_ones():
    g = np.arange(BRANCH_W) // HEAD_DIM
    return jnp.asarray((g[:, None] == g[None, :]).astype(np.float32), BF16)


def _hgrn_post_kernel(of_ref, ob_ref, og_ref, g_ref, ones_ref, a_ref):
    o = of_ref[...] + ob_ref[...]
    ms = _dot01_right(o * o, ones_ref[...]) * (1.0 / HEAD_DIM)
    a_ref[...] = _bf(o * lax.rsqrt(ms + EPS) * g_ref[...] * _silu(og_ref[...].astype(F32)))


def _hgrn_post(o_f, o_b, zr, onorm_g, tm):
    n = o_f.shape[0]
    return pl.pallas_call(
        _hgrn_post_kernel,
        grid=(n // tm,),
        in_specs=[
            pl.BlockSpec((tm, BRANCH_W), lambda i: (i, 0)),
            pl.BlockSpec((tm, BRANCH_W), lambda i: (i, 0)),
            pl.BlockSpec((tm, BRANCH_W), lambda i: (i, COL_HOG)),
            _const_spec((1, BRANCH_W)),
            _const_spec((BRANCH_W, BRANCH_W)),
        ],
        out_specs=pl.BlockSpec((tm, BRANCH_W), lambda i: (i, 0)),
        out_shape=jax.ShapeDtypeStruct((n, BRANCH_W), BF16),
        compiler_params=_params("arbitrary"),
        name="hgrn_post",
    )(o_f, o_b, zr, onorm_g.reshape(1, BRANCH_W), _group_ones())


def _ctx_attn_kernel(*refs, n_q, n_kv, has_sink):
    if has_sink:
        sink_ref, q_ref, k_ref, v_ref, o_ref = refs
    else:
        q_ref, k_ref, v_ref, o_ref = refs
    group = n_q // n_kv
    for slot, hq in enumerate(SWA_HEAD_ORDER if has_sink else range(n_q)):
        hk = hq // group
        q = q_ref[:, slot * HEAD_DIM:(slot + 1) * HEAD_DIM]
        k = k_ref[:, hk * HEAD_DIM:(hk + 1) * HEAD_DIM]
        v = v_ref[:, hk * HEAD_DIM:(hk + 1) * HEAD_DIM]
        s = _dot_nt(q, k)
        m = jnp.max(s, axis=-1, keepdims=True)
        if has_sink:
            sink = sink_ref[hq] * LOG2E
            m = jnp.maximum(m, sink)
        p = jnp.exp2(s - m)
        l = jnp.sum(p, axis=-1, keepdims=True)
        if has_sink:
            l = l + jnp.exp2(sink - m)
        o_ref[:, slot * HEAD_DIM:(slot + 1) * HEAD_DIM] = _bf(_dot(_bf(p), v) / l)


def _ctx_attn(zr, batch, seqlen, q_col, k_col, v_col, kv_width, n_kv, sink):
    n = batch * seqlen
    n_q = N_HEADS
    has_sink = sink is not None
    in_specs = [
        pl.BlockSpec((seqlen, BRANCH_W), lambda b: (b, q_col)),
        pl.BlockSpec((seqlen, kv_width), lambda b: (b, k_col)),
        pl.BlockSpec((seqlen, kv_width), lambda b: (b, v_col)),
    ]
    args = [zr, zr, zr]
    if has_sink:
        in_specs = [pl.BlockSpec(memory_space=pltpu.SMEM)] + in_specs
        args = [sink.astype(F32)] + args
    return pl.pallas_call(
        functools.partial(_ctx_attn_kernel, n_q=n_q, n_kv=n_kv, has_sink=has_sink),
        grid=(batch,),
        in_specs=in_specs,
        out_specs=pl.BlockSpec((seqlen, BRANCH_W), lambda b: (b, 0)),
        out_shape=jax.ShapeDtypeStruct((n, BRANCH_W), BF16),
        compiler_params=_params("arbitrary"),
        name="ctx_attn_sink" if has_sink else "ctx_attn",
    )(*args)


def _na_bias_kernel(rpb_ref, onehot_ref, mask_ref, o_ref):
    o_ref[...] = (_dot01_right(rpb_ref[...], onehot_ref[...]) + mask_ref[...]) * LOG2E


def _na_bias_tables(na_rpb):
    n_dr, n_dc = 2 * NA_ROWS - 1, 2 * NA_COLS - 1
    col = np.arange(GRID_W)
    col_start = np.clip(col - NA_COLS // 2, 0, GRID_W - NA_COLS)
    col_mask = (col[None, :] >= col_start[:, None]) & (col[None, :] < col_start[:, None] + NA_COLS)
    d_col = np.clip(col[None, :] - col[:, None], -(NA_COLS - 1), NA_COLS - 1) + (NA_COLS - 1)
    onehot = (np.arange(128)[:, None] == d_col.reshape(1, -1)).astype(np.float32)
    mask_add = np.where(col_mask.reshape(1, -1), 0.0, -np.inf).astype(np.float32)
    rows = DEPTH * N_HEADS * n_dr
    rpb2 = jnp.zeros((128, 128), F32).at[:rows, :n_dc].set(na_rpb.astype(F32).reshape(rows, n_dc))
    tab = pl.pallas_call(
        _na_bias_kernel,
        grid=(1,),
        in_specs=[_const_spec((128, 128)), _const_spec((128, GRID_W * GRID_W)), _const_spec((1, GRID_W * GRID_W))],
        out_specs=_const_spec((128, GRID_W * GRID_W)),
        out_shape=jax.ShapeDtypeStruct((128, GRID_W * GRID_W), F32),
        compiler_params=_params("arbitrary"),
        name="na_bias",
    )(rpb2, jnp.asarray(onehot, BF16), jnp.asarray(mask_add))
    tab = tab[:rows].reshape(DEPTH, N_HEADS, n_dr, GRID_W, GRID_W)
    slabs = [jnp.transpose(tab[:, :, first:first + NA_ROWS], (0, 1, 3, 2, 4)).reshape(DEPTH, N_HEADS * GRID_W, NA_ROWS * GRID_W)
             for first in range(NA_ROWS)]
    return jnp.stack(slabs, axis=1)


def _na_lat_kernel(q_ref, k_ref, v_ref, kc_ref, vc_ref, bias_ref, o_ref, *, rows_per_step, n_rows):
    nk = NA_ROWS * GRID_W
    r0 = pl.program_id(1) * rows_per_step
    head_mask = _head_mask()

    def body(i, carry):
        r = r0 + i
        row_start = jnp.clip(r - NA_ROWS // 2, 0, n_rows - NA_ROWS)
        first = row_start - r + (NA_ROWS - 1)
        k0 = pl.multiple_of(row_start * GRID_W, GRID_W)
        q0 = pl.multiple_of(i * GRID_W, GRID_W)
        qx = _expand_heads(q_ref[pl.ds(q0, GRID_W), :], head_mask)
        s_lat = _dot_nt(qx, k_ref[pl.ds(k0, nk), :]) + bias_ref[first]
        s_ctx = _dot_nt(qx, kc_ref[0])
        m = jnp.maximum(jnp.max(s_lat, axis=-1, keepdims=True), jnp.max(s_ctx, axis=-1, keepdims=True))
        p_lat = jnp.exp2(s_lat - m)
        p_ctx = jnp.exp2(s_ctx - m)
        l = jnp.sum(p_lat, axis=-1, keepdims=True) + jnp.sum(p_ctx, axis=-1, keepdims=True)
        acc = _dot(_bf(p_lat), v_ref[pl.ds(k0, nk), :]) + _dot(_bf(p_ctx), vc_ref[0])
        o_ref[pl.ds(q0, GRID_W), :] = _bf(_collapse_heads(acc / l, head_mask))
        return carry

    lax.fori_loop(0, rows_per_step, body, 0, unroll=8)


def _na_latent(zr, kc, vc, bias_tab, batch, seqlen):
    n = batch * seqlen
    n_rows = seqlen // GRID_W
    assert n_rows >= NA_ROWS
    rows_per_step = 8
    steps = n_rows // rows_per_step
    tq = rows_per_step * GRID_W
    n_ctx = kc.shape[1]
    return pl.pallas_call(
        functools.partial(_na_lat_kernel, rows_per_step=rows_per_step, n_rows=n_rows),
        grid=(batch, steps),
        in_specs=[
            pl.BlockSpec((tq, BRANCH_W), lambda b, j: (b * steps + j, COL_NAQ)),
            pl.BlockSpec((seqlen, BRANCH_W), lambda b, j: (b, COL_NAK)),
            pl.BlockSpec((seqlen, BRANCH_W), lambda b, j: (b, COL_NAV)),
            pl.BlockSpec((1, n_ctx, BRANCH_W), lambda b, j: (b, 0, 0)),
            pl.BlockSpec((1, n_ctx, BRANCH_W), lambda b, j: (b, 0, 0)),
            _const_spec(bias_tab.shape),
        ],
        out_specs=pl.BlockSpec((tq, BRANCH_W), lambda b, j: (b * steps + j, 0)),
        out_shape=jax.ShapeDtypeStruct((n, BRANCH_W), BF16),
        compiler_params=_params("arbitrary", "arbitrary"),
        name="na_latent",
    )(zr, zr, zr, kc, vc, bias_tab)


def _rope_tables(seqlen):
    half = HEAD_DIM // 2
    t = np.arange(seqlen)
    rows = (t // GRID_W).astype(np.float32)
    cols = (t % GRID_W).astype(np.float32)
    inv = (1.0 / (np.float32(ROPE_THETA) ** (np.arange(0, half, 2, dtype=np.float32) / np.float32(half)))).astype(np.float32)
    ang_r = rows[:, None] * inv[None, :]
    ang_c = cols[:, None] * inv[None, :]
    cos = np.concatenate([np.cos(ang_r), np.cos(ang_r), np.cos(ang_c), np.cos(ang_c)], axis=-1)
    sin = np.concatenate([-np.sin(ang_r), np.sin(ang_r), -np.sin(ang_c), np.sin(ang_c)], axis=-1)
    cos = np.tile(cos.astype(np.float32), (1, N_HEADS))
    sin = np.tile(sin.astype(np.float32), (1, N_HEADS))
    return jnp.asarray(cos), jnp.asarray(sin)


def _rope(x, cos, sin_signed):
    w = x.shape[-1]
    lane = lax.broadcasted_iota(jnp.int32, x.shape, 1)
    partner = jnp.where((lane % 32) < 16, pltpu.roll(x, w - 16, 1), pltpu.roll(x, 16, 1))
    return x * cos + partner * sin_signed


def _swa_lat_kernel(sink_ref, q_ref, k_ref, v_ref, kc_ref, vc_ref, cos_ref, sin_ref, band_ref, o_ref, *, seqlen):
    for sub in range(SWA_BLOCKS_PER_STEP):
        rows = slice(sub * SWA_BLOCK, (sub + 1) * SWA_BLOCK)
        _swa_block(pl.program_id(1) * SWA_BLOCKS_PER_STEP + sub, sink_ref, q_ref.at[rows, :], k_ref, v_ref, kc_ref, vc_ref,
                   cos_ref, sin_ref, band_ref, o_ref.at[rows, :], seqlen)


def _swa_block(j, sink_ref, q_ref, k_ref, v_ref, kc_ref, vc_ref, cos_ref, sin_ref, band_ref, o_ref, seqlen):
    blk = SWA_BLOCK
    nwin = 3 * blk
    kvw = SWA_KV_HEADS * HEAD_DIM
    q0 = pl.multiple_of(j * blk, blk)
    k_blk = jnp.clip(j - 1, 0, seqlen // blk - 3)
    k0 = pl.multiple_of(k_blk * blk, blk)
    q = _rope(q_ref[...].astype(F32), cos_ref[pl.ds(q0, blk), :], sin_ref[pl.ds(q0, blk), :])
    kw = _rope(k_ref[pl.ds(k0, nwin), :].astype(F32), cos_ref[pl.ds(k0, nwin), 0:kvw], sin_ref[pl.ds(k0, nwin), 0:kvw])
    kw = _bf(kw)
    vw = v_ref[pl.ds(k0, nwin), :]
    n_slot = N_HEADS
    qb = _bf(q)
    lane_kv = lax.broadcasted_iota(jnp.int32, (blk, kvw), 1) // HEAD_DIM
    qx = jnp.concatenate(
        [jnp.where(lane_kv == (slot % SWA_KV_HEADS), qb[:, (slot // SWA_KV_HEADS) * kvw:(slot // SWA_KV_HEADS + 1) * kvw],
                   jnp.zeros((), BF16)) for slot in range(n_slot)], axis=0)
    row_slot = lax.broadcasted_iota(jnp.int32, (n_slot * blk, 1), 0) // blk
    sink = jnp.zeros((n_slot * blk, 1), F32)
    for slot in range(n_slot):
        sink = jnp.where(row_slot == slot, sink_ref[SWA_HEAD_ORDER[slot]] * LOG2E, sink)
    band = band_ref[j - k_blk]
    s_band = _dot_nt(qx, kw) + jnp.concatenate([band] * n_slot, axis=0)
    s_ctx = _dot_nt(qx, kc_ref[0])
    m = jnp.maximum(jnp.maximum(jnp.max(s_band, axis=-1, keepdims=True), jnp.max(s_ctx, axis=-1, keepdims=True)), sink)
    p_band = jnp.exp2(s_band - m)
    p_ctx = jnp.exp2(s_ctx - m)
    l = jnp.sum(p_band, axis=-1, keepdims=True) + jnp.sum(p_ctx, axis=-1, keepdims=True) + jnp.exp2(sink - m)
    acc = (_dot(_bf(p_band), vw) + _dot(_bf(p_ctx), vc_ref[0])) / l
    halves = []
    for half in range(n_slot // SWA_KV_HEADS):
        r0 = half * SWA_KV_HEADS * blk
        halves.append(jnp.where(lane_kv == 0, acc[r0:r0 + blk], acc[r0 + blk:r0 + 2 * blk]))
    o_ref[...] = _bf(jnp.concatenate(halves, axis=-1))


def _swa_latent(zr, kc, vc, sink, batch, seqlen):
    n = batch * seqlen
    nb = seqlen // SWA_BLOCK
    kvw = SWA_KV_HEADS * HEAD_DIM
    n_ctx = kc.shape[1]
    cos, sin = _rope_tables(seqlen)
    assert nb >= 3 and nb % SWA_BLOCKS_PER_STEP == 0
    steps = nb // SWA_BLOCKS_PER_STEP
    tq = SWA_BLOCKS_PER_STEP * SWA_BLOCK
    a = np.arange(SWA_BLOCK)[:, None]
    c = np.arange(3 * SWA_BLOCK)[None, :]
    band = jnp.asarray(np.stack([np.where(np.abs(c - a - off * SWA_BLOCK) <= SWA_WINDOW, 0.0, -np.inf) for off in range(3)])
                       .astype(np.float32))
    return pl.pallas_call(
        functools.partial(_swa_lat_kernel, seqlen=seqlen),
        grid=(batch, steps),
        in_specs=[
            pl.BlockSpec(memory_space=pltpu.SMEM),
            pl.BlockSpec((tq, BRANCH_W), lambda b, j: (b * steps + j, COL_SQ)),
            pl.BlockSpec((seqlen, kvw), lambda b, j: (b, COL_SK128)),
            pl.BlockSpec((seqlen, kvw), lambda b, j: (b, COL_SV128)),
            pl.BlockSpec((1, n_ctx, kvw), lambda b, j: (b, 0, 0)),
            pl.BlockSpec((1, n_ctx, kvw), lambda b, j: (b, 0, 0)),
            _const_spec(cos.shape),
            _const_spec(sin.shape),
            _const_spec(band.shape),
        ],
        out_specs=pl.BlockSpec((tq, BRANCH_W), lambda b, j: (b * steps + j, 0)),
        out_shape=jax.ShapeDtypeStruct((n, BRANCH_W), BF16),
        compiler_params=_params("arbitrary", "arbitrary"),
        name="swa_latent",
    )(sink.astype(F32), zr, zr, zr, kc, vc, cos, sin, band)


def _smlp_kernel(u_ref, v_ref, ws_ref, bias_ref, ones_ref, o_ref, *, n_chunks):
    v = v_ref[...].astype(F32)
    ms = _dot01_right(v * v, ones_ref[...]) * (1.0 / HEAD_DIM)
    vn = _bf(v * lax.rsqrt(ms + EPS))
    lane_g = lax.broadcasted_iota(jnp.int32, (SMLP_CHUNK, BRANCH_W), 1) // HEAD_DIM
    for ci in range(n_chunks):
        rows = slice(ci * SMLP_CHUNK, (ci + 1) * SMLP_CHUNK)
        mixed = bias_ref[...]
        for g in range(SMLP_GROUPS):
            mixed = mixed + jnp.where(lane_g == g, _dot(ws_ref[g], vn[rows]), 0.0)
        o_ref[rows, :] = _bf(u_ref[rows, :].astype(F32) * mixed)


def _smlp(zr, ws, b, n_chunks):
    n = zr.shape[0]
    tm = n_chunks * SMLP_CHUNK
    bias = jnp.repeat(b.astype(F32).T, BRANCH_W // SMLP_GROUPS, axis=1)
    return pl.pallas_call(
        functools.partial(_smlp_kernel, n_chunks=n_chunks),
        grid=(n // tm,),
        in_specs=[
            pl.BlockSpec((tm, BRANCH_W), lambda i: (i, COL_MU)),
            pl.BlockSpec((tm, BRANCH_W), lambda i: (i, COL_MV)),
            _const_spec((SMLP_GROUPS, SMLP_CHUNK, SMLP_CHUNK)),
            _const_spec((SMLP_CHUNK, BRANCH_W)),
            _const_spec((BRANCH_W, BRANCH_W)),
        ],
        out_specs=pl.BlockSpec((tm, BRANCH_W), lambda i: (i, 0)),
        out_shape=jax.ShapeDtypeStruct((n, BRANCH_W), BF16),
        compiler_params=_params("arbitrary"),
        name="smlp",
    )(zr, zr, _bf(ws), bias, _group_ones())


def _merge_kernel(a_ref, b_ref, c_ref, d_ref, g0_ref, g1_ref, g2_ref, g3_ref, x_ref, gate_ref, sh_ref, sc_ref,
                  ng_ref, wb_ref, wo_ref, wr_ref, br_ref, tri_ref, upper_ref, x1_ref, h_ref, route_ref, seg_ref):
    tm = x_ref.shape[0]
    mix = None
    for br, gt, i in ((a_ref, g0_ref, 0), (b_ref, g1_ref, 1), (c_ref, g2_ref, 2), (d_ref, g3_ref, 3)):
        t = (1.0 + jnp.tanh(gt[...].astype(F32))) * _dot(br[...], wb_ref[i])
        mix = t if mix is None else mix + t
    x1 = x_ref[...] + gate_ref[0] * _dot(_bf(mix), wo_ref[...])
    x1_ref[...] = x1
    ms = jnp.mean(x1 * x1, axis=-1, keepdims=True)
    h = x1 * lax.rsqrt(ms + EPS) * ng_ref[...]
    h = h * (1.0 + sc_ref[0]) + sh_ref[0]
    h_ref[...] = _bf(h)

    hh = _bf(h)
    hm = _bf(h - hh.astype(F32))
    logits = (_dot(hh, wr_ref[0]) + _dot(hm, wr_ref[0]) + _dot(hh, wr_ref[1])) + br_ref[...]
    lane_i = lax.broadcasted_iota(jnp.int32, logits.shape, 1)
    lane = lane_i.astype(F32)
    lane_grp = (lane_i // EXPERTS_PER_GROUP).astype(F32)
    neg = -jnp.inf
    far = float(4 * N_EXPERTS)
    is_g = (lane_i >= N_EXPERTS) & (lane_i < N_EXPERTS + N_GROUPS)
    gl = jnp.where(is_g, logits, neg)
    gmax = jnp.max(gl, axis=-1, keepdims=True)
    gsum = jnp.sum(jnp.exp(gl - gmax), axis=-1, keepdims=True)
    g_top_p = 1.0 / gsum
    g_idx = jnp.min(jnp.where(is_g & (gl == gmax), lane, far), axis=-1, keepdims=True) - float(N_EXPERTS)
    in_grp = (lane_i < N_EXPERTS) & (lane_grp == g_idx)
    e_l = jnp.where(in_grp, logits, neg)
    e1 = jnp.max(e_l, axis=-1, keepdims=True)
    i1 = jnp.min(jnp.where(in_grp & (e_l == e1), lane, far), axis=-1, keepdims=True)
    e_l2 = jnp.where(lane == i1, neg, e_l)
    e2 = jnp.max(e_l2, axis=-1, keepdims=True)
    i2 = jnp.min(jnp.where(in_grp & (lane != i1) & (e_l2 == e2), lane, far), axis=-1, keepdims=True)
    t2 = jnp.exp(e2 - e1)
    w1 = g_top_p / (1.0 + t2)
    w2 = w1 * t2

    sel = (lane == i1) | (lane == i2)
    sel_f = jnp.where(sel, 1.0, 0.0)
    cum = _dot(tri_ref[...], _bf(sel_f))
    counts = cum[tm - 1:tm, :]
    padded = jnp.floor((counts + (MOE_ROW_ALIGN - 1)) * (1.0 / MOE_ROW_ALIGN)) * MOE_ROW_ALIGN
    seg_start = _dot(_bf(jnp.broadcast_to(padded, (SUBLANES, 128))), upper_ref[...])[0:1, :]
    slot = seg_start + cum - sel_f
    pos1 = jnp.sum(jnp.where(lane == i1, slot, 0.0), axis=-1, keepdims=True)
    pos2 = jnp.sum(jnp.where(lane == i2, slot, 0.0), axis=-1, keepdims=True)
    route_ref[...] = jnp.where(lane_i == 0, pos1, jnp.where(lane_i == 1, pos2, jnp.where(lane_i == 2, w1,
                               jnp.where(lane_i == 3, w2, 0.0))))
    seg = jnp.where(lane_i[0:1] < N_EXPERTS, seg_start, pltpu.roll(jnp.broadcast_to(padded, (SUBLANES, 128)), N_EXPERTS, 1)[0:1])
    seg_ref[0] = jnp.where(lane_i[0:1] < 2 * N_EXPERTS, seg, 0.0).astype(jnp.int32)


def _router_tables(w_rg, b_rg, w_re, b_re):
    w = jnp.zeros((D_MODEL, 128), F32)
    w = w.at[:, :N_EXPERTS].set(w_re.astype(F32)).at[:, N_EXPERTS:N_EXPERTS + N_GROUPS].set(w_rg.astype(F32))
    b = jnp.zeros((1, 128), F32)
    b = b.at[0, :N_EXPERTS].set(b_re.astype(F32)).at[0, N_EXPERTS:N_EXPERTS + N_GROUPS].set(b_rg.astype(F32))
    return jnp.stack(_split3(w)[:2], axis=0), b


def _merge(branches, zr, x, mods3, mod_row, norm_g, w_branch_bf, w_out_bf, wr3, br):
    n = x.shape[0]
    tm = MOE_TILE
    t = np.arange(tm)
    tri = jnp.asarray((t[None, :] <= t[:, None]).astype(np.float32), BF16)
    e = np.arange(128)
    upper = jnp.asarray((e[:, None] < e[None, :]).astype(np.float32), BF16)
    row = lambda w: pl.BlockSpec((tm, w), lambda i: (i, 0))
    gate = lambda k: pl.BlockSpec((tm, D_MODEL), lambda i: (i, COL_GATES1024 + k))
    mod = lambda k: pl.BlockSpec((1, 1, D_MODEL), lambda i: (mod_row(i, tm), 0, k))
    return pl.pallas_call(
        _merge_kernel,
        grid=(n // tm,),
        in_specs=[
            row(BRANCH_W), row(BRANCH_W), row(BRANCH_W), row(BRANCH_W),
            gate(0), gate(1), gate(2), gate(3),
            row(D_MODEL),
            mod(2), mod(3), mod(4),
            _const_spec((1, D_MODEL)),
            _const_spec((N_BRANCH, BRANCH_W, D_MODEL)),
            _const_spec((D_MODEL, D_MODEL)),
            _const_spec((2, D_MODEL, 128)),
            _const_spec((1, 128)),
            _const_spec((tm, tm)),
            _const_spec((128, 128)),
        ],
        out_specs=[row(D_MODEL), row(D_MODEL), row(128), pl.BlockSpec((1, 1, 128), lambda i: (i, 0, 0))],
        out_shape=[
            jax.ShapeDtypeStruct((n, D_MODEL), F32),
            jax.ShapeDtypeStruct((n, D_MODEL), BF16),
            jax.ShapeDtypeStruct((n, 128), F32),
            jax.ShapeDtypeStruct((n // tm, 1, 128), jnp.int32),
        ],
        compiler_params=_params("arbitrary"),
        name="merge",
    )(*branches, zr, zr, zr, zr, x, mods3, mods3, mods3, norm_g.reshape(1, D_MODEL), w_branch_bf, w_out_bf, wr3, br,
      tri, upper)


def _moe_kernel(seg_ref, h_ref, route_ref, x1_ref, gate_ref, fg_ref, wgu_ref, wd_ref, *outs_and_scratch, final):
    if final:
        x2_ref, y_ref, hs_scr, ys_scr = outs_and_scratch
    else:
        x2_ref, hs_scr, ys_scr = outs_and_scratch
    tm = MOE_TILE
    tile = pl.program_id(0)
    route = route_ref[...]
    route_t = route.T
    pos1_row, pos2_row = route_t[0:1, :], route_t[1:2, :]
    h = h_ref[...]
    for rb in range(MOE_ROWS // MOE_GATHER_BLK):
        r = (rb * MOE_GATHER_BLK + lax.broadcasted_iota(jnp.int32, (MOE_GATHER_BLK, tm), 0)).astype(F32)
        p = jnp.where((r == pos1_row) | (r == pos2_row), 1.0, 0.0)
        hs_scr[rb * MOE_GATHER_BLK:(rb + 1) * MOE_GATHER_BLK, :] = _bf(_dot(_bf(p), h))
    ys_scr[...] = jnp.zeros_like(ys_scr)

    row_in_chunk = lax.broadcasted_iota(jnp.int32, (MOE_CHUNK, D_MODEL), 0)

    def expert_chunk(e, r0, end):
        r0 = pl.multiple_of(r0, MOE_ROW_ALIGN)
        gu = _dot(hs_scr[pl.ds(r0, MOE_CHUNK), :], wgu_ref[e])
        a = _silu(gu[:, :EXPERT_FF]) * gu[:, EXPERT_FF:]
        y = _dot(_bf(a), wd_ref[e])
        ys_scr[pl.ds(r0, MOE_CHUNK), :] = jnp.where(row_in_chunk < end - r0, _bf(y), ys_scr[pl.ds(r0, MOE_CHUNK), :])

    starts = [seg_ref[tile * 128 + e] for e in range(N_EXPERTS)]
    ends = [starts[e] + seg_ref[tile * 128 + N_EXPERTS + e] for e in range(N_EXPERTS)]
    for e in range(N_EXPERTS):
        expert_chunk(e, starts[e], ends[e])
    for e in range(N_EXPERTS):
        n_chunks = lax.div(ends[e] - starts[e] + (MOE_CHUNK - 1), MOE_CHUNK)

        def more(c, carry, e=e):
            expert_chunk(e, starts[e] + c * MOE_CHUNK, ends[e])
            return carry

        lax.fori_loop(1, n_chunks, more, 0)

    pos1, pos2, w1, w2 = route[:, 0:1], route[:, 1:2], route[:, 2:3], route[:, 3:4]
    acc = None
    for cb in range(MOE_SEG_ROWS // MOE_SCATTER_BLK):
        r = (cb * MOE_SCATTER_BLK + lax.broadcasted_iota(jnp.int32, (tm, MOE_SCATTER_BLK), 1)).astype(F32)
        q = jnp.where(r == pos1, w1, jnp.where(r == pos2, w2, 0.0))
        part = _dot(_bf(q), ys_scr[cb * MOE_SCATTER_BLK:(cb + 1) * MOE_SCATTER_BLK, :])
        acc = part if acc is None else acc + part
    x2 = x1_ref[...] + gate_ref[0] * acc
    x2_ref[...] = x2
    if final:
        ms = jnp.mean(x2 * x2, axis=-1, keepdims=True)
        y_ref[...] = x2 * lax.rsqrt(ms + EPS) * fg_ref[...]


def _moe(h, route, seg, x1, mods3, mod_row, final_g, wgu_bf, wd_bf, final):
    n = x1.shape[0]
    tm = MOE_TILE
    row = lambda w: pl.BlockSpec((tm, w), lambda i, s: (i, 0))
    resident = lambda shape: pl.BlockSpec(shape, lambda i, s: (0,) * len(shape), pipeline_mode=pl.Buffered(1))
    out_specs = [row(D_MODEL)]
    out_shape = [jax.ShapeDtypeStruct((n, D_MODEL), F32)]
    if final:
        out_specs.append(row(D_MODEL))
        out_shape.append(jax.ShapeDtypeStruct((n, D_MODEL), F32))
    return pl.pallas_call(
        functools.partial(_moe_kernel, final=final),
        grid_spec=pltpu.PrefetchScalarGridSpec(
            num_scalar_prefetch=1,
            grid=(n // tm,),
            in_specs=[
                row(D_MODEL), row(128), row(D_MODEL),
                pl.BlockSpec((1, 1, D_MODEL), lambda i, s: (mod_row(i, tm), 0, 5)),
                pl.BlockSpec((1, D_MODEL), lambda i, s: (0, 0)),
                resident((N_EXPERTS, D_MODEL, 2 * EXPERT_FF)),
                resident((N_EXPERTS, EXPERT_FF, D_MODEL)),
            ],
            out_specs=out_specs,
            scratch_shapes=[pltpu.VMEM((MOE_ROWS, D_MODEL), BF16), pltpu.VMEM((MOE_ROWS, D_MODEL), BF16)],
        ),
        out_shape=out_shape,
        compiler_params=_params("arbitrary"),
        name="moe_final" if final else "moe",
    )(seg.reshape(-1), h, route, x1, mods3, final_g.reshape(1, D_MODEL), wgu_bf, wd_bf)


def _state_to_blockdiag_t(s):
    b = s.shape[0]
    st = jnp.swapaxes(s.astype(F32), -1, -2)
    eye = jnp.eye(N_HEADS, dtype=F32)
    full = st[:, :, :, :, None, :] * eye[None, None, :, None, :, None]
    return full.reshape(b, 2, BRANCH_W, BRANCH_W)


def _blockdiag_t_to_state(st):
    b = st.shape[0]
    full = st.reshape(b, 2, N_HEADS, HEAD_DIM, N_HEADS, HEAD_DIM)
    diag = jnp.stack([full[:, :, h, :, h, :] for h in range(N_HEADS)], axis=2)
    return jnp.swapaxes(diag, -1, -2)


def _layer(x, lw, mods3, mod_row, batch, seqlen, latent, caches, final, final_g):
    proj = _in_proj(x, mods3, mod_row, lw["norm1_g"], lw["w_in"], IN_TILE, not latent)
    zf, zr = proj[0], proj[1]
    kv = None if latent else proj[2]
    o_f, o_b, st = _hgrn_scan(zf, lw["lb"], lw["st0"], batch, seqlen, n_sub=4)
    a_out = _hgrn_post(o_f, o_b, zr, lw["onorm_g"], ROW_TILE)
    c_out = _smlp(zr, lw["smlp_ws"], lw["smlp_b"], n_chunks=4)
    if latent:
        ck_na, cv_na, ck_swa, cv_swa = caches
        b_out = _na_latent(zr, ck_na, cv_na, lw["na_bias"], batch, seqlen)
        d_out = _swa_latent(zr, ck_swa, cv_swa, lw["swa_sink"], batch, seqlen)
    else:
        b_out = _ctx_attn(zr, batch, seqlen, COL_NAQ, COL_NAK, COL_NAV, BRANCH_W, N_HEADS, None)
        d_out = _ctx_attn(zr, batch, seqlen, COL_SQ, COL_SK128, COL_SV128, SWA_KV_HEADS * HEAD_DIM, SWA_KV_HEADS,
                          lw["swa_sink"])
    x1, h2, route, seg = _merge((a_out, b_out, c_out, d_out), zr, x, mods3, mod_row, lw["norm2_g"], lw["w_branch"],
                                lw["w_out"], lw["wr3"], lw["br"])
    out = _moe(h2, route, seg, x1, mods3, mod_row, final_g, lw["wgu"], lw["wd"], final)
    return out, kv, st


def kernel(x_prompt, x_sample, c, cache_na_k, cache_na_v, cache_swa_k, cache_swa_v, state_hgrn, c_ctx, w_ada, b_ada, norm1_g, norm2_g, w_in, hgrn_lb, hgrn_onorm_g, na_rpb, smlp_ws, smlp_b, swa_sink, w_branch, w_out, router_g_w, router_g_b, router_e_w, router_e_b, moe_w_gate, moe_w_up, moe_w_down, final_g):
    bc, lc, _ = x_prompt.shape
    bl, ll, _ = x_sample.shape
    n_ctx_tok = bc * lc

    cond = jnp.zeros((MOD_ROWS, D_MODEL), F32).at[0].set(c_ctx.astype(F32)).at[1:1 + bl].set(c.astype(F32))
    mods = _ada_mods(cond, w_ada, b_ada)

    lb_soft = jax.nn.softmax(hgrn_lb.astype(F32), axis=0)
    lb_all = jnp.cumsum(lb_soft, axis=0) - lb_soft[0:1]
    na_bias = _na_bias_tables(na_rpb)

    gate_col0 = ZF_W + COL_GATES1024 * 1024
    col_scale = np.ones((P_IN,), np.float32)
    col_scale[gate_col0:] = 0.5
    for q_col in (COL_NAQ, COL_SQ):
        col_scale[ZF_W + q_col * BRANCH_W:ZF_W + (q_col + 1) * BRANCH_W] = QK_PRESCALE
    in_col_scale = jnp.asarray(col_scale)[None, :]

    sq0 = ZF_W + COL_SQ * BRANCH_W

    def layer_weights(l):
        wr3, br = _router_tables(router_g_w[l], router_g_b[l], router_e_w[l], router_e_b[l])
        w_in_l = jnp.concatenate([w_in[l][:, :sq0]] + [w_in[l][:, sq0 + h * HEAD_DIM:sq0 + (h + 1) * HEAD_DIM] for h in SWA_HEAD_ORDER]
                                 + [w_in[l][:, sq0 + BRANCH_W:]], axis=1)
        w_br = w_branch[l]
        w_br = w_br.at[N_BRANCH - 1].set(jnp.concatenate([w_br[N_BRANCH - 1, h * HEAD_DIM:(h + 1) * HEAD_DIM] for h in SWA_HEAD_ORDER], axis=0))
        return dict(
            norm1_g=norm1_g[l], norm2_g=norm2_g[l], w_in=_bf(w_in_l * in_col_scale), lb=lb_all[l],
            onorm_g=hgrn_onorm_g[l], smlp_ws=smlp_ws[l], smlp_b=smlp_b[l], swa_sink=swa_sink[l],
            w_branch=_bf(w_br), w_out=_bf(0.5 * w_out[l]),
            wr3=wr3, br=br, wgu=_bf(jnp.concatenate([moe_w_gate[l], moe_w_up[l]], axis=-1)), wd=_bf(moe_w_down[l]),
            na_bias=na_bias[l],
        )

    lws = [layer_weights(l) for l in range(DEPTH)]

    ctx_row = lambda i, tm: 0
    lat_row = lambda i, tm: 1 + i // (ll // tm)
    xp = x_prompt.reshape(n_ctx_tok, D_MODEL)
    na_ks, na_vs, swa_ks, swa_vs, states = [], [], [], [], []
    y_prompt = None
    for l in range(DEPTH):
        lw = dict(lws[l], st0=jnp.zeros((bc, 2, BRANCH_W, BRANCH_W), F32))
        final = l == DEPTH - 1
        out, kv, st = _layer(xp, lw, mods[l].reshape(MOD_ROWS, 1, -1), ctx_row, bc, lc, False, None, final, final_g)
        if final:
            xp, y_prompt = out
        else:
            xp = out[0]
        na_ks.append(kv[:, 0:256].reshape(bc, lc, N_HEADS, HEAD_DIM))
        na_vs.append(kv[:, 256:512].reshape(bc, lc, N_HEADS, HEAD_DIM))
        swa_ks.append(kv[:, 512:640].reshape(bc, lc, SWA_KV_HEADS, HEAD_DIM))
        swa_vs.append(kv[:, 640:768].reshape(bc, lc, SWA_KV_HEADS, HEAD_DIM))
        states.append(_blockdiag_t_to_state(st))

    xs = x_sample.reshape(bl * ll, D_MODEL)
    y_sample = None
    n_past = cache_na_k.shape[2]
    for l in range(DEPTH):
        lw = dict(lws[l], st0=_state_to_blockdiag_t(state_hgrn[:, l]))
        caches = (_bf(cache_na_k[:, l]).reshape(bl, n_past, BRANCH_W), _bf(cache_na_v[:, l]).reshape(bl, n_past, BRANCH_W),
                  _bf(cache_swa_k[:, l]).reshape(bl, n_past, SWA_KV_HEADS * HEAD_DIM),
                  _bf(cache_swa_v[:, l]).reshape(bl, n_past, SWA_KV_HEADS * HEAD_DIM))
        final = l == DEPTH - 1
        out, _, _ = _layer(xs, lw, mods[l].reshape(MOD_ROWS, 1, -1), lat_row, bl, ll, True, caches, final, final_g)
        if final:
            xs, y_sample = out
        else:
            xs = out[0]

    return (y_prompt.reshape(bc, lc, D_MODEL), y_sample.reshape(bl, ll, D_MODEL),
            jnp.stack(na_ks, axis=1), jnp.stack(na_vs, axis=1), jnp.stack(swa_ks, axis=1), jnp.stack(swa_vs, axis=1),
            jnp.stack(states, axis=1))
```

```python
import functools

import numpy as np
import jax
import jax.numpy as jnp
from jax import lax
from jax.experimental import pallas as pl
from jax.experimental.pallas import tpu as pltpu

D_MODEL = 1024
DEPTH = 2
GRID_W = 64
HEAD_DIM = 64
N_BRANCH = 4
BRANCH_W = 256
N_HEADS = 4
HGRN_CHUNK = 64
HGRN_CHUNKS_PER_STEP = 8
NA_ROWS = 8
NA_COLS = 16
SMLP_GROUPS = 4
SMLP_CHUNK = 128
SWA_KV_HEADS = 2
SWA_HEAD_ORDER = (0, 2, 1, 3)
SWA_WINDOW = 128
SWA_BLOCK = 128
SWA_BLOCKS_PER_STEP = 4
ROPE_THETA = 10000.0
N_GROUPS = 4
EXPERTS_PER_GROUP = 4
N_EXPERTS = 16
EXPERT_FF = 256
ADA_CHUNKS = 6
EPS = 1e-6
TINY = 1e-30
P_IN = 7168
ATT_SCALE = HEAD_DIM ** -0.5
LOG2E = 1.4426950408889634
QK_PRESCALE = ATT_SCALE * LOG2E

ZF_W = 1024
ZR_W = P_IN - ZF_W
COL_HQ, COL_HI, COL_HFF, COL_HFB = 0, 1, 2, 3
COL_HOG, COL_NAQ, COL_NAK, COL_NAV = 0, 1, 2, 3
COL_MU, COL_MV, COL_SQ = 4, 5, 6
COL_SK128, COL_SV128 = 14, 15
COL_GATES1024 = 2
IN_COL_CHUNK = 1024
IN_TILE = 512
ROW_TILE = 1024

MOD_ROWS = 16
VMEM_LIMIT = 56 * 1024 * 1024

F32 = jnp.float32
BF16 = jnp.bfloat16

MOE_TILE = 512
MOE_ROW_ALIGN = 16
MOE_CHUNK = 96
MOE_GATHER_BLK = 128
MOE_SCATTER_BLK = 256
MOE_SEG_ROWS = -(-(2 * MOE_TILE + N_EXPERTS * (MOE_ROW_ALIGN - 1)) // MOE_SCATTER_BLK) * MOE_SCATTER_BLK
MOE_ROWS = -(-(2 * MOE_TILE + N_EXPERTS * (MOE_ROW_ALIGN - 1) + MOE_CHUNK) // MOE_GATHER_BLK) * MOE_GATHER_BLK
HGRN_LEVELS = (0, 1, 2, 4, 8, 16, 32)
SUBLANES = 8
HGRN_MXU_REF_LEVELS = ()


def _bf(x):
    return x.astype(BF16)


def _dot(a, b):
    return jnp.dot(a, b, preferred_element_type=F32)


def _dot_nt(a, b):
    return lax.dot_general(a, b, (((1,), (1,)), ((), ())), preferred_element_type=F32)


def _dot_tn(a, b):
    return lax.dot_general(a, b, (((0,), (0,)), ((), ())), preferred_element_type=F32)


def _split3(x):
    hi = _bf(x)
    r1 = x - hi.astype(F32)
    mid = _bf(r1)
    lo = _bf(r1 - mid.astype(F32))
    return hi, mid, lo


def _dot01_left(m01, x):
    hi, mid, lo = _split3(x)
    return _dot(m01, hi) + _dot(m01, mid) + _dot(m01, lo)


def _dot01_right(x, m01):
    hi, mid, lo = _split3(x)
    return _dot(hi, m01) + _dot(mid, m01) + _dot(lo, m01)


def _sigmoid(x):
    return 0.5 * jnp.tanh(0.5 * x) + 0.5


def _silu(x):
    return x * _sigmoid(x)


def _params(*sem):
    return pltpu.CompilerParams(dimension_semantics=sem, vmem_limit_bytes=VMEM_LIMIT)


def _const_spec(shape):
    n = len(shape)
    return pl.BlockSpec(shape, lambda *_: (0,) * n)


def _head_mask():
    row = lax.broadcasted_iota(jnp.int32, (N_HEADS * HEAD_DIM, BRANCH_W), 0)
    lane = lax.broadcasted_iota(jnp.int32, (N_HEADS * HEAD_DIM, BRANCH_W), 1)
    return (row // HEAD_DIM) == (lane // HEAD_DIM)


def _expand_heads(x, head_mask):
    return jnp.where(head_mask, jnp.concatenate([x] * N_HEADS, axis=0), jnp.zeros((), x.dtype))


def _collapse_heads(r, head_mask):
    r = jnp.where(head_mask, r, 0.0)
    n = HEAD_DIM
    return (r[0:n] + r[n:2 * n]) + (r[2 * n:3 * n] + r[3 * n:4 * n])


def _ada_kernel(cond_ref, w_ref, b_ref, o_ref):
    s = _silu(cond_ref[...])
    o_ref[0] = _dot(_bf(s), _bf(w_ref[0])) + b_ref[0]


def _ada_mods(cond, w_ada, b_ada):
    tn = 1536
    n = ADA_CHUNKS * D_MODEL
    return pl.pallas_call(
        _ada_kernel,
        grid=(DEPTH, n // tn),
        in_specs=[
            pl.BlockSpec((MOD_ROWS, D_MODEL), lambda l, j: (0, 0)),
            pl.BlockSpec((1, D_MODEL, tn), lambda l, j: (l, 0, j)),
            pl.BlockSpec((1, 1, tn), lambda l, j: (l, 0, j)),
        ],
        out_specs=pl.BlockSpec((1, MOD_ROWS, tn), lambda l, j: (l, 0, j)),
        out_shape=jax.ShapeDtypeStruct((DEPTH, MOD_ROWS, n), F32),
        compiler_params=_params("arbitrary", "arbitrary"),
        name="ada_mods",
    )(cond, w_ada, b_ada.reshape(DEPTH, 1, n))


def _in_kernel(x_ref, sh_ref, sc_ref, g_ref, w_ref, zf_ref, zr_ref, *rest, want_kv):
    x = x_ref[...]
    ms = jnp.mean(x * x, axis=-1, keepdims=True)
    h = x * lax.rsqrt(ms + EPS) * g_ref[...]
    h = _bf(h * (1.0 + sc_ref[0]) + sh_ref[0])
    tn = IN_COL_CHUNK
    kvw = SWA_KV_HEADS * HEAD_DIM
    for j in range(P_IN // tn):
        acc = _dot(h, w_ref[:, j * tn:(j + 1) * tn])
        if j == 0:
            zf_ref[...] = acc
            continue
        c0 = (j - 1) * tn
        sq0 = COL_SQ * BRANCH_W - c0
        if 0 <= sq0 < tn:
            heads = [acc[:, sq0 + hd * HEAD_DIM:sq0 + (hd + 1) * HEAD_DIM] for hd in SWA_HEAD_ORDER]
            zr_ref[:, c0:c0 + tn] = _bf(jnp.concatenate([acc[:, :sq0]] + heads + [acc[:, sq0 + BRANCH_W:]], axis=1))
        else:
            zr_ref[:, c0:c0 + tn] = _bf(acc)
        if want_kv:
            nak_ref, nav_ref, sk_ref, sv_ref = rest
            for ref, col, width in ((nak_ref, COL_NAK * BRANCH_W, BRANCH_W), (nav_ref, COL_NAV * BRANCH_W, BRANCH_W),
                                    (sk_ref, COL_SK128 * kvw, kvw), (sv_ref, COL_SV128 * kvw, kvw)):
                if c0 <= col < c0 + tn:
                    ref[...] = acc[:, col - c0:col - c0 + width]


def _in_proj(x, mods3, mod_row, norm_g, w_in_bf, layer, tm, want_kv):
    n = x.shape[0]
    assert ZF_W == IN_COL_CHUNK
    kvw = SWA_KV_HEADS * HEAD_DIM
    row = lambda w: pl.BlockSpec((tm, w), lambda i: (i, 0))
    out_specs = [row(ZF_W), row(ZR_W)]
    out_shape = [jax.ShapeDtypeStruct((n, ZF_W), F32), jax.ShapeDtypeStruct((n, ZR_W), BF16)]
    if want_kv:
        for width in (BRANCH_W, BRANCH_W, kvw, kvw):
            out_specs.append(row(width))
            out_shape.append(jax.ShapeDtypeStruct((n, width), F32))
    return pl.pallas_call(
        functools.partial(_in_kernel, want_kv=want_kv),
        grid=(n // tm,),
        in_specs=[
            row(D_MODEL),
            pl.BlockSpec((1, 1, D_MODEL), lambda i: (mod_row(i, tm), 0, 0)),
            pl.BlockSpec((1, 1, D_MODEL), lambda i: (mod_row(i, tm), 0, 1)),
            _const_spec((1, D_MODEL)),
            pl.BlockSpec((None, D_MODEL, P_IN), lambda i: (layer, 0, 0), pipeline_mode=pl.Buffered(1)),
        ],
        out_specs=out_specs,
        out_shape=out_shape,
        compiler_params=_params("arbitrary"),
        name="in_proj_kv" if want_kv else "in_proj",
    )(x, mods3, mods3, norm_g.reshape(1, D_MODEL), w_in_bf)


def _hgrn_tables():
    c = HGRN_CHUNK
    t = np.arange(c)
    tsel = np.zeros((2, (1 + len(HGRN_MXU_REF_LEVELS)) * c, c), np.float32)
    pmask = np.zeros((2, len(HGRN_LEVELS), c, N_HEADS * c), np.float32)
    for rev in (0, 1):
        cum = (t[None, :] >= t[:, None]) if rev else (t[None, :] <= t[:, None])
        tsel[rev, :c] = cum
        pmask[rev, 0] = np.tile(np.eye(c, dtype=np.float32), (1, N_HEADS))
        for li, m in enumerate(HGRN_LEVELS[1:], start=1):
            if m in HGRN_MXU_REF_LEVELS:
                slot = 1 + HGRN_MXU_REF_LEVELS.index(m)
                tsel[rev, slot * c:(slot + 1) * c] = cum[(t // (2 * m)) * (2 * m) + (m - 1 if rev else m)]
            same = (t[:, None] // (2 * m)) == (t[None, :] // (2 * m))
            q_half = ((t & m) == 0) if rev else ((t & m) != 0)
            k_half = ~q_half
            pmask[rev, li] = np.tile((same & q_half[:, None] & k_half[None, :]).astype(np.float32), (1, N_HEADS))
    return jnp.asarray(tsel, BF16), jnp.asarray(pmask, F32)


def _ref_rows(b_ref, m, rev):
    c = HGRN_CHUNK
    off = (m - 1) if rev else m
    row = lambda r, n: jnp.broadcast_to(b_ref[pl.ds(r, 1), :], (n, BRANCH_W))
    if 2 * m >= SUBLANES:
        return jnp.concatenate([row(s + off, 2 * m) for s in range(0, c, 2 * m)], axis=0)
    sub = lax.broadcasted_iota(jnp.int32, (SUBLANES, BRANCH_W), 0)
    tiles = []
    for t0 in range(0, c, SUBLANES):
        cur = row(t0 + off, SUBLANES)
        for s in range(2 * m, SUBLANES, 2 * m):
            cur = jnp.where(sub >= s, row(t0 + s + off, SUBLANES), cur)
        tiles.append(cur)
    return jnp.concatenate(tiles, axis=0)


def _hgrn_chunk(q_raw, v, f_raw, lb, st, tsel, pmask_ref, rev, head_mask, b_ref):
    c = HGRN_CHUNK
    qq = _silu(q_raw)
    f = lb + (1.0 - lb) * _sigmoid(f_raw)
    lf = jnp.log2(jnp.maximum(f, TINY))
    k = 1.0 - f
    ball = _dot01_left(tsel, lf)
    b = ball[:c]
    b_ref[...] = b
    bl = b[0:1] if rev else b[c - 1:c]
    vb = _bf(v)
    v_x = _expand_heads(vb, head_mask)

    o = _dot_nt(_bf(qq * jnp.exp2(b)), _bf(st))

    p = None
    qb, kb = _bf(qq), _bf(k)
    for li, m in enumerate(HGRN_LEVELS):
        if m == 0:
            qe, ke = qb, kb
        else:
            if m in HGRN_MXU_REF_LEVELS:
                slot = 1 + HGRN_MXU_REF_LEVELS.index(m)
                ref = ball[slot * c:(slot + 1) * c]
            else:
                ref = _ref_rows(b_ref, m, rev)
            e = _bf(jnp.exp2(-jnp.abs(b - ref)))
            qe, ke = qb * e, kb * e
        s = _dot_nt(qe, _expand_heads(ke, head_mask)) * pmask_ref[li]
        p = s if p is None else p + s
    o = o + _dot(_bf(p), v_x)

    ke_state = _bf(k * jnp.exp2(bl - b))
    st_new = st * jnp.exp2(bl) + jnp.where(head_mask, _dot_tn(vb, ke_state), 0.0)
    return o, st_new


def _hgrn_kernel(*refs, n_sub, has_init):
    if has_init:
        qf_ref, vf_ref, ff_ref, qb_ref, vb_ref, fb_ref, lb_ref, st0_ref, tsel_ref, pmask_ref = refs[:10]
        of_ref, ob_ref, fin_ref, st_ref, b_scr = refs[10:]
    else:
        qf_ref, vf_ref, ff_ref, qb_ref, vb_ref, fb_ref, lb_ref, tsel_ref, pmask_ref = refs[:9]
        of_ref, ob_ref, fin_ref, st_ref, b_scr = refs[9:]
    c = HGRN_CHUNK

    @pl.when(pl.program_id(1) == 0)
    def _():
        st_ref[...] = st0_ref[0] if has_init else jnp.zeros_like(st_ref)

    head_mask = _head_mask()
    st_f = st_ref[0]
    st_b = st_ref[1]
    for j in range(n_sub):
        rf = slice(j * c, (j + 1) * c)
        rb = slice((n_sub - 1 - j) * c, (n_sub - j) * c)
        o_f, st_f = _hgrn_chunk(qf_ref[rf, :], vf_ref[rf, :], ff_ref[rf, :], lb_ref[0:1, :], st_f,
                                tsel_ref[0], pmask_ref.at[0], False, head_mask, b_scr.at[2 * j])
        o_b, st_b = _hgrn_chunk(qb_ref[rb, :], vb_ref[rb, :], fb_ref[rb, :], lb_ref[1:2, :], st_b,
                                tsel_ref[1], pmask_ref.at[1], True, head_mask, b_scr.at[2 * j + 1])
        of_ref[rf, :] = o_f
        ob_ref[rb, :] = o_b
    st_ref[0] = st_f
    st_ref[1] = st_b

    @pl.when(pl.program_id(1) == pl.num_programs(1) - 1)
    def _():
        fin_ref[0, 0] = _collapse_heads(st_f, head_mask)
        fin_ref[0, 1] = _collapse_heads(st_b, head_mask)


def _hgrn_scan(zf, lb2, st0, batch, seqlen, n_sub):
    n = batch * seqlen
    tb = n_sub * HGRN_CHUNK
    nblk = seqlen // tb
    tsel, pmask = _hgrn_tables()
    has_init = st0 is not None
    fwd = lambda col: pl.BlockSpec((tb, BRANCH_W), lambda b, c: (b * nblk + c, col))
    bwd = lambda col: pl.BlockSpec((tb, BRANCH_W), lambda b, c: (b * nblk + nblk - 1 - c, col))
    in_specs = [fwd(COL_HQ), fwd(COL_HI), fwd(COL_HFF), bwd(COL_HQ), bwd(COL_HI), bwd(COL_HFB), _const_spec((2, BRANCH_W))]
    args = [zf, zf, zf, zf, zf, zf, lb2]
    if has_init:
        in_specs.append(pl.BlockSpec((1, 2, BRANCH_W, BRANCH_W), lambda b, c: (b, 0, 0, 0)))
        args.append(st0)
    in_specs += [_const_spec(tsel.shape), _const_spec(pmask.shape)]
    args += [tsel, pmask]
    return pl.pallas_call(
        functools.partial(_hgrn_kernel, n_sub=n_sub, has_init=has_init),
        grid=(batch, nblk),
        in_specs=in_specs,
        out_specs=[
            pl.BlockSpec((tb, BRANCH_W), lambda b, c: (b * nblk + c, 0)),
            pl.BlockSpec((tb, BRANCH_W), lambda b, c: (b * nblk + nblk - 1 - c, 0)),
            pl.BlockSpec((1, 2, HEAD_DIM, BRANCH_W), lambda b, c: (b, 0, 0, 0)),
        ],
        out_shape=[
            jax.ShapeDtypeStruct((n, BRANCH_W), F32),
            jax.ShapeDtypeStruct((n, BRANCH_W), F32),
            jax.ShapeDtypeStruct((batch, 2, HEAD_DIM, BRANCH_W), F32),
        ],
        scratch_shapes=[pltpu.VMEM((2, BRANCH_W, BRANCH_W), F32), pltpu.VMEM((2 * n_sub, HGRN_CHUNK, BRANCH_W), F32)],
        compiler_params=_params("arbitrary", "arbitrary"),
        name="hgrn_scan",
    )(*args)


def _group_ones():
    g = np.arange(BRANCH_W) // HEAD_DIM
    return jnp.asarray((g[:, None] == g[None, :]).astype(np.float32), BF16)


def _hgrn_post_kernel(of_ref, ob_ref, og_ref, g_ref, ones_ref, a_ref):
    o = of_ref[...] + ob_ref[...]
    ms = _dot01_right(o * o, ones_ref[...]) * (1.0 / HEAD_DIM)
    a_ref[...] = _bf(o * lax.rsqrt(ms + EPS) * g_ref[...] * _silu(og_ref[...].astype(F32)))


def _hgrn_post(o_f, o_b, zr, onorm_g, tm):
    n = o_f.shape[0]
    return pl.pallas_call(
        _hgrn_post_kernel,
        grid=(n // tm,),
        in_specs=[
            pl.BlockSpec((tm, BRANCH_W), lambda i: (i, 0)),
            pl.BlockSpec((tm, BRANCH_W), lambda i: (i, 0)),
            pl.BlockSpec((tm, BRANCH_W), lambda i: (i, COL_HOG)),
            _const_spec((1, BRANCH_W)),
            _const_spec((BRANCH_W, BRANCH_W)),
        ],
        out_specs=pl.BlockSpec((tm, BRANCH_W), lambda i: (i, 0)),
        out_shape=jax.ShapeDtypeStruct((n, BRANCH_W), BF16),
        compiler_params=_params("arbitrary"),
        name="hgrn_post",
    )(o_f, o_b, zr, onorm_g.reshape(1, BRANCH_W), _group_ones())


def _ctx_attn_kernel(*refs, n_q, n_kv, has_sink):
    if has_sink:
        sink_ref, q_ref, k_ref, v_ref, o_ref = refs
    else:
        q_ref, k_ref, v_ref, o_ref = refs
    group = n_q // n_kv
    for slot, hq in enumerate(SWA_HEAD_ORDER if has_sink else range(n_q)):
        hk = hq // group
        q = q_ref[:, slot * HEAD_DIM:(slot + 1) * HEAD_DIM]
        k = k_ref[:, hk * HEAD_DIM:(hk + 1) * HEAD_DIM]
        v = v_ref[:, hk * HEAD_DIM:(hk + 1) * HEAD_DIM]
        s = _dot_nt(q, k)
        m = jnp.max(s, axis=-1, keepdims=True)
        if has_sink:
            sink = sink_ref[hq] * LOG2E
            m = jnp.maximum(m, sink)
        p = jnp.exp2(s - m)
        l = jnp.sum(p, axis=-1, keepdims=True)
        if has_sink:
            l = l + jnp.exp2(sink - m)
        o_ref[:, slot * HEAD_DIM:(slot + 1) * HEAD_DIM] = _bf(_dot(_bf(p), v) / l)


def _ctx_attn(zr, batch, seqlen, q_col, k_col, v_col, kv_width, n_kv, sink):
    n = batch * seqlen
    n_q = N_HEADS
    has_sink = sink is not None
    in_specs = [
        pl.BlockSpec((seqlen, BRANCH_W), lambda b: (b, q_col)),
        pl.BlockSpec((seqlen, kv_width), lambda b: (b, k_col)),
        pl.BlockSpec((seqlen, kv_width), lambda b: (b, v_col)),
    ]
    args = [zr, zr, zr]
    if has_sink:
        in_specs = [pl.BlockSpec(memory_space=pltpu.SMEM)] + in_specs
        args = [sink.astype(F32)] + args
    return pl.pallas_call(
        functools.partial(_ctx_attn_kernel, n_q=n_q, n_kv=n_kv, has_sink=has_sink),
        grid=(batch,),
        in_specs=in_specs,
        out_specs=pl.BlockSpec((seqlen, BRANCH_W), lambda b: (b, 0)),
        out_shape=jax.ShapeDtypeStruct((n, BRANCH_W), BF16),
        compiler_params=_params("arbitrary"),
        name="ctx_attn_sink" if has_sink else "ctx_attn",
    )(*args)


def _na_bias_kernel(rpb_ref, onehot_ref, mask_ref, o_ref):
    o_ref[...] = (_dot01_right(rpb_ref[...], onehot_ref[...]) + mask_ref[...]) * LOG2E


def _na_bias_tables(na_rpb):
    n_dr, n_dc = 2 * NA_ROWS - 1, 2 * NA_COLS - 1
    col = np.arange(GRID_W)
    col_start = np.clip(col - NA_COLS // 2, 0, GRID_W - NA_COLS)
    col_mask = (col[None, :] >= col_start[:, None]) & (col[None, :] < col_start[:, None] + NA_COLS)
    d_col = np.clip(col[None, :] - col[:, None], -(NA_COLS - 1), NA_COLS - 1) + (NA_COLS - 1)
    onehot = (np.arange(128)[:, None] == d_col.reshape(1, -1)).astype(np.float32)
    mask_add = np.where(col_mask.reshape(1, -1), 0.0, -np.inf).astype(np.float32)
    rows = DEPTH * N_HEADS * n_dr
    rpb2 = jnp.zeros((128, 128), F32).at[:rows, :n_dc].set(na_rpb.astype(F32).reshape(rows, n_dc))
    tab = pl.pallas_call(
        _na_bias_kernel,
        grid=(1,),
        in_specs=[_const_spec((128, 128)), _const_spec((128, GRID_W * GRID_W)), _const_spec((1, GRID_W * GRID_W))],
        out_specs=_const_spec((128, GRID_W * GRID_W)),
        out_shape=jax.ShapeDtypeStruct((128, GRID_W * GRID_W), F32),
        compiler_params=_params("arbitrary"),
        name="na_bias",
    )(rpb2, jnp.asarray(onehot, BF16), jnp.asarray(mask_add))
    tab = tab[:rows].reshape(DEPTH, N_HEADS, n_dr, GRID_W, GRID_W)
    slabs = [jnp.transpose(tab[:, :, first:first + NA_ROWS], (0, 1, 3, 2, 4)).reshape(DEPTH, N_HEADS * GRID_W, NA_ROWS * GRID_W)
             for first in range(NA_ROWS)]
    return jnp.stack(slabs, axis=1)


def _na_lat_kernel(q_ref, k_ref, v_ref, kc_ref, vc_ref, bias_ref, o_ref, *, rows_per_step, n_rows):
    nk = NA_ROWS * GRID_W
    r0 = pl.program_id(1) * rows_per_step
    head_mask = _head_mask()

    def body(i, carry):
        r = r0 + i
        row_start = jnp.clip(r - NA_ROWS // 2, 0, n_rows - NA_ROWS)
        first = row_start - r + (NA_ROWS - 1)
        k0 = pl.multiple_of(row_start * GRID_W, GRID_W)
        q0 = pl.multiple_of(i * GRID_W, GRID_W)
        qx = _expand_heads(q_ref[pl.ds(q0, GRID_W), :], head_mask)
        s_lat = _dot_nt(qx, k_ref[pl.ds(k0, nk), :]) + bias_ref[first]
        s_ctx = _dot_nt(qx, kc_ref[0])
        m = jnp.maximum(jnp.max(s_lat, axis=-1, keepdims=True), jnp.max(s_ctx, axis=-1, keepdims=True))
        p_lat = jnp.exp2(s_lat - m)
        p_ctx = jnp.exp2(s_ctx - m)
        l = jnp.sum(p_lat, axis=-1, keepdims=True) + jnp.sum(p_ctx, axis=-1, keepdims=True)
        acc = _dot(_bf(p_lat), v_ref[pl.ds(k0, nk), :]) + _dot(_bf(p_ctx), vc_ref[0])
        o_ref[pl.ds(q0, GRID_W), :] = _bf(_collapse_heads(acc / l, head_mask))
        return carry

    lax.fori_loop(0, rows_per_step, body, 0, unroll=8)


def _na_latent(zr, kc, vc, bias_tab, batch, seqlen):
    n = batch * seqlen
    n_rows = seqlen // GRID_W
    assert n_rows >= NA_ROWS
    rows_per_step = 8
    steps = n_rows // rows_per_step
    tq = rows_per_step * GRID_W
    n_ctx = kc.shape[1]
    return pl.pallas_call(
        functools.partial(_na_lat_kernel, rows_per_step=rows_per_step, n_rows=n_rows),
        grid=(batch, steps),
        in_specs=[
            pl.BlockSpec((tq, BRANCH_W), lambda b, j: (b * steps + j, COL_NAQ)),
            pl.BlockSpec((seqlen, BRANCH_W), lambda b, j: (b, COL_NAK)),
            pl.BlockSpec((seqlen, BRANCH_W), lambda b, j: (b, COL_NAV)),
            pl.BlockSpec((1, n_ctx, BRANCH_W), lambda b, j: (b, 0, 0)),
            pl.BlockSpec((1, n_ctx, BRANCH_W), lambda b, j: (b, 0, 0)),
            _const_spec(bias_tab.shape),
        ],
        out_specs=pl.BlockSpec((tq, BRANCH_W), lambda b, j: (b * steps + j, 0)),
        out_shape=jax.ShapeDtypeStruct((n, BRANCH_W), BF16),
        compiler_params=_params("arbitrary", "arbitrary"),
        name="na_latent",
    )(zr, zr, zr, kc, vc, bias_tab)


def _rope_tables(seqlen):
    half = HEAD_DIM // 2
    t = np.arange(seqlen)
    rows = (t // GRID_W).astype(np.float32)
    cols = (t % GRID_W).astype(np.float32)
    inv = (1.0 / (np.float32(ROPE_THETA) ** (np.arange(0, half, 2, dtype=np.float32) / np.float32(half)))).astype(np.float32)
    ang_r = rows[:, None] * inv[None, :]
    ang_c = cols[:, None] * inv[None, :]
    cos = np.concatenate([np.cos(ang_r), np.cos(ang_r), np.cos(ang_c), np.cos(ang_c)], axis=-1)
    sin = np.concatenate([-np.sin(ang_r), np.sin(ang_r), -np.sin(ang_c), np.sin(ang_c)], axis=-1)
    cos = np.tile(cos.astype(np.float32), (1, N_HEADS))
    sin = np.tile(sin.astype(np.float32), (1, N_HEADS))
    return jnp.asarray(cos), jnp.asarray(sin)


def _rope(x, cos, sin_signed):
    w = x.shape[-1]
    lane = lax.broadcasted_iota(jnp.int32, x.shape, 1)
    partner = jnp.where((lane % 32) < 16, pltpu.roll(x, w - 16, 1), pltpu.roll(x, 16, 1))
    return x * cos + partner * sin_signed


def _swa_lat_kernel(sink_ref, q_ref, k_ref, v_ref, kc_ref, vc_ref, cos_ref, sin_ref, band_ref, o_ref, *, seqlen):
    for sub in range(SWA_BLOCKS_PER_STEP):
        rows = slice(sub * SWA_BLOCK, (sub + 1) * SWA_BLOCK)
        _swa_block(pl.program_id(1) * SWA_BLOCKS_PER_STEP + sub, sink_ref, q_ref.at[rows, :], k_ref, v_ref, kc_ref, vc_ref,
                   cos_ref, sin_ref, band_ref, o_ref.at[rows, :], seqlen)


def _swa_block(j, sink_ref, q_ref, k_ref, v_ref, kc_ref, vc_ref, cos_ref, sin_ref, band_ref, o_ref, seqlen):
    blk = SWA_BLOCK
    nwin = 3 * blk
    kvw = SWA_KV_HEADS * HEAD_DIM
    q0 = pl.multiple_of(j * blk, blk)
    k_blk = jnp.clip(j - 1, 0, seqlen // blk - 3)
    k0 = pl.multiple_of(k_blk * blk, blk)
    q = _rope(q_ref[...].astype(F32), cos_ref[pl.ds(q0, blk), :], sin_ref[pl.ds(q0, blk), :])
    kw = _rope(k_ref[pl.ds(k0, nwin), :].astype(F32), cos_ref[pl.ds(k0, nwin), 0:kvw], sin_ref[pl.ds(k0, nwin), 0:kvw])
    kw = _bf(kw)
    vw = v_ref[pl.ds(k0, nwin), :]
    n_slot = N_HEADS
    qb = _bf(q)
    lane_kv = lax.broadcasted_iota(jnp.int32, (blk, kvw), 1) // HEAD_DIM
    qx = jnp.concatenate(
        [jnp.where(lane_kv == (slot % SWA_KV_HEADS), qb[:, (slot // SWA_KV_HEADS) * kvw:(slot // SWA_KV_HEADS + 1) * kvw],
                   jnp.zeros((), BF16)) for slot in range(n_slot)], axis=0)
    row_slot = lax.broadcasted_iota(jnp.int32, (n_slot * blk, 1), 0) // blk
    sink = jnp.zeros((n_slot * blk, 1), F32)
    for slot in range(n_slot):
        sink = jnp.where(row_slot == slot, sink_ref[SWA_HEAD_ORDER[slot]] * LOG2E, sink)
    band = band_ref[j - k_blk]
    s_band = _dot_nt(qx, kw) + jnp.concatenate([band] * n_slot, axis=0)
    s_ctx = _dot_nt(qx, kc_ref[0])
    m = jnp.maximum(jnp.maximum(jnp.max(s_band, axis=-1, keepdims=True), jnp.max(s_ctx, axis=-1, keepdims=True)), sink)
    p_band = jnp.exp2(s_band - m)
    p_ctx = jnp.exp2(s_ctx - m)
    l = jnp.sum(p_band, axis=-1, keepdims=True) + jnp.sum(p_ctx, axis=-1, keepdims=True) + jnp.exp2(sink - m)
    acc = (_dot(_bf(p_band), vw) + _dot(_bf(p_ctx), vc_ref[0])) / l
    halves = []
    for half in range(n_slot // SWA_KV_HEADS):
        r0 = half * SWA_KV_HEADS * blk
        halves.append(jnp.where(lane_kv == 0, acc[r0:r0 + blk], acc[r0 + blk:r0 + 2 * blk]))
    o_ref[...] = _bf(jnp.concatenate(halves, axis=-1))


def _swa_latent(zr, kc, vc, sink, batch, seqlen):
    n = batch * seqlen
    nb = seqlen // SWA_BLOCK
    kvw = SWA_KV_HEADS * HEAD_DIM
    n_ctx = kc.shape[1]
    cos, sin = _rope_tables(seqlen)
    assert nb >= 3 and nb % SWA_BLOCKS_PER_STEP == 0
    steps = nb // SWA_BLOCKS_PER_STEP
    tq = SWA_BLOCKS_PER_STEP * SWA_BLOCK
    a = np.arange(SWA_BLOCK)[:, None]
    c = np.arange(3 * SWA_BLOCK)[None, :]
    band = jnp.asarray(np.stack([np.where(np.abs(c - a - off * SWA_BLOCK) <= SWA_WINDOW, 0.0, -np.inf) for off in range(3)])
                       .astype(np.float32))
    return pl.pallas_call(
        functools.partial(_swa_lat_kernel, seqlen=seqlen),
        grid=(batch, steps),
        in_specs=[
            pl.BlockSpec(memory_space=pltpu.SMEM),
            pl.BlockSpec((tq, BRANCH_W), lambda b, j: (b * steps + j, COL_SQ)),
            pl.BlockSpec((seqlen, kvw), lambda b, j: (b, COL_SK128)),
            pl.BlockSpec((seqlen, kvw), lambda b, j: (b, COL_SV128)),
            pl.BlockSpec((1, n_ctx, kvw), lambda b, j: (b, 0, 0)),
            pl.BlockSpec((1, n_ctx, kvw), lambda b, j: (b, 0, 0)),
            _const_spec(cos.shape),
            _const_spec(sin.shape),
            _const_spec(band.shape),
        ],
        out_specs=pl.BlockSpec((tq, BRANCH_W), lambda b, j: (b * steps + j, 0)),
        out_shape=jax.ShapeDtypeStruct((n, BRANCH_W), BF16),
        compiler_params=_params("arbitrary", "arbitrary"),
        name="swa_latent",
    )(sink.astype(F32), zr, zr, zr, kc, vc, cos, sin, band)


def _smlp_kernel(u_ref, v_ref, ws_ref, bias_ref, ones_ref, o_ref, *, n_chunks):
    v = v_ref[...].astype(F32)
    ms = _dot01_right(v * v, ones_ref[...]) * (1.0 / HEAD_DIM)
    vn = _bf(v * lax.rsqrt(ms + EPS))
    lane_g = lax.broadcasted_iota(jnp.int32, (SMLP_CHUNK, BRANCH_W), 1) // HEAD_DIM
    for ci in range(n_chunks):
        rows = slice(ci * SMLP_CHUNK, (ci + 1) * SMLP_CHUNK)
        mixed = bias_ref[...]
        for g in range(SMLP_GROUPS):
            mixed = mixed + jnp.where(lane_g == g, _dot(ws_ref[g], vn[rows]), 0.0)
        o_ref[rows, :] = _bf(u_ref[rows, :].astype(F32) * mixed)


def _smlp(zr, ws, b, n_chunks):
    n = zr.shape[0]
    tm = n_chunks * SMLP_CHUNK
    bias = jnp.repeat(b.astype(F32).T, BRANCH_W // SMLP_GROUPS, axis=1)
    return pl.pallas_call(
        functools.partial(_smlp_kernel, n_chunks=n_chunks),
        grid=(n // tm,),
        in_specs=[
            pl.BlockSpec((tm, BRANCH_W), lambda i: (i, COL_MU)),
            pl.BlockSpec((tm, BRANCH_W), lambda i: (i, COL_MV)),
            _const_spec((SMLP_GROUPS, SMLP_CHUNK, SMLP_CHUNK)),
            _const_spec((SMLP_CHUNK, BRANCH_W)),
            _const_spec((BRANCH_W, BRANCH_W)),
        ],
        out_specs=pl.BlockSpec((tm, BRANCH_W), lambda i: (i, 0)),
        out_shape=jax.ShapeDtypeStruct((n, BRANCH_W), BF16),
        compiler_params=_params("arbitrary"),
        name="smlp",
    )(zr, zr, _bf(ws), bias, _group_ones())


def _merge_kernel(a_ref, b_ref, c_ref, d_ref, g0_ref, g1_ref, g2_ref, g3_ref, x_ref, gate_ref, sh_ref, sc_ref,
                  ng_ref, wb_ref, wo_ref, wr_ref, br_ref, tri_ref, upper_ref, x1_ref, h_ref, route_ref, seg_ref):
    tm = x_ref.shape[0]
    mix = None
    for br, gt, i in ((a_ref, g0_ref, 0), (b_ref, g1_ref, 1), (c_ref, g2_ref, 2), (d_ref, g3_ref, 3)):
        t = (1.0 + jnp.tanh(gt[...].astype(F32))) * _dot(br[...], wb_ref[i])
        mix = t if mix is None else mix + t
    x1 = x_ref[...] + gate_ref[0] * _dot(_bf(mix), wo_ref[...])
    x1_ref[...] = x1
    ms = jnp.mean(x1 * x1, axis=-1, keepdims=True)
    h = x1 * lax.rsqrt(ms + EPS) * ng_ref[...]
    h = h * (1.0 + sc_ref[0]) + sh_ref[0]
    h_ref[...] = _bf(h)

    hh = _bf(h)
    hm = _bf(h - hh.astype(F32))
    logits = (_dot(hh, wr_ref[0]) + _dot(hm, wr_ref[0]) + _dot(hh, wr_ref[1])) + br_ref[...]
    lane_i = lax.broadcasted_iota(jnp.int32, logits.shape, 1)
    lane = lane_i.astype(F32)
    lane_grp = (lane_i // EXPERTS_PER_GROUP).astype(F32)
    neg = -jnp.inf
    far = float(4 * N_EXPERTS)
    is_g = (lane_i >= N_EXPERTS) & (lane_i < N_EXPERTS + N_GROUPS)
    gl = jnp.where(is_g, logits, neg)
    gmax = jnp.max(gl, axis=-1, keepdims=True)
    gsum = jnp.sum(jnp.exp(gl - gmax), axis=-1, keepdims=True)
    g_top_p = 1.0 / gsum
    g_idx = jnp.min(jnp.where(is_g & (gl == gmax), lane, far), axis=-1, keepdims=True) - float(N_EXPERTS)
    in_grp = (lane_i < N_EXPERTS) & (lane_grp == g_idx)
    e_l = jnp.where(in_grp, logits, neg)
    e1 = jnp.max(e_l, axis=-1, keepdims=True)
    i1 = jnp.min(jnp.where(in_grp & (e_l == e1), lane, far), axis=-1, keepdims=True)
    e_l2 = jnp.where(lane == i1, neg, e_l)
    e2 = jnp.max(e_l2, axis=-1, keepdims=True)
    i2 = jnp.min(jnp.where(in_grp & (lane != i1) & (e_l2 == e2), lane, far), axis=-1, keepdims=True)
    t2 = jnp.exp(e2 - e1)
    w1 = g_top_p / (1.0 + t2)
    w2 = w1 * t2

    sel = (lane == i1) | (lane == i2)
    sel_f = jnp.where(sel, 1.0, 0.0)
    cum = _dot(tri_ref[...], _bf(sel_f))
    counts = cum[tm - 1:tm, :]
    padded = jnp.floor((counts + (MOE_ROW_ALIGN - 1)) * (1.0 / MOE_ROW_ALIGN)) * MOE_ROW_ALIGN
    seg_start = _dot(_bf(jnp.broadcast_to(padded, (SUBLANES, 128))), upper_ref[...])[0:1, :]
    slot = seg_start + cum - sel_f
    pos1 = jnp.sum(jnp.where(lane == i1, slot, 0.0), axis=-1, keepdims=True)
    pos2 = jnp.sum(jnp.where(lane == i2, slot, 0.0), axis=-1, keepdims=True)
    route_ref[...] = jnp.where(lane_i == 0, pos1, jnp.where(lane_i == 1, pos2, jnp.where(lane_i == 2, w1,
                               jnp.where(lane_i == 3, w2, 0.0))))
    seg = jnp.where(lane_i[0:1] < N_EXPERTS, seg_start, pltpu.roll(jnp.broadcast_to(padded, (SUBLANES, 128)), N_EXPERTS, 1)[0:1])
    seg_ref[0] = jnp.where(lane_i[0:1] < 2 * N_EXPERTS, seg, 0.0).astype(jnp.int32)


def _router_tables(w_rg, b_rg, w_re, b_re):
    w = jnp.zeros((D_MODEL, 128), F32)
    w = w.at[:, :N_EXPERTS].set(w_re.astype(F32)).at[:, N_EXPERTS:N_EXPERTS + N_GROUPS].set(w_rg.astype(F32))
    b = jnp.zeros((1, 128), F32)
    b = b.at[0, :N_EXPERTS].set(b_re.astype(F32)).at[0, N_EXPERTS:N_EXPERTS + N_GROUPS].set(b_rg.astype(F32))
    return jnp.stack(_split3(w)[:2], axis=0), b


def _merge(branches, zr, x, mods3, mod_row, norm_g, w_branch_bf, w_out_bf, wr3, br, layer):
    n = x.shape[0]
    per_layer = lambda shape: pl.BlockSpec((None,) + shape, lambda i: (layer,) + (0,) * len(shape))
    tm = MOE_TILE
    t = np.arange(tm)
    tri = jnp.asarray((t[None, :] <= t[:, None]).astype(np.float32), BF16)
    e = np.arange(128)
    upper = jnp.asarray((e[:, None] < e[None, :]).astype(np.float32), BF16)
    row = lambda w: pl.BlockSpec((tm, w), lambda i: (i, 0))
    gate = lambda k: pl.BlockSpec((tm, D_MODEL), lambda i: (i, COL_GATES1024 + k))
    mod = lambda k: pl.BlockSpec((1, 1, D_MODEL), lambda i: (mod_row(i, tm), 0, k))
    return pl.pallas_call(
        _merge_kernel,
        grid=(n // tm,),
        in_specs=[
            row(BRANCH_W), row(BRANCH_W), row(BRANCH_W), row(BRANCH_W),
            gate(0), gate(1), gate(2), gate(3),
            row(D_MODEL),
            mod(2), mod(3), mod(4),
            _const_spec((1, D_MODEL)),
            per_layer((N_BRANCH, BRANCH_W, D_MODEL)),
            per_layer((D_MODEL, D_MODEL)),
            per_layer((2, D_MODEL, 128)),
            per_layer((1, 128)),
            _const_spec((tm, tm)),
            _const_spec((128, 128)),
        ],
        out_specs=[row(D_MODEL), row(D_MODEL), row(128), pl.BlockSpec((1, 1, 128), lambda i: (i, 0, 0))],
        out_shape=[
            jax.ShapeDtypeStruct((n, D_MODEL), F32),
            jax.ShapeDtypeStruct((n, D_MODEL), BF16),
            jax.ShapeDtypeStruct((n, 128), F32),
            jax.ShapeDtypeStruct((n // tm, 1, 128), jnp.int32),
        ],
        compiler_params=_params("arbitrary"),
        name="merge",
    )(*branches, zr, zr, zr, zr, x, mods3, mods3, mods3, norm_g.reshape(1, D_MODEL), w_branch_bf, w_out_bf, wr3, br,
      tri, upper)


def _moe_kernel(seg_ref, h_ref, route_ref, x1_ref, gate_ref, fg_ref, wg_ref, wu_ref, wd_ref, *outs_and_scratch, final):
    if final:
        x2_ref, y_ref, hs_scr, ys_scr = outs_and_scratch
    else:
        x2_ref, hs_scr, ys_scr = outs_and_scratch
    tm = MOE_TILE
    tile = pl.program_id(0)
    route = route_ref[...]
    route_t = route.T
    pos1_row, pos2_row = route_t[0:1, :], route_t[1:2, :]
    h = h_ref[...]
    for rb in range(MOE_ROWS // MOE_GATHER_BLK):
        r = (rb * MOE_GATHER_BLK + lax.broadcasted_iota(jnp.int32, (MOE_GATHER_BLK, tm), 0)).astype(F32)
        p = jnp.where((r == pos1_row) | (r == pos2_row), 1.0, 0.0)
        hs_scr[rb * MOE_GATHER_BLK:(rb + 1) * MOE_GATHER_BLK, :] = _bf(_dot(_bf(p), h))
    ys_scr[...] = jnp.zeros_like(ys_scr)

    row_in_chunk = lax.broadcasted_iota(jnp.int32, (MOE_CHUNK, D_MODEL), 0)

    def expert_chunk(e, r0, end):
        r0 = pl.multiple_of(r0, MOE_ROW_ALIGN)
        rows = hs_scr[pl.ds(r0, MOE_CHUNK), :]
        a = _silu(_dot(rows, wg_ref[e])) * _dot(rows, wu_ref[e])
        y = _dot(_bf(a), wd_ref[e])
        ys_scr[pl.ds(r0, MOE_CHUNK), :] = jnp.where(row_in_chunk < end - r0, _bf(y), ys_scr[pl.ds(r0, MOE_CHUNK), :])

    starts = [seg_ref[tile * 128 + e] for e in range(N_EXPERTS)]
    ends = [starts[e] + seg_ref[tile * 128 + N_EXPERTS + e] for e in range(N_EXPERTS)]
    for e in range(N_EXPERTS):
        expert_chunk(e, starts[e], ends[e])
    for e in range(N_EXPERTS):
        n_chunks = lax.div(ends[e] - starts[e] + (MOE_CHUNK - 1), MOE_CHUNK)

        def more(c, carry, e=e):
            expert_chunk(e, starts[e] + c * MOE_CHUNK, ends[e])
            return carry

        lax.fori_loop(1, n_chunks, more, 0)

    pos1, pos2, w1, w2 = route[:, 0:1], route[:, 1:2], route[:, 2:3], route[:, 3:4]
    acc = None
    for cb in range(MOE_SEG_ROWS // MOE_SCATTER_BLK):
        r = (cb * MOE_SCATTER_BLK + lax.broadcasted_iota(jnp.int32, (tm, MOE_SCATTER_BLK), 1)).astype(F32)
        q = jnp.where(r == pos1, w1, jnp.where(r == pos2, w2, 0.0))
        part = _dot(_bf(q), ys_scr[cb * MOE_SCATTER_BLK:(cb + 1) * MOE_SCATTER_BLK, :])
        acc = part if acc is None else acc + part
    x2 = x1_ref[...] + gate_ref[0] * acc
    x2_ref[...] = x2
    if final:
        ms = jnp.mean(x2 * x2, axis=-1, keepdims=True)
        y_ref[...] = x2 * lax.rsqrt(ms + EPS) * fg_ref[...]


def _moe(h, route, seg, x1, mods3, mod_row, final_g, wg_bf, wu_bf, wd_bf, layer, final):
    n = x1.shape[0]
    tm = MOE_TILE
    row = lambda w: pl.BlockSpec((tm, w), lambda i, s: (i, 0))
    resident = lambda shape: pl.BlockSpec((None,) + shape, lambda i, s: (layer,) + (0,) * len(shape),
                                          pipeline_mode=pl.Buffered(1))
    out_specs = [row(D_MODEL)]
    out_shape = [jax.ShapeDtypeStruct((n, D_MODEL), F32)]
    if final:
        out_specs.append(row(D_MODEL))
        out_shape.append(jax.ShapeDtypeStruct((n, D_MODEL), F32))
    return pl.pallas_call(
        functools.partial(_moe_kernel, final=final),
        grid_spec=pltpu.PrefetchScalarGridSpec(
            num_scalar_prefetch=1,
            grid=(n // tm,),
            in_specs=[
                row(D_MODEL), row(128), row(D_MODEL),
                pl.BlockSpec((1, 1, D_MODEL), lambda i, s: (mod_row(i, tm), 0, 5)),
                pl.BlockSpec((1, D_MODEL), lambda i, s: (0, 0)),
                resident((N_EXPERTS, D_MODEL, EXPERT_FF)),
                resident((N_EXPERTS, D_MODEL, EXPERT_FF)),
                resident((N_EXPERTS, EXPERT_FF, D_MODEL)),
            ],
            out_specs=out_specs,
            scratch_shapes=[pltpu.VMEM((MOE_ROWS, D_MODEL), BF16), pltpu.VMEM((MOE_ROWS, D_MODEL), BF16)],
        ),
        out_shape=out_shape,
        compiler_params=_params("arbitrary"),
        name="moe_final" if final else "moe",
    )(seg.reshape(-1), h, route, x1, mods3, final_g.reshape(1, D_MODEL), wg_bf, wu_bf, wd_bf)


def _state_to_blockdiag_t(s):
    b = s.shape[0]
    st = jnp.swapaxes(s.astype(F32), -1, -2)
    eye = jnp.eye(N_HEADS, dtype=F32)
    full = st[:, :, :, :, None, :] * eye[None, None, :, None, :, None]
    return full.reshape(b, 2, BRANCH_W, BRANCH_W)


def _compact_to_state(fin):
    b = fin.shape[0]
    return jnp.transpose(fin.reshape(b, 2, HEAD_DIM, N_HEADS, HEAD_DIM), (0, 1, 3, 4, 2))


def _layer(x, l, w, mods3, mod_row, batch, seqlen, latent, st0, caches, final):
    proj = _in_proj(x, mods3, mod_row, w["norm1_g"][l], w["w_in"], l, IN_TILE, not latent)
    zf, zr = proj[0], proj[1]
    kv = None if latent else proj[2:]
    o_f, o_b, fin = _hgrn_scan(zf, w["lb"][l], st0, batch, seqlen, n_sub=min(HGRN_CHUNKS_PER_STEP, seqlen // HGRN_CHUNK))
    a_out = _hgrn_post(o_f, o_b, zr, w["onorm_g"][l], ROW_TILE)
    c_out = _smlp(zr, w["smlp_ws"][l], w["smlp_b"][l], n_chunks=4)
    if latent:
        ck_na, cv_na, ck_swa, cv_swa = caches
        b_out = _na_latent(zr, ck_na, cv_na, w["na_bias"][l], batch, seqlen)
        d_out = _swa_latent(zr, ck_swa, cv_swa, w["swa_sink"][l], batch, seqlen)
    else:
        b_out = _ctx_attn(zr, batch, seqlen, COL_NAQ, COL_NAK, COL_NAV, BRANCH_W, N_HEADS, None)
        d_out = _ctx_attn(zr, batch, seqlen, COL_SQ, COL_SK128, COL_SV128, SWA_KV_HEADS * HEAD_DIM, SWA_KV_HEADS,
                          w["swa_sink"][l])
    x1, h2, route, seg = _merge((a_out, b_out, c_out, d_out), zr, x, mods3, mod_row, w["norm2_g"][l], w["w_branch"],
                                w["w_out"], w["wr"], w["br"], l)
    out = _moe(h2, route, seg, x1, mods3, mod_row, w["final_g"], w["wg"], w["wu"], w["wd"], l, final)
    return out, kv, fin


def kernel(x_prompt, x_sample, c, cache_na_k, cache_na_v, cache_swa_k, cache_swa_v, state_hgrn, c_ctx, w_ada, b_ada, norm1_g, norm2_g, w_in, hgrn_lb, hgrn_onorm_g, na_rpb, smlp_ws, smlp_b, swa_sink, w_branch, w_out, router_g_w, router_g_b, router_e_w, router_e_b, moe_w_gate, moe_w_up, moe_w_down, final_g):
    bc, lc, _ = x_prompt.shape
    bl, ll, _ = x_sample.shape
    n_ctx_tok = bc * lc

    cond = jnp.zeros((MOD_ROWS, D_MODEL), F32).at[0].set(c_ctx.astype(F32)).at[1:1 + bl].set(c.astype(F32))
    mods = _ada_mods(cond, w_ada, b_ada)

    lb_soft = jax.nn.softmax(hgrn_lb.astype(F32), axis=0)
    lb_all = jnp.cumsum(lb_soft, axis=0) - lb_soft[0:1]

    col_scale = np.ones((P_IN,), np.float32)
    col_scale[ZF_W + COL_GATES1024 * 1024:] = 0.5
    for q_col in (COL_NAQ, COL_SQ):
        col_scale[ZF_W + q_col * BRANCH_W:ZF_W + (q_col + 1) * BRANCH_W] = QK_PRESCALE
    swa_rows = jnp.concatenate([w_branch[:, N_BRANCH - 1, h * HEAD_DIM:(h + 1) * HEAD_DIM] for h in SWA_HEAD_ORDER], axis=1)
    routers = [_router_tables(router_g_w[l], router_g_b[l], router_e_w[l], router_e_b[l]) for l in range(DEPTH)]
    w = dict(
        norm1_g=norm1_g, norm2_g=norm2_g, lb=lb_all, onorm_g=hgrn_onorm_g, smlp_ws=smlp_ws, smlp_b=smlp_b,
        swa_sink=swa_sink, final_g=final_g,
        w_in=_bf(w_in * jnp.asarray(col_scale)[None, None, :]),
        w_branch=_bf(w_branch.at[:, N_BRANCH - 1].set(swa_rows)),
        w_out=_bf(0.5 * w_out),
        wr=jnp.stack([r[0] for r in routers]), br=jnp.stack([r[1] for r in routers]),
        wg=_bf(moe_w_gate), wu=_bf(moe_w_up), wd=_bf(moe_w_down),
        na_bias=_na_bias_tables(na_rpb),
    )

    ctx_row = lambda i, tm: 0
    lat_row = lambda i, tm: 1 + i // (ll // tm)
    xp = x_prompt.reshape(n_ctx_tok, D_MODEL)
    kvs, states = [], []
    y_prompt = None
    for l in range(DEPTH):
        final = l == DEPTH - 1
        out, kv, fin = _layer(xp, l, w, mods[l].reshape(MOD_ROWS, 1, -1), ctx_row, bc, lc, False, None, None, final)
        if final:
            xp, y_prompt = out
        else:
            xp = out[0]
        kvs.append(kv)
        states.append(_compact_to_state(fin))
    cache_out = [jnp.stack([kvs[l][k].reshape(bc, lc, -1, HEAD_DIM) for l in range(DEPTH)], axis=1) for k in range(4)]

    xs = x_sample.reshape(bl * ll, D_MODEL)
    y_sample = None
    n_past = cache_na_k.shape[2]
    for l in range(DEPTH):
        caches = (_bf(cache_na_k[:, l]).reshape(bl, n_past, BRANCH_W), _bf(cache_na_v[:, l]).reshape(bl, n_past, BRANCH_W),
                  _bf(cache_swa_k[:, l]).reshape(bl, n_past, SWA_KV_HEADS * HEAD_DIM),
                  _bf(cache_swa_v[:, l]).reshape(bl, n_past, SWA_KV_HEADS * HEAD_DIM))
        final = l == DEPTH - 1
        out, _, _ = _layer(xs, l, w, mods[l].reshape(MOD_ROWS, 1, -1), lat_row, bl, ll, True,
                           _state_to_blockdiag_t(state_hgrn[:, l]), caches, final)
        if final:
            xs, y_sample = out
        else:
            xs = out[0]

    return (y_prompt.reshape(bc, lc, D_MODEL), y_sample.reshape(bl, ll, D_MODEL), *cache_out, jnp.stack(states, axis=1))
```

```python
import functools

import numpy as np
import jax
import jax.numpy as jnp
from jax import lax
from jax.experimental import pallas as pl
from jax.experimental.pallas import tpu as pltpu

D_MODEL = 1024
DEPTH = 2
GRID_W = 64
HEAD_DIM = 64
N_BRANCH = 4
BRANCH_W = 256
N_HEADS = 4
HGRN_CHUNK = 64
HGRN_CHUNKS_PER_STEP = 8
NA_ROWS = 8
NA_COLS = 16
SMLP_GROUPS = 4
SMLP_CHUNK = 128
SWA_KV_HEADS = 2
SWA_HEAD_ORDER = (0, 2, 1, 3)
SWA_WINDOW = 128
SWA_BLOCK = 128
SWA_BLOCKS_PER_STEP = 4
ROPE_THETA = 10000.0
N_GROUPS = 4
EXPERTS_PER_GROUP = 4
N_EXPERTS = 16
EXPERT_FF = 256
ADA_CHUNKS = 6
EPS = 1e-6
TINY = 1e-30
P_IN = 7168
ATT_SCALE = HEAD_DIM ** -0.5
LOG2E = 1.4426950408889634
QK_PRESCALE = ATT_SCALE * LOG2E

ZF_W = 1024
ZR_W = P_IN - ZF_W
COL_HQ, COL_HI, COL_HFF, COL_HFB = 0, 1, 2, 3
COL_HOG, COL_NAQ, COL_NAK, COL_NAV = 0, 1, 2, 3
COL_MU, COL_MV, COL_SQ = 4, 5, 6
COL_SK128, COL_SV128 = 14, 15
COL_GATES1024 = 2
IN_COL_CHUNK = 1024
IN_TILE = 512

MOD_ROWS = 16
VMEM_LIMIT = 56 * 1024 * 1024

F32 = jnp.float32
BF16 = jnp.bfloat16

MOE_TILE = 512
MOE_ROW_ALIGN = 16
MOE_CHUNK = 96
MOE_GATHER_BLK = 128
MOE_SCATTER_BLK = 256
MOE_SEG_ROWS = -(-(2 * MOE_TILE + N_EXPERTS * (MOE_ROW_ALIGN - 1)) // MOE_SCATTER_BLK) * MOE_SCATTER_BLK
MOE_ROWS = -(-(2 * MOE_TILE + N_EXPERTS * (MOE_ROW_ALIGN - 1) + MOE_CHUNK) // MOE_GATHER_BLK) * MOE_GATHER_BLK
HGRN_LEVELS = (0, 1, 2, 4, 8, 16, 32)
SUBLANES = 8
HGRN_MXU_REF_LEVELS = ()


def _bf(x):
    return x.astype(BF16)


def _dot(a, b):
    return jnp.dot(a, b, preferred_element_type=F32)


def _dot_nt(a, b):
    return lax.dot_general(a, b, (((1,), (1,)), ((), ())), preferred_element_type=F32)


def _dot_tn(a, b):
    return lax.dot_general(a, b, (((0,), (0,)), ((), ())), preferred_element_type=F32)


def _split3(x):
    hi = _bf(x)
    r1 = x - hi.astype(F32)
    mid = _bf(r1)
    lo = _bf(r1 - mid.astype(F32))
    return hi, mid, lo


def _dot01_left(m01, x):
    hi, mid, lo = _split3(x)
    return _dot(m01, hi) + _dot(m01, mid) + _dot(m01, lo)


def _dot01_right(x, m01):
    hi, mid, lo = _split3(x)
    return _dot(hi, m01) + _dot(mid, m01) + _dot(lo, m01)


def _sigmoid(x):
    return 0.5 * jnp.tanh(0.5 * x) + 0.5


def _silu(x):
    return x * _sigmoid(x)


def _params(*sem):
    return pltpu.CompilerParams(dimension_semantics=sem, vmem_limit_bytes=VMEM_LIMIT)


def _const_spec(shape):
    n = len(shape)
    return pl.BlockSpec(shape, lambda *_: (0,) * n)


def _head_mask():
    row = lax.broadcasted_iota(jnp.int32, (N_HEADS * HEAD_DIM, BRANCH_W), 0)
    lane = lax.broadcasted_iota(jnp.int32, (N_HEADS * HEAD_DIM, BRANCH_W), 1)
    return (row // HEAD_DIM) == (lane // HEAD_DIM)


def _expand_heads(x, head_mask):
    return jnp.where(head_mask, jnp.concatenate([x] * N_HEADS, axis=0), jnp.zeros((), x.dtype))


def _collapse_heads(r, head_mask):
    r = jnp.where(head_mask, r, 0.0)
    n = HEAD_DIM
    return (r[0:n] + r[n:2 * n]) + (r[2 * n:3 * n] + r[3 * n:4 * n])


def _ada_kernel(cond_ref, w_ref, b_ref, o_ref):
    s = _silu(cond_ref[...])
    o_ref[0] = _dot(_bf(s), _bf(w_ref[0])) + b_ref[0]


def _ada_mods(cond, w_ada, b_ada):
    tn = 1536
    n = ADA_CHUNKS * D_MODEL
    return pl.pallas_call(
        _ada_kernel,
        grid=(DEPTH, n // tn),
        in_specs=[
            pl.BlockSpec((MOD_ROWS, D_MODEL), lambda l, j: (0, 0)),
            pl.BlockSpec((1, D_MODEL, tn), lambda l, j: (l, 0, j)),
            pl.BlockSpec((1, 1, tn), lambda l, j: (l, 0, j)),
        ],
        out_specs=pl.BlockSpec((1, MOD_ROWS, tn), lambda l, j: (l, 0, j)),
        out_shape=jax.ShapeDtypeStruct((DEPTH, MOD_ROWS, n), F32),
        compiler_params=_params("arbitrary", "arbitrary"),
        name="ada_mods",
    )(cond, w_ada, b_ada.reshape(DEPTH, 1, n))


def _smlp_tile(u, v, ws_ref, bias_ref, ones_ref):
    ms = _dot01_right(v * v, ones_ref[...]) * (1.0 / HEAD_DIM)
    vn = _bf(v * lax.rsqrt(ms + EPS))
    lane_g = lax.broadcasted_iota(jnp.int32, (SMLP_CHUNK, BRANCH_W), 1) // HEAD_DIM
    outs = []
    for ci in range(u.shape[0] // SMLP_CHUNK):
        rows = slice(ci * SMLP_CHUNK, (ci + 1) * SMLP_CHUNK)
        mixed = bias_ref[...]
        for g in range(SMLP_GROUPS):
            mixed = mixed + jnp.where(lane_g == g, _dot(ws_ref[g], vn[rows]), 0.0)
        outs.append(u[rows] * mixed)
    return jnp.concatenate(outs, axis=0)


def _in_kernel(x_ref, sh_ref, sc_ref, g_ref, w_ref, ws_ref, bias_ref, ones_ref, zf_ref, zr_ref, c_ref, *rest, want_kv):
    x = x_ref[...]
    ms = jnp.mean(x * x, axis=-1, keepdims=True)
    h = x * lax.rsqrt(ms + EPS) * g_ref[...]
    h = _bf(h * (1.0 + sc_ref[0]) + sh_ref[0])
    tn = IN_COL_CHUNK
    kvw = SWA_KV_HEADS * HEAD_DIM
    for j in range(P_IN // tn):
        acc = _dot(h, w_ref[:, j * tn:(j + 1) * tn])
        if j == 0:
            zf_ref[...] = acc
            continue
        c0 = (j - 1) * tn
        sq0 = COL_SQ * BRANCH_W - c0
        if 0 <= sq0 < tn:
            heads = [acc[:, sq0 + hd * HEAD_DIM:sq0 + (hd + 1) * HEAD_DIM] for hd in SWA_HEAD_ORDER]
            zr_ref[:, c0:c0 + tn] = _bf(jnp.concatenate([acc[:, :sq0]] + heads + [acc[:, sq0 + BRANCH_W:]], axis=1))
        else:
            zr_ref[:, c0:c0 + tn] = _bf(acc)
        mu0 = COL_MU * BRANCH_W - c0
        if 0 <= mu0 and mu0 + 2 * BRANCH_W <= tn:
            assert COL_MV == COL_MU + 1
            c_ref[...] = _bf(_smlp_tile(acc[:, mu0:mu0 + BRANCH_W], acc[:, mu0 + BRANCH_W:mu0 + 2 * BRANCH_W],
                                        ws_ref, bias_ref, ones_ref))
        if want_kv:
            nak_ref, nav_ref, sk_ref, sv_ref = rest
            for ref, col, width in ((nak_ref, COL_NAK * BRANCH_W, BRANCH_W), (nav_ref, COL_NAV * BRANCH_W, BRANCH_W),
                                    (sk_ref, COL_SK128 * kvw, kvw), (sv_ref, COL_SV128 * kvw, kvw)):
                if c0 <= col < c0 + tn:
                    ref[...] = acc[:, col - c0:col - c0 + width]


def _in_proj(x, mods3, mod_row, norm_g, w_in_bf, smlp_ws, smlp_b, layer, tm, want_kv):
    n = x.shape[0]
    assert ZF_W == IN_COL_CHUNK and tm % SMLP_CHUNK == 0
    kvw = SWA_KV_HEADS * HEAD_DIM
    bias = jnp.repeat(smlp_b.astype(F32).T, BRANCH_W // SMLP_GROUPS, axis=1)
    row = lambda w: pl.BlockSpec((tm, w), lambda i: (i, 0))
    out_specs = [row(ZF_W), row(ZR_W), row(BRANCH_W)]
    out_shape = [jax.ShapeDtypeStruct((n, ZF_W), F32), jax.ShapeDtypeStruct((n, ZR_W), BF16),
                 jax.ShapeDtypeStruct((n, BRANCH_W), BF16)]
    if want_kv:
        for width in (BRANCH_W, BRANCH_W, kvw, kvw):
            out_specs.append(row(width))
            out_shape.append(jax.ShapeDtypeStruct((n, width), F32))
    return pl.pallas_call(
        functools.partial(_in_kernel, want_kv=want_kv),
        grid=(n // tm,),
        in_specs=[
            row(D_MODEL),
            pl.BlockSpec((1, 1, D_MODEL), lambda i: (mod_row(i, tm), 0, 0)),
            pl.BlockSpec((1, 1, D_MODEL), lambda i: (mod_row(i, tm), 0, 1)),
            _const_spec((1, D_MODEL)),
            pl.BlockSpec((None, D_MODEL, P_IN), lambda i: (layer, 0, 0), pipeline_mode=pl.Buffered(1)),
            _const_spec((SMLP_GROUPS, SMLP_CHUNK, SMLP_CHUNK)),
            _const_spec((SMLP_CHUNK, BRANCH_W)),
            _const_spec((BRANCH_W, BRANCH_W)),
        ],
        out_specs=out_specs,
        out_shape=out_shape,
        compiler_params=_params("arbitrary"),
        name="in_proj_kv" if want_kv else "in_proj",
    )(x, mods3, mods3, norm_g.reshape(1, D_MODEL), w_in_bf, _bf(smlp_ws), bias, _group_ones())


def _hgrn_tables():
    c = HGRN_CHUNK
    t = np.arange(c)
    tsel = np.zeros((2, (1 + len(HGRN_MXU_REF_LEVELS)) * c, c), np.float32)
    pmask = np.zeros((2, len(HGRN_LEVELS), c, N_HEADS * c), np.float32)
    for rev in (0, 1):
        cum = (t[None, :] >= t[:, None]) if rev else (t[None, :] <= t[:, None])
        tsel[rev, :c] = cum
        pmask[rev, 0] = np.tile(np.eye(c, dtype=np.float32), (1, N_HEADS))
        for li, m in enumerate(HGRN_LEVELS[1:], start=1):
            if m in HGRN_MXU_REF_LEVELS:
                slot = 1 + HGRN_MXU_REF_LEVELS.index(m)
                tsel[rev, slot * c:(slot + 1) * c] = cum[(t // (2 * m)) * (2 * m) + (m - 1 if rev else m)]
            same = (t[:, None] // (2 * m)) == (t[None, :] // (2 * m))
            q_half = ((t & m) == 0) if rev else ((t & m) != 0)
            k_half = ~q_half
            pmask[rev, li] = np.tile((same & q_half[:, None] & k_half[None, :]).astype(np.float32), (1, N_HEADS))
    return jnp.asarray(tsel, BF16), jnp.asarray(pmask, F32)


def _ref_rows(b_ref, m, rev):
    c = HGRN_CHUNK
    off = (m - 1) if rev else m
    row = lambda r, n: jnp.broadcast_to(b_ref[pl.ds(r, 1), :], (n, BRANCH_W))
    if 2 * m >= SUBLANES:
        return jnp.concatenate([row(s + off, 2 * m) for s in range(0, c, 2 * m)], axis=0)
    sub = lax.broadcasted_iota(jnp.int32, (SUBLANES, BRANCH_W), 0)
    tiles = []
    for t0 in range(0, c, SUBLANES):
        cur = row(t0 + off, SUBLANES)
        for s in range(2 * m, SUBLANES, 2 * m):
            cur = jnp.where(sub >= s, row(t0 + s + off, SUBLANES), cur)
        tiles.append(cur)
    return jnp.concatenate(tiles, axis=0)


def _hgrn_chunk(q_raw, v, f_raw, lb, st, tsel, pmask_ref, rev, head_mask, b_ref):
    c = HGRN_CHUNK
    qq = _silu(q_raw)
    f = lb + (1.0 - lb) * _sigmoid(f_raw)
    lf = jnp.log2(jnp.maximum(f, TINY))
    k = 1.0 - f
    ball = _dot01_left(tsel, lf)
    b = ball[:c]
    b_ref[...] = b
    bl = b[0:1] if rev else b[c - 1:c]
    vb = _bf(v)
    v_x = _expand_heads(vb, head_mask)

    o = _dot_nt(_bf(qq * jnp.exp2(b)), _bf(st))

    p = None
    qb, kb = _bf(qq), _bf(k)
    for li, m in enumerate(HGRN_LEVELS):
        if m == 0:
            qe, ke = qb, kb
        else:
            if m in HGRN_MXU_REF_LEVELS:
                slot = 1 + HGRN_MXU_REF_LEVELS.index(m)
                ref = ball[slot * c:(slot + 1) * c]
            else:
                ref = _ref_rows(b_ref, m, rev)
            e = _bf(jnp.exp2(-jnp.abs(b - ref)))
            qe, ke = qb * e, kb * e
        s = _dot_nt(qe, _expand_heads(ke, head_mask)) * pmask_ref[li]
        p = s if p is None else p + s
    o = o + _dot(_bf(p), v_x)

    ke_state = _bf(k * jnp.exp2(bl - b))
    st_new = st * jnp.exp2(bl) + jnp.where(head_mask, _dot_tn(vb, ke_state), 0.0)
    return o, st_new


def _hgrn_finish(o, og, g_ref, ones_ref):
    ms = _dot01_right(o * o, ones_ref[...]) * (1.0 / HEAD_DIM)
    return _bf(o * lax.rsqrt(ms + EPS) * g_ref[...] * _silu(og.astype(F32)))


def _hgrn_kernel(*refs, n_sub, n_blk, has_init):
    qf_ref, vf_ref, ff_ref, qb_ref, vb_ref, fb_ref, ogf_ref, ogb_ref, lb_ref, g_ref, ones_ref = refs[:11]
    n_in = 14 if has_init else 13
    st0_ref = refs[11] if has_init else None
    tsel_ref, pmask_ref = refs[n_in - 2], refs[n_in - 1]
    if n_blk == 1:
        af_ref, fin_ref, st_ref, b_scr, blk_scr = refs[n_in:]
    else:
        af_ref, ab_ref, fin_ref, st_ref, b_scr, blk_scr, keep_scr = refs[n_in:]
    c = HGRN_CHUNK
    step = pl.program_id(1)

    @pl.when(step == 0)
    def _():
        st_ref[...] = st0_ref[0] if has_init else jnp.zeros_like(st_ref)

    head_mask = _head_mask()
    st_f = st_ref[0]
    st_b = st_ref[1]
    for j in range(n_sub):
        rf = slice(j * c, (j + 1) * c)
        rb = slice((n_sub - 1 - j) * c, (n_sub - j) * c)
        o_f, st_f = _hgrn_chunk(qf_ref[rf, :], vf_ref[rf, :], ff_ref[rf, :], lb_ref[0:1, :], st_f,
                                tsel_ref[0], pmask_ref.at[0], False, head_mask, b_scr.at[2 * j])
        o_b, st_b = _hgrn_chunk(qb_ref[rb, :], vb_ref[rb, :], fb_ref[rb, :], lb_ref[1:2, :], st_b,
                                tsel_ref[1], pmask_ref.at[1], True, head_mask, b_scr.at[2 * j + 1])
        blk_scr[0, rf, :] = o_f
        blk_scr[1, rb, :] = o_b
    st_ref[0] = st_f
    st_ref[1] = st_b

    if n_blk == 1:
        af_ref[...] = _hgrn_finish(blk_scr[0] + blk_scr[1], ogf_ref[...], g_ref, ones_ref)
    else:
        half = n_blk // 2

        @pl.when(step < half)
        def _():
            keep_scr[0, step] = blk_scr[0]
            keep_scr[1, step] = blk_scr[1]

        @pl.when(step >= half)
        def _():
            other = n_blk - 1 - step
            af_ref[...] = _hgrn_finish(blk_scr[0] + keep_scr[1, other], ogf_ref[...], g_ref, ones_ref)
            ab_ref[...] = _hgrn_finish(blk_scr[1] + keep_scr[0, other], ogb_ref[...], g_ref, ones_ref)

    @pl.when(step == n_blk - 1)
    def _():
        fin_ref[0, 0] = _collapse_heads(st_f, head_mask)
        fin_ref[0, 1] = _collapse_heads(st_b, head_mask)


def _hgrn_mixer(zf, zr, lb2, onorm_g, st0, batch, seqlen, n_sub):
    n = batch * seqlen
    tb = n_sub * HGRN_CHUNK
    n_blk = seqlen // tb
    assert n_blk == 1 or n_blk % 2 == 0
    half = n_blk // 2
    tsel, pmask = _hgrn_tables()
    has_init = st0 is not None
    fwd = lambda col: pl.BlockSpec((tb, BRANCH_W), lambda b, c: (b * n_blk + c, col))
    bwd = lambda col: pl.BlockSpec((tb, BRANCH_W), lambda b, c: (b * n_blk + n_blk - 1 - c, col))
    in_specs = [fwd(COL_HQ), fwd(COL_HI), fwd(COL_HFF), bwd(COL_HQ), bwd(COL_HI), bwd(COL_HFB), fwd(COL_HOG), bwd(COL_HOG),
                _const_spec((2, BRANCH_W)), _const_spec((1, BRANCH_W)), _const_spec((BRANCH_W, BRANCH_W))]
    args = [zf, zf, zf, zf, zf, zf, zr, zr, lb2, onorm_g.reshape(1, BRANCH_W), _group_ones()]
    if has_init:
        in_specs.append(pl.BlockSpec((1, 2, BRANCH_W, BRANCH_W), lambda b, c: (b, 0, 0, 0)))
        args.append(st0)
    in_specs += [_const_spec(tsel.shape), _const_spec(pmask.shape)]
    args += [tsel, pmask]
    out_specs = [pl.BlockSpec((tb, BRANCH_W), lambda b, c: (b * n_blk + jnp.maximum(c, half), 0))]
    out_shape = [jax.ShapeDtypeStruct((n, BRANCH_W), BF16)]
    scratch = [pltpu.VMEM((2, BRANCH_W, BRANCH_W), F32), pltpu.VMEM((2 * n_sub, HGRN_CHUNK, BRANCH_W), F32),
               pltpu.VMEM((2, tb, BRANCH_W), F32)]
    if n_blk > 1:
        out_specs.append(pl.BlockSpec((tb, BRANCH_W), lambda b, c: (b * n_blk + jnp.minimum(n_blk - 1 - c, half - 1), 0)))
        out_shape.append(jax.ShapeDtypeStruct((n, BRANCH_W), BF16))
        scratch.append(pltpu.VMEM((2, half, tb, BRANCH_W), F32))
    out_specs.append(pl.BlockSpec((1, 2, HEAD_DIM, BRANCH_W), lambda b, c: (b, 0, 0, 0)))
    out_shape.append(jax.ShapeDtypeStruct((batch, 2, HEAD_DIM, BRANCH_W), F32))
    res = pl.pallas_call(
        functools.partial(_hgrn_kernel, n_sub=n_sub, n_blk=n_blk, has_init=has_init),
        grid=(batch, n_blk),
        in_specs=in_specs,
        out_specs=out_specs,
        out_shape=out_shape,
        scratch_shapes=scratch,
        compiler_params=_params("arbitrary", "arbitrary"),
        name="hgrn_mixer",
    )(*args)
    if n_blk == 1:
        return res[0], res[1]
    a_f, a_b, fin = res
    blocks = lambda a: a.reshape(batch, n_blk, tb, BRANCH_W)
    a = jnp.concatenate([blocks(a_b)[:, :half], blocks(a_f)[:, half:]], axis=1).reshape(n, BRANCH_W)
    return a, fin


def _group_ones():
    g = np.arange(BRANCH_W) // HEAD_DIM
    return jnp.asarray((g[:, None] == g[None, :]).astype(np.float32), BF16)


def _ctx_attn_kernel(*refs, n_q, n_kv, has_sink):
    if has_sink:
        sink_ref, q_ref, k_ref, v_ref, o_ref = refs
    else:
        q_ref, k_ref, v_ref, o_ref = refs
    group = n_q // n_kv
    for slot, hq in enumerate(SWA_HEAD_ORDER if has_sink else range(n_q)):
        hk = hq // group
        q = q_ref[:, slot * HEAD_DIM:(slot + 1) * HEAD_DIM]
        k = k_ref[:, hk * HEAD_DIM:(hk + 1) * HEAD_DIM]
        v = v_ref[:, hk * HEAD_DIM:(hk + 1) * HEAD_DIM]
        s = _dot_nt(q, k)
        m = jnp.max(s, axis=-1, keepdims=True)
        if has_sink:
            sink = sink_ref[hq] * LOG2E
            m = jnp.maximum(m, sink)
        p = jnp.exp2(s - m)
        l = jnp.sum(p, axis=-1, keepdims=True)
        if has_sink:
            l = l + jnp.exp2(sink - m)
        o_ref[:, slot * HEAD_DIM:(slot + 1) * HEAD_DIM] = _bf(_dot(_bf(p), v) / l)


def _ctx_attn(zr, batch, seqlen, q_col, k_col, v_col, kv_width, n_kv, sink):
    n = batch * seqlen
    n_q = N_HEADS
    has_sink = sink is not None
    in_specs = [
        pl.BlockSpec((seqlen, BRANCH_W), lambda b: (b, q_col)),
        pl.BlockSpec((seqlen, kv_width), lambda b: (b, k_col)),
        pl.BlockSpec((seqlen, kv_width), lambda b: (b, v_col)),
    ]
    args = [zr, zr, zr]
    if has_sink:
        in_specs = [pl.BlockSpec(memory_space=pltpu.SMEM)] + in_specs
        args = [sink.astype(F32)] + args
    return pl.pallas_call(
        functools.partial(_ctx_attn_kernel, n_q=n_q, n_kv=n_kv, has_sink=has_sink),
        grid=(batch,),
        in_specs=in_specs,
        out_specs=pl.BlockSpec((seqlen, BRANCH_W), lambda b: (b, 0)),
        out_shape=jax.ShapeDtypeStruct((n, BRANCH_W), BF16),
        compiler_params=_params("arbitrary"),
        name="ctx_attn_sink" if has_sink else "ctx_attn",
    )(*args)


def _na_bias_kernel(rpb_ref, onehot_ref, mask_ref, o_ref):
    o_ref[...] = (_dot01_right(rpb_ref[...], onehot_ref[...]) + mask_ref[...]) * LOG2E


def _na_bias_tables(na_rpb):
    n_dr, n_dc = 2 * NA_ROWS - 1, 2 * NA_COLS - 1
    col = np.arange(GRID_W)
    col_start = np.clip(col - NA_COLS // 2, 0, GRID_W - NA_COLS)
    col_mask = (col[None, :] >= col_start[:, None]) & (col[None, :] < col_start[:, None] + NA_COLS)
    d_col = np.clip(col[None, :] - col[:, None], -(NA_COLS - 1), NA_COLS - 1) + (NA_COLS - 1)
    onehot = (np.arange(128)[:, None] == d_col.reshape(1, -1)).astype(np.float32)
    mask_add = np.where(col_mask.reshape(1, -1), 0.0, -np.inf).astype(np.float32)
    rows = DEPTH * N_HEADS * n_dr
    rpb2 = jnp.zeros((128, 128), F32).at[:rows, :n_dc].set(na_rpb.astype(F32).reshape(rows, n_dc))
    tab = pl.pallas_call(
        _na_bias_kernel,
        grid=(1,),
        in_specs=[_const_spec((128, 128)), _const_spec((128, GRID_W * GRID_W)), _const_spec((1, GRID_W * GRID_W))],
        out_specs=_const_spec((128, GRID_W * GRID_W)),
        out_shape=jax.ShapeDtypeStruct((128, GRID_W * GRID_W), F32),
        compiler_params=_params("arbitrary"),
        name="na_bias",
    )(rpb2, jnp.asarray(onehot, BF16), jnp.asarray(mask_add))
    tab = tab[:rows].reshape(DEPTH, N_HEADS, n_dr, GRID_W, GRID_W)
    slabs = [jnp.transpose(tab[:, :, first:first + NA_ROWS], (0, 1, 3, 2, 4)).reshape(DEPTH, N_HEADS * GRID_W, NA_ROWS * GRID_W)
             for first in range(NA_ROWS)]
    return jnp.stack(slabs, axis=1)


def _na_lat_kernel(q_ref, k_ref, v_ref, kc_ref, vc_ref, bias_ref, o_ref, *, rows_per_step, n_rows):
    nk = NA_ROWS * GRID_W
    r0 = pl.program_id(1) * rows_per_step
    head_mask = _head_mask()

    def body(i, carry):
        r = r0 + i
        row_start = jnp.clip(r - NA_ROWS // 2, 0, n_rows - NA_ROWS)
        first = row_start - r + (NA_ROWS - 1)
        k0 = pl.multiple_of(row_start * GRID_W, GRID_W)
        q0 = pl.multiple_of(i * GRID_W, GRID_W)
        qx = _expand_heads(q_ref[pl.ds(q0, GRID_W), :], head_mask)
        s_lat = _dot_nt(qx, k_ref[pl.ds(k0, nk), :]) + bias_ref[first]
        s_ctx = _dot_nt(qx, kc_ref[0])
        m = jnp.maximum(jnp.max(s_lat, axis=-1, keepdims=True), jnp.max(s_ctx, axis=-1, keepdims=True))
        p_lat = jnp.exp2(s_lat - m)
        p_ctx = jnp.exp2(s_ctx - m)
        l = jnp.sum(p_lat, axis=-1, keepdims=True) + jnp.sum(p_ctx, axis=-1, keepdims=True)
        acc = _dot(_bf(p_lat), v_ref[pl.ds(k0, nk), :]) + _dot(_bf(p_ctx), vc_ref[0])
        o_ref[pl.ds(q0, GRID_W), :] = _bf(_collapse_heads(acc / l, head_mask))
        return carry

    lax.fori_loop(0, rows_per_step, body, 0, unroll=8)


def _na_latent(zr, kc, vc, bias_tab, batch, seqlen):
    n = batch * seqlen
    n_rows = seqlen // GRID_W
    assert n_rows >= NA_ROWS
    rows_per_step = 8
    steps = n_rows // rows_per_step
    tq = rows_per_step * GRID_W
    n_ctx = kc.shape[1]
    return pl.pallas_call(
        functools.partial(_na_lat_kernel, rows_per_step=rows_per_step, n_rows=n_rows),
        grid=(batch, steps),
        in_specs=[
            pl.BlockSpec((tq, BRANCH_W), lambda b, j: (b * steps + j, COL_NAQ)),
            pl.BlockSpec((seqlen, BRANCH_W), lambda b, j: (b, COL_NAK)),
            pl.BlockSpec((seqlen, BRANCH_W), lambda b, j: (b, COL_NAV)),
            pl.BlockSpec((1, n_ctx, BRANCH_W), lambda b, j: (b, 0, 0)),
            pl.BlockSpec((1, n_ctx, BRANCH_W), lambda b, j: (b, 0, 0)),
            _const_spec(bias_tab.shape),
        ],
        out_specs=pl.BlockSpec((tq, BRANCH_W), lambda b, j: (b * steps + j, 0)),
        out_shape=jax.ShapeDtypeStruct((n, BRANCH_W), BF16),
        compiler_params=_params("arbitrary", "arbitrary"),
        name="na_latent",
    )(zr, zr, zr, kc, vc, bias_tab)


def _rope_tables(seqlen):
    half = HEAD_DIM // 2
    t = np.arange(seqlen)
    rows = (t // GRID_W).astype(np.float32)
    cols = (t % GRID_W).astype(np.float32)
    inv = (1.0 / (np.float32(ROPE_THETA) ** (np.arange(0, half, 2, dtype=np.float32) / np.float32(half)))).astype(np.float32)
    ang_r = rows[:, None] * inv[None, :]
    ang_c = cols[:, None] * inv[None, :]
    cos = np.concatenate([np.cos(ang_r), np.cos(ang_r), np.cos(ang_c), np.cos(ang_c)], axis=-1)
    sin = np.concatenate([-np.sin(ang_r), np.sin(ang_r), -np.sin(ang_c), np.sin(ang_c)], axis=-1)
    cos = np.tile(cos.astype(np.float32), (1, N_HEADS))
    sin = np.tile(sin.astype(np.float32), (1, N_HEADS))
    return jnp.asarray(cos), jnp.asarray(sin)


def _rope(x, cos, sin_signed):
    w = x.shape[-1]
    lane = lax.broadcasted_iota(jnp.int32, x.shape, 1)
    partner = jnp.where((lane % 32) < 16, pltpu.roll(x, w - 16, 1), pltpu.roll(x, 16, 1))
    return x * cos + partner * sin_signed


def _swa_lat_kernel(sink_ref, q_ref, k_ref, v_ref, kc_ref, vc_ref, cos_ref, sin_ref, band_ref, o_ref, *, seqlen):
    for sub in range(SWA_BLOCKS_PER_STEP):
        rows = slice(sub * SWA_BLOCK, (sub + 1) * SWA_BLOCK)
        _swa_block(pl.program_id(1) * SWA_BLOCKS_PER_STEP + sub, sink_ref, q_ref.at[rows, :], k_ref, v_ref, kc_ref, vc_ref,
                   cos_ref, sin_ref, band_ref, o_ref.at[rows, :], seqlen)


def _swa_block(j, sink_ref, q_ref, k_ref, v_ref, kc_ref, vc_ref, cos_ref, sin_ref, band_ref, o_ref, seqlen):
    blk = SWA_BLOCK
    nwin = 3 * blk
    kvw = SWA_KV_HEADS * HEAD_DIM
    q0 = pl.multiple_of(j * blk, blk)
    k_blk = jnp.clip(j - 1, 0, seqlen // blk - 3)
    k0 = pl.multiple_of(k_blk * blk, blk)
    q = _rope(q_ref[...].astype(F32), cos_ref[pl.ds(q0, blk), :], sin_ref[pl.ds(q0, blk), :])
    kw = _rope(k_ref[pl.ds(k0, nwin), :].astype(F32), cos_ref[pl.ds(k0, nwin), 0:kvw], sin_ref[pl.ds(k0, nwin), 0:kvw])
    kw = _bf(kw)
    vw = v_ref[pl.ds(k0, nwin), :]
    n_slot = N_HEADS
    qb = _bf(q)
    lane_kv = lax.broadcasted_iota(jnp.int32, (blk, kvw), 1) // HEAD_DIM
    qx = jnp.concatenate(
        [jnp.where(lane_kv == (slot % SWA_KV_HEADS), qb[:, (slot // SWA_KV_HEADS) * kvw:(slot // SWA_KV_HEADS + 1) * kvw],
                   jnp.zeros((), BF16)) for slot in range(n_slot)], axis=0)
    row_slot = lax.broadcasted_iota(jnp.int32, (n_slot * blk, 1), 0) // blk
    sink = jnp.zeros((n_slot * blk, 1), F32)
    for slot in range(n_slot):
        sink = jnp.where(row_slot == slot, sink_ref[SWA_HEAD_ORDER[slot]] * LOG2E, sink)
    band = band_ref[j - k_blk]
    s_band = _dot_nt(qx, kw) + jnp.concatenate([band] * n_slot, axis=0)
    s_ctx = _dot_nt(qx, kc_ref[0])
    m = jnp.maximum(jnp.maximum(jnp.max(s_band, axis=-1, keepdims=True), jnp.max(s_ctx, axis=-1, keepdims=True)), sink)
    p_band = jnp.exp2(s_band - m)
    p_ctx = jnp.exp2(s_ctx - m)
    l = jnp.sum(p_band, axis=-1, keepdims=True) + jnp.sum(p_ctx, axis=-1, keepdims=True) + jnp.exp2(sink - m)
    acc = (_dot(_bf(p_band), vw) + _dot(_bf(p_ctx), vc_ref[0])) / l
    halves = []
    for half in range(n_slot // SWA_KV_HEADS):
        r0 = half * SWA_KV_HEADS * blk
        halves.append(jnp.where(lane_kv == 0, acc[r0:r0 + blk], acc[r0 + blk:r0 + 2 * blk]))
    o_ref[...] = _bf(jnp.concatenate(halves, axis=-1))


def _swa_latent(zr, kc, vc, sink, batch, seqlen):
    n = batch * seqlen
    nb = seqlen // SWA_BLOCK
    kvw = SWA_KV_HEADS * HEAD_DIM
    n_ctx = kc.shape[1]
    cos, sin = _rope_tables(seqlen)
    assert nb >= 3 and nb % SWA_BLOCKS_PER_STEP == 0
    steps = nb // SWA_BLOCKS_PER_STEP
    tq = SWA_BLOCKS_PER_STEP * SWA_BLOCK
    a = np.arange(SWA_BLOCK)[:, None]
    c = np.arange(3 * SWA_BLOCK)[None, :]
    band = jnp.asarray(np.stack([np.where(np.abs(c - a - off * SWA_BLOCK) <= SWA_WINDOW, 0.0, -np.inf) for off in range(3)])
                       .astype(np.float32))
    return pl.pallas_call(
        functools.partial(_swa_lat_kernel, seqlen=seqlen),
        grid=(batch, steps),
        in_specs=[
            pl.BlockSpec(memory_space=pltpu.SMEM),
            pl.BlockSpec((tq, BRANCH_W), lambda b, j: (b * steps + j, COL_SQ)),
            pl.BlockSpec((seqlen, kvw), lambda b, j: (b, COL_SK128)),
            pl.BlockSpec((seqlen, kvw), lambda b, j: (b, COL_SV128)),
            pl.BlockSpec((1, n_ctx, kvw), lambda b, j: (b, 0, 0)),
            pl.BlockSpec((1, n_ctx, kvw), lambda b, j: (b, 0, 0)),
            _const_spec(cos.shape),
            _const_spec(sin.shape),
            _const_spec(band.shape),
        ],
        out_specs=pl.BlockSpec((tq, BRANCH_W), lambda b, j: (b * steps + j, 0)),
        out_shape=jax.ShapeDtypeStruct((n, BRANCH_W), BF16),
        compiler_params=_params("arbitrary", "arbitrary"),
        name="swa_latent",
    )(sink.astype(F32), zr, zr, zr, kc, vc, cos, sin, band)


def _merge_kernel(a_ref, b_ref, c_ref, d_ref, g0_ref, g1_ref, g2_ref, g3_ref, x_ref, gate_ref, sh_ref, sc_ref,
                  ng_ref, wb_ref, wo_ref, wr_ref, br_ref, tri_ref, upper_ref, x1_ref, h_ref, route_ref, seg_ref):
    tm = x_ref.shape[0]
    mix = None
    for br, gt, i in ((a_ref, g0_ref, 0), (b_ref, g1_ref, 1), (c_ref, g2_ref, 2), (d_ref, g3_ref, 3)):
        t = (1.0 + jnp.tanh(gt[...].astype(F32))) * _dot(br[...], wb_ref[i])
        mix = t if mix is None else mix + t
    x1 = x_ref[...] + gate_ref[0] * _dot(_bf(mix), wo_ref[...])
    x1_ref[...] = x1
    ms = jnp.mean(x1 * x1, axis=-1, keepdims=True)
    h = x1 * lax.rsqrt(ms + EPS) * ng_ref[...]
    h = h * (1.0 + sc_ref[0]) + sh_ref[0]
    h_ref[...] = _bf(h)

    hh = _bf(h)
    hm = _bf(h - hh.astype(F32))
    logits = (_dot(hh, wr_ref[0]) + _dot(hm, wr_ref[0]) + _dot(hh, wr_ref[1])) + br_ref[...]
    lane_i = lax.broadcasted_iota(jnp.int32, logits.shape, 1)
    lane = lane_i.astype(F32)
    lane_grp = (lane_i // EXPERTS_PER_GROUP).astype(F32)
    neg = -jnp.inf
    far = float(4 * N_EXPERTS)
    is_g = (lane_i >= N_EXPERTS) & (lane_i < N_EXPERTS + N_GROUPS)
    gl = jnp.where(is_g, logits, neg)
    gmax = jnp.max(gl, axis=-1, keepdims=True)
    gsum = jnp.sum(jnp.exp(gl - gmax), axis=-1, keepdims=True)
    g_top_p = 1.0 / gsum
    g_idx = jnp.min(jnp.where(is_g & (gl == gmax), lane, far), axis=-1, keepdims=True) - float(N_EXPERTS)
    in_grp = (lane_i < N_EXPERTS) & (lane_grp == g_idx)
    e_l = jnp.where(in_grp, logits, neg)
    e1 = jnp.max(e_l, axis=-1, keepdims=True)
    i1 = jnp.min(jnp.where(in_grp & (e_l == e1), lane, far), axis=-1, keepdims=True)
    e_l2 = jnp.where(lane == i1, neg, e_l)
    e2 = jnp.max(e_l2, axis=-1, keepdims=True)
    i2 = jnp.min(jnp.where(in_grp & (lane != i1) & (e_l2 == e2), lane, far), axis=-1, keepdims=True)
    t2 = jnp.exp(e2 - e1)
    w1 = g_top_p / (1.0 + t2)
    w2 = w1 * t2

    sel = (lane == i1) | (lane == i2)
    sel_f = jnp.where(sel, 1.0, 0.0)
    cum = _dot(tri_ref[...], _bf(sel_f))
    counts = cum[tm - 1:tm, :]
    padded = jnp.floor((counts + (MOE_ROW_ALIGN - 1)) * (1.0 / MOE_ROW_ALIGN)) * MOE_ROW_ALIGN
    seg_start = _dot(_bf(jnp.broadcast_to(padded, (SUBLANES, 128))), upper_ref[...])[0:1, :]
    slot = seg_start + cum - sel_f
    pos1 = jnp.sum(jnp.where(lane == i1, slot, 0.0), axis=-1, keepdims=True)
    pos2 = jnp.sum(jnp.where(lane == i2, slot, 0.0), axis=-1, keepdims=True)
    route_ref[...] = jnp.where(lane_i == 0, pos1, jnp.where(lane_i == 1, pos2, jnp.where(lane_i == 2, w1,
                               jnp.where(lane_i == 3, w2, 0.0))))
    seg = jnp.where(lane_i[0:1] < N_EXPERTS, seg_start, pltpu.roll(jnp.broadcast_to(padded, (SUBLANES, 128)), N_EXPERTS, 1)[0:1])
    seg_ref[0] = jnp.where(lane_i[0:1] < 2 * N_EXPERTS, seg, 0.0).astype(jnp.int32)


def _router_tables(w_rg, b_rg, w_re, b_re):
    w = jnp.zeros((D_MODEL, 128), F32)
    w = w.at[:, :N_EXPERTS].set(w_re.astype(F32)).at[:, N_EXPERTS:N_EXPERTS + N_GROUPS].set(w_rg.astype(F32))
    b = jnp.zeros((1, 128), F32)
    b = b.at[0, :N_EXPERTS].set(b_re.astype(F32)).at[0, N_EXPERTS:N_EXPERTS + N_GROUPS].set(b_rg.astype(F32))
    return jnp.stack(_split3(w)[:2], axis=0), b


def _merge(branches, zr, x, mods3, mod_row, norm_g, w_branch_bf, w_out_bf, wr3, br, layer):
    n = x.shape[0]
    per_layer = lambda shape: pl.BlockSpec((None,) + shape, lambda i: (layer,) + (0,) * len(shape))
    tm = MOE_TILE
    t = np.arange(tm)
    tri = jnp.asarray((t[None, :] <= t[:, None]).astype(np.float32), BF16)
    e = np.arange(128)
    upper = jnp.asarray((e[:, None] < e[None, :]).astype(np.float32), BF16)
    row = lambda w: pl.BlockSpec((tm, w), lambda i: (i, 0))
    gate = lambda k: pl.BlockSpec((tm, D_MODEL), lambda i: (i, COL_GATES1024 + k))
    mod = lambda k: pl.BlockSpec((1, 1, D_MODEL), lambda i: (mod_row(i, tm), 0, k))
    return pl.pallas_call(
        _merge_kernel,
        grid=(n // tm,),
        in_specs=[
            row(BRANCH_W), row(BRANCH_W), row(BRANCH_W), row(BRANCH_W),
            gate(0), gate(1), gate(2), gate(3),
            row(D_MODEL),
            mod(2), mod(3), mod(4),
            _const_spec((1, D_MODEL)),
            per_layer((N_BRANCH, BRANCH_W, D_MODEL)),
            per_layer((D_MODEL, D_MODEL)),
            per_layer((2, D_MODEL, 128)),
            per_layer((1, 128)),
            _const_spec((tm, tm)),
            _const_spec((128, 128)),
        ],
        out_specs=[row(D_MODEL), row(D_MODEL), row(128), pl.BlockSpec((1, 1, 128), lambda i: (i, 0, 0))],
        out_shape=[
            jax.ShapeDtypeStruct((n, D_MODEL), F32),
            jax.ShapeDtypeStruct((n, D_MODEL), BF16),
            jax.ShapeDtypeStruct((n, 128), F32),
            jax.ShapeDtypeStruct((n // tm, 1, 128), jnp.int32),
        ],
        compiler_params=_params("arbitrary"),
        name="merge",
    )(*branches, zr, zr, zr, zr, x, mods3, mods3, mods3, norm_g.reshape(1, D_MODEL), w_branch_bf, w_out_bf, wr3, br,
      tri, upper)


def _moe_kernel(seg_ref, h_ref, route_ref, x1_ref, gate_ref, fg_ref, wg_ref, wu_ref, wd_ref, *outs_and_scratch, final):
    if final:
        x2_ref, y_ref, hs_scr, ys_scr = outs_and_scratch
    else:
        x2_ref, hs_scr, ys_scr = outs_and_scratch
    tm = MOE_TILE
    tile = pl.program_id(0)
    route = route_ref[...]
    route_t = route.T
    pos1_row, pos2_row = route_t[0:1, :], route_t[1:2, :]
    h = h_ref[...]
    for rb in range(MOE_ROWS // MOE_GATHER_BLK):
        r = (rb * MOE_GATHER_BLK + lax.broadcasted_iota(jnp.int32, (MOE_GATHER_BLK, tm), 0)).astype(F32)
        p = jnp.where((r == pos1_row) | (r == pos2_row), 1.0, 0.0)
        hs_scr[rb * MOE_GATHER_BLK:(rb + 1) * MOE_GATHER_BLK, :] = _bf(_dot(_bf(p), h))
    ys_scr[...] = jnp.zeros_like(ys_scr)

    row_in_chunk = lax.broadcasted_iota(jnp.int32, (MOE_CHUNK, D_MODEL), 0)

    def expert_chunk(e, r0, end):
        r0 = pl.multiple_of(r0, MOE_ROW_ALIGN)
        rows = hs_scr[pl.ds(r0, MOE_CHUNK), :]
        a = _silu(_dot(rows, wg_ref[e])) * _dot(rows, wu_ref[e])
        y = _dot(_bf(a), wd_ref[e])
        ys_scr[pl.ds(r0, MOE_CHUNK), :] = jnp.where(row_in_chunk < end - r0, _bf(y), ys_scr[pl.ds(r0, MOE_CHUNK), :])

    starts = [seg_ref[tile * 128 + e] for e in range(N_EXPERTS)]
    ends = [starts[e] + seg_ref[tile * 128 + N_EXPERTS + e] for e in range(N_EXPERTS)]
    for e in range(N_EXPERTS):
        expert_chunk(e, starts[e], ends[e])
    for e in range(N_EXPERTS):
        n_chunks = lax.div(ends[e] - starts[e] + (MOE_CHUNK - 1), MOE_CHUNK)

        def more(c, carry, e=e):
            expert_chunk(e, starts[e] + c * MOE_CHUNK, ends[e])
            return carry

        lax.fori_loop(1, n_chunks, more, 0)

    pos1, pos2, w1, w2 = route[:, 0:1], route[:, 1:2], route[:, 2:3], route[:, 3:4]
    acc = None
    for cb in range(MOE_SEG_ROWS // MOE_SCATTER_BLK):
        r = (cb * MOE_SCATTER_BLK + lax.broadcasted_iota(jnp.int32, (tm, MOE_SCATTER_BLK), 1)).astype(F32)
        q = jnp.where(r == pos1, w1, jnp.where(r == pos2, w2, 0.0))
        part = _dot(_bf(q), ys_scr[cb * MOE_SCATTER_BLK:(cb + 1) * MOE_SCATTER_BLK, :])
        acc = part if acc is None else acc + part
    x2 = x1_ref[...] + gate_ref[0] * acc
    x2_ref[...] = x2
    if final:
        ms = jnp.mean(x2 * x2, axis=-1, keepdims=True)
        y_ref[...] = x2 * lax.rsqrt(ms + EPS) * fg_ref[...]


def _moe(h, route, seg, x1, mods3, mod_row, final_g, wg_bf, wu_bf, wd_bf, layer, final):
    n = x1.shape[0]
    tm = MOE_TILE
    row = lambda w: pl.BlockSpec((tm, w), lambda i, s: (i, 0))
    resident = lambda shape: pl.BlockSpec((None,) + shape, lambda i, s: (layer,) + (0,) * len(shape),
                                          pipeline_mode=pl.Buffered(1))
    out_specs = [row(D_MODEL)]
    out_shape = [jax.ShapeDtypeStruct((n, D_MODEL), F32)]
    if final:
        out_specs.append(row(D_MODEL))
        out_shape.append(jax.ShapeDtypeStruct((n, D_MODEL), F32))
    return pl.pallas_call(
        functools.partial(_moe_kernel, final=final),
        grid_spec=pltpu.PrefetchScalarGridSpec(
            num_scalar_prefetch=1,
            grid=(n // tm,),
            in_specs=[
                row(D_MODEL), row(128), row(D_MODEL),
                pl.BlockSpec((1, 1, D_MODEL), lambda i, s: (mod_row(i, tm), 0, 5)),
                pl.BlockSpec((1, D_MODEL), lambda i, s: (0, 0)),
                resident((N_EXPERTS, D_MODEL, EXPERT_FF)),
                resident((N_EXPERTS, D_MODEL, EXPERT_FF)),
                resident((N_EXPERTS, EXPERT_FF, D_MODEL)),
            ],
            out_specs=out_specs,
            scratch_shapes=[pltpu.VMEM((MOE_ROWS, D_MODEL), BF16), pltpu.VMEM((MOE_ROWS, D_MODEL), BF16)],
        ),
        out_shape=out_shape,
        compiler_params=_params("arbitrary"),
        name="moe_final" if final else "moe",
    )(seg.reshape(-1), h, route, x1, mods3, final_g.reshape(1, D_MODEL), wg_bf, wu_bf, wd_bf)


def _state_to_blockdiag_t(s):
    b = s.shape[0]
    st = jnp.swapaxes(s.astype(F32), -1, -2)
    eye = jnp.eye(N_HEADS, dtype=F32)
    full = st[:, :, :, :, None, :] * eye[None, None, :, None, :, None]
    return full.reshape(b, 2, BRANCH_W, BRANCH_W)


def _compact_to_state(fin):
    b = fin.shape[0]
    return jnp.transpose(fin.reshape(b, 2, HEAD_DIM, N_HEADS, HEAD_DIM), (0, 1, 3, 4, 2))


def _layer(x, l, w, mods3, mod_row, batch, seqlen, latent, st0, caches, final):
    proj = _in_proj(x, mods3, mod_row, w["norm1_g"][l], w["w_in"], w["smlp_ws"][l], w["smlp_b"][l], l, IN_TILE, not latent)
    zf, zr, c_out = proj[0], proj[1], proj[2]
    kv = None if latent else proj[3:]
    a_out, fin = _hgrn_mixer(zf, zr, w["lb"][l], w["onorm_g"][l], st0, batch, seqlen,
                             n_sub=min(HGRN_CHUNKS_PER_STEP, seqlen // HGRN_CHUNK))
    if latent:
        ck_na, cv_na, ck_swa, cv_swa = caches
        b_out = _na_latent(zr, ck_na, cv_na, w["na_bias"][l], batch, seqlen)
        d_out = _swa_latent(zr, ck_swa, cv_swa, w["swa_sink"][l], batch, seqlen)
    else:
        b_out = _ctx_attn(zr, batch, seqlen, COL_NAQ, COL_NAK, COL_NAV, BRANCH_W, N_HEADS, None)
        d_out = _ctx_attn(zr, batch, seqlen, COL_SQ, COL_SK128, COL_SV128, SWA_KV_HEADS * HEAD_DIM, SWA_KV_HEADS,
                          w["swa_sink"][l])
    x1, h2, route, seg = _merge((a_out, b_out, c_out, d_out), zr, x, mods3, mod_row, w["norm2_g"][l], w["w_branch"],
                                w["w_out"], w["wr"], w["br"], l)
    out = _moe(h2, route, seg, x1, mods3, mod_row, w["final_g"], w["wg"], w["wu"], w["wd"], l, final)
    return out, kv, fin


def kernel(x_prompt, x_sample, c, cache_na_k, cache_na_v, cache_swa_k, cache_swa_v, state_hgrn, c_ctx, w_ada, b_ada, norm1_g, norm2_g, w_in, hgrn_lb, hgrn_onorm_g, na_rpb, smlp_ws, smlp_b, swa_sink, w_branch, w_out, router_g_w, router_g_b, router_e_w, router_e_b, moe_w_gate, moe_w_up, moe_w_down, final_g):
    bc, lc, _ = x_prompt.shape
    bl, ll, _ = x_sample.shape
    n_ctx_tok = bc * lc

    cond = jnp.zeros((MOD_ROWS, D_MODEL), F32).at[0].set(c_ctx.astype(F32)).at[1:1 + bl].set(c.astype(F32))
    mods = _ada_mods(cond, w_ada, b_ada)

    lb_soft = jax.nn.softmax(hgrn_lb.astype(F32), axis=0)
    lb_all = jnp.cumsum(lb_soft, axis=0) - lb_soft[0:1]

    col_scale = np.ones((P_IN,), np.float32)
    col_scale[ZF_W + COL_GATES1024 * 1024:] = 0.5
    for q_col in (COL_NAQ, COL_SQ):
        col_scale[ZF_W + q_col * BRANCH_W:ZF_W + (q_col + 1) * BRANCH_W] = QK_PRESCALE
    swa_rows = jnp.concatenate([w_branch[:, N_BRANCH - 1, h * HEAD_DIM:(h + 1) * HEAD_DIM] for h in SWA_HEAD_ORDER], axis=1)
    routers = [_router_tables(router_g_w[l], router_g_b[l], router_e_w[l], router_e_b[l]) for l in range(DEPTH)]
    w = dict(
        norm1_g=norm1_g, norm2_g=norm2_g, lb=lb_all, onorm_g=hgrn_onorm_g, smlp_ws=smlp_ws, smlp_b=smlp_b,
        swa_sink=swa_sink, final_g=final_g,
        w_in=_bf(w_in * jnp.asarray(col_scale)[None, None, :]),
        w_branch=_bf(w_branch.at[:, N_BRANCH - 1].set(swa_rows)),
        w_out=_bf(0.5 * w_out),
        wr=jnp.stack([r[0] for r in routers]), br=jnp.stack([r[1] for r in routers]),
        wg=_bf(moe_w_gate), wu=_bf(moe_w_up), wd=_bf(moe_w_down),
        na_bias=_na_bias_tables(na_rpb),
    )

    ctx_row = lambda i, tm: 0
    lat_row = lambda i, tm: 1 + i // (ll // tm)
    xp = x_prompt.reshape(n_ctx_tok, D_MODEL)
    kvs, states = [], []
    y_prompt = None
    for l in range(DEPTH):
        final = l == DEPTH - 1
        out, kv, fin = _layer(xp, l, w, mods[l].reshape(MOD_ROWS, 1, -1), ctx_row, bc, lc, False, None, None, final)
        if final:
            xp, y_prompt = out
        else:
            xp = out[0]
        kvs.append(kv)
        states.append(_compact_to_state(fin))
    cache_out = [jnp.stack([kvs[l][k].reshape(bc, lc, -1, HEAD_DIM) for l in range(DEPTH)], axis=1) for k in range(4)]

    xs = x_sample.reshape(bl * ll, D_MODEL)
    y_sample = None
    n_past = cache_na_k.shape[2]
    for l in range(DEPTH):
        caches = (_bf(cache_na_k[:, l]).reshape(bl, n_past, BRANCH_W), _bf(cache_na_v[:, l]).reshape(bl, n_past, BRANCH_W),
                  _bf(cache_swa_k[:, l]).reshape(bl, n_past, SWA_KV_HEADS * HEAD_DIM),
                  _bf(cache_swa_v[:, l]).reshape(bl, n_past, SWA_KV_HEADS * HEAD_DIM))
        final = l == DEPTH - 1
        out, _, _ = _layer(xs, l, w, mods[l].reshape(MOD_ROWS, 1, -1), lat_row, bl, ll, True,
                           _state_to_blockdiag_t(state_hgrn[:, l]), caches, final)
        if final:
            xs, y_sample = out
        else:
            xs = out[0]

    return (y_prompt.reshape(bc, lc, D_MODEL), y_sample.reshape(bl, ll, D_MODEL), *cache_out, jnp.stack(states, axis=1))
```

```python
import functools

import numpy as np
import jax
import jax.numpy as jnp
from jax import lax
from jax.experimental import pallas as pl
from jax.experimental.pallas import tpu as pltpu

D_MODEL = 1024
DEPTH = 2
GRID_W = 64
HEAD_DIM = 64
N_BRANCH = 4
BRANCH_W = 256
N_HEADS = 4
HGRN_CHUNK = 64
HGRN_CHUNKS_PER_STEP = 8
NA_ROWS = 8
NA_COLS = 16
SMLP_GROUPS = 4
SMLP_CHUNK = 128
SWA_KV_HEADS = 2
SWA_HEAD_ORDER = (0, 2, 1, 3)
SWA_WINDOW = 128
SWA_BLOCK = 128
SWA_BLOCKS_PER_STEP = 4
ROPE_THETA = 10000.0
N_GROUPS = 4
EXPERTS_PER_GROUP = 4
N_EXPERTS = 16
EXPERT_FF = 256
ADA_CHUNKS = 6
EPS = 1e-6
TINY = 1e-30
P_IN = 7168
ATT_SCALE = HEAD_DIM ** -0.5
LOG2E = 1.4426950408889634
QK_PRESCALE = ATT_SCALE * LOG2E

ZF_W = 1024
ZR_W = P_IN - ZF_W
COL_HQ, COL_HI, COL_HFF, COL_HFB = 0, 1, 2, 3
COL_HOG, COL_NAQ, COL_NAK, COL_NAV = 0, 1, 2, 3
COL_MU, COL_MV, COL_SQ = 4, 5, 6
COL_SK128, COL_SV128 = 14, 15
COL_GATES1024 = 2
IN_COL_CHUNK = 1024
IN_TILE = 512

MOD_ROWS = 16
VMEM_LIMIT = 56 * 1024 * 1024

F32 = jnp.float32
BF16 = jnp.bfloat16

MOE_TILE = 512
MOE_ROW_ALIGN = 16
MOE_CHUNK = 96
MOE_GATHER_BLK = 128
MOE_SCATTER_BLK = 256
MOE_SEG_ROWS = -(-(2 * MOE_TILE + N_EXPERTS * (MOE_ROW_ALIGN - 1)) // MOE_SCATTER_BLK) * MOE_SCATTER_BLK
MOE_ROWS = -(-(2 * MOE_TILE + N_EXPERTS * (MOE_ROW_ALIGN - 1) + MOE_CHUNK) // MOE_GATHER_BLK) * MOE_GATHER_BLK
HGRN_LEVELS = (0, 1, 2, 4, 8, 16, 32)
SUBLANES = 8
LANES = 128
HGRN_MXU_REF_LEVELS = ()


def _bf(x):
    return x.astype(BF16)


def _dot(a, b):
    return jnp.dot(a, b, preferred_element_type=F32)


def _dot_nt(a, b):
    return lax.dot_general(a, b, (((1,), (1,)), ((), ())), preferred_element_type=F32)


def _dot_tn(a, b):
    return lax.dot_general(a, b, (((0,), (0,)), ((), ())), preferred_element_type=F32)


def _split3(x):
    hi = _bf(x)
    r1 = x - hi.astype(F32)
    mid = _bf(r1)
    lo = _bf(r1 - mid.astype(F32))
    return hi, mid, lo


def _dot01_left(m01, x):
    hi, mid, lo = _split3(x)
    return _dot(m01, hi) + _dot(m01, mid) + _dot(m01, lo)


def _dot01_right(x, m01):
    hi, mid, lo = _split3(x)
    return _dot(hi, m01) + _dot(mid, m01) + _dot(lo, m01)


def _sigmoid(x):
    return 0.5 * jnp.tanh(0.5 * x) + 0.5


def _silu(x):
    return x * _sigmoid(x)


def _params(*sem):
    return pltpu.CompilerParams(dimension_semantics=sem, vmem_limit_bytes=VMEM_LIMIT)


def _const_spec(shape):
    n = len(shape)
    return pl.BlockSpec(shape, lambda *_: (0,) * n)


def _head_mask():
    row = lax.broadcasted_iota(jnp.int32, (N_HEADS * HEAD_DIM, BRANCH_W), 0)
    lane = lax.broadcasted_iota(jnp.int32, (N_HEADS * HEAD_DIM, BRANCH_W), 1)
    return (row // HEAD_DIM) == (lane // HEAD_DIM)


def _expand_heads(x, head_mask):
    return jnp.where(head_mask, jnp.concatenate([x] * N_HEADS, axis=0), jnp.zeros((), x.dtype))


def _collapse_heads(r, head_mask):
    r = jnp.where(head_mask, r, 0.0)
    n = HEAD_DIM
    return (r[0:n] + r[n:2 * n]) + (r[2 * n:3 * n] + r[3 * n:4 * n])


def _ada_kernel(cond_ref, w_ref, b_ref, o_ref):
    s = _silu(cond_ref[...])
    o_ref[0] = _dot(_bf(s), _bf(w_ref[0])) + b_ref[0]


def _ada_mods(cond, w_ada, b_ada):
    tn = 1536
    n = ADA_CHUNKS * D_MODEL
    return pl.pallas_call(
        _ada_kernel,
        grid=(DEPTH, n // tn),
        in_specs=[
            pl.BlockSpec((MOD_ROWS, D_MODEL), lambda l, j: (0, 0)),
            pl.BlockSpec((1, D_MODEL, tn), lambda l, j: (l, 0, j)),
            pl.BlockSpec((1, 1, tn), lambda l, j: (l, 0, j)),
        ],
        out_specs=pl.BlockSpec((1, MOD_ROWS, tn), lambda l, j: (l, 0, j)),
        out_shape=jax.ShapeDtypeStruct((DEPTH, MOD_ROWS, n), F32),
        compiler_params=_params("arbitrary", "arbitrary"),
        name="ada_mods",
    )(cond, w_ada, b_ada.reshape(DEPTH, 1, n))


def _smlp_tile(u, v, ws_ref, bias_ref, ones_ref):
    ms = _dot01_right(v * v, ones_ref[...]) * (1.0 / HEAD_DIM)
    vn = _bf(v * lax.rsqrt(ms + EPS))
    lane_g = lax.broadcasted_iota(jnp.int32, (SMLP_CHUNK, BRANCH_W), 1) // HEAD_DIM
    outs = []
    for ci in range(u.shape[0] // SMLP_CHUNK):
        rows = slice(ci * SMLP_CHUNK, (ci + 1) * SMLP_CHUNK)
        mixed = bias_ref[...]
        for g in range(SMLP_GROUPS):
            mixed = mixed + jnp.where(lane_g == g, _dot(ws_ref[g], vn[rows]), 0.0)
        outs.append(u[rows] * mixed)
    return jnp.concatenate(outs, axis=0)


def _in_kernel(x_ref, sh_ref, sc_ref, g_ref, w_ref, ws_ref, bias_ref, ones_ref, zf_ref, zr_ref, c_ref, *rest, want_kv):
    x = x_ref[...]
    ms = jnp.mean(x * x, axis=-1, keepdims=True)
    h = x * lax.rsqrt(ms + EPS) * g_ref[...]
    h = _bf(h * (1.0 + sc_ref[0]) + sh_ref[0])
    tn = IN_COL_CHUNK
    kvw = SWA_KV_HEADS * HEAD_DIM
    for j in range(P_IN // tn):
        acc = _dot(h, w_ref[:, j * tn:(j + 1) * tn])
        if j == 0:
            zf_ref[...] = acc
            continue
        c0 = (j - 1) * tn
        sq0 = COL_SQ * BRANCH_W - c0
        if 0 <= sq0 < tn:
            heads = [acc[:, sq0 + hd * HEAD_DIM:sq0 + (hd + 1) * HEAD_DIM] for hd in SWA_HEAD_ORDER]
            zr_ref[:, c0:c0 + tn] = _bf(jnp.concatenate([acc[:, :sq0]] + heads + [acc[:, sq0 + BRANCH_W:]], axis=1))
        else:
            zr_ref[:, c0:c0 + tn] = _bf(acc)
        mu0 = COL_MU * BRANCH_W - c0
        if 0 <= mu0 and mu0 + 2 * BRANCH_W <= tn:
            assert COL_MV == COL_MU + 1
            c_ref[...] = _bf(_smlp_tile(acc[:, mu0:mu0 + BRANCH_W], acc[:, mu0 + BRANCH_W:mu0 + 2 * BRANCH_W],
                                        ws_ref, bias_ref, ones_ref))
        if want_kv:
            nak_ref, nav_ref, sk_ref, sv_ref = rest
            for ref, col, width in ((nak_ref, COL_NAK * BRANCH_W, BRANCH_W), (nav_ref, COL_NAV * BRANCH_W, BRANCH_W),
                                    (sk_ref, COL_SK128 * kvw, kvw), (sv_ref, COL_SV128 * kvw, kvw)):
                if c0 <= col < c0 + tn:
                    ref[...] = acc[:, col - c0:col - c0 + width]


def _in_proj(x, mods3, mod_row, norm_g, w_in_bf, smlp_ws, smlp_b, layer, tm, want_kv):
    n = x.shape[0]
    assert ZF_W == IN_COL_CHUNK and tm % SMLP_CHUNK == 0
    kvw = SWA_KV_HEADS * HEAD_DIM
    bias = jnp.repeat(smlp_b.astype(F32).T, BRANCH_W // SMLP_GROUPS, axis=1)
    row = lambda w: pl.BlockSpec((tm, w), lambda i: (i, 0))
    out_specs = [row(ZF_W), row(ZR_W), row(BRANCH_W)]
    out_shape = [jax.ShapeDtypeStruct((n, ZF_W), F32), jax.ShapeDtypeStruct((n, ZR_W), BF16),
                 jax.ShapeDtypeStruct((n, BRANCH_W), BF16)]
    if want_kv:
        for width in (BRANCH_W, BRANCH_W, kvw, kvw):
            out_specs.append(row(width))
            out_shape.append(jax.ShapeDtypeStruct((n, width), F32))
    return pl.pallas_call(
        functools.partial(_in_kernel, want_kv=want_kv),
        grid=(n // tm,),
        in_specs=[
            row(D_MODEL),
            pl.BlockSpec((1, 1, D_MODEL), lambda i: (mod_row(i, tm), 0, 0)),
            pl.BlockSpec((1, 1, D_MODEL), lambda i: (mod_row(i, tm), 0, 1)),
            _const_spec((1, D_MODEL)),
            pl.BlockSpec((None, D_MODEL, P_IN), lambda i: (layer, 0, 0), pipeline_mode=pl.Buffered(1)),
            _const_spec((SMLP_GROUPS, SMLP_CHUNK, SMLP_CHUNK)),
            _const_spec((SMLP_CHUNK, BRANCH_W)),
            _const_spec((BRANCH_W, BRANCH_W)),
        ],
        out_specs=out_specs,
        out_shape=out_shape,
        compiler_params=_params("arbitrary"),
        name="in_proj_kv" if want_kv else "in_proj",
    )(x, mods3, mods3, norm_g.reshape(1, D_MODEL), w_in_bf, _bf(smlp_ws), bias, _group_ones())


def _hgrn_tables():
    c = HGRN_CHUNK
    t = np.arange(c)
    tsel = np.zeros((2, (1 + len(HGRN_MXU_REF_LEVELS)) * c, c), np.float32)
    pmask = np.zeros((2, len(HGRN_LEVELS), c, N_HEADS * c), np.float32)
    for rev in (0, 1):
        cum = (t[None, :] >= t[:, None]) if rev else (t[None, :] <= t[:, None])
        tsel[rev, :c] = cum
        pmask[rev, 0] = np.tile(np.eye(c, dtype=np.float32), (1, N_HEADS))
        for li, m in enumerate(HGRN_LEVELS[1:], start=1):
            if m in HGRN_MXU_REF_LEVELS:
                slot = 1 + HGRN_MXU_REF_LEVELS.index(m)
                tsel[rev, slot * c:(slot + 1) * c] = cum[(t // (2 * m)) * (2 * m) + (m - 1 if rev else m)]
            same = (t[:, None] // (2 * m)) == (t[None, :] // (2 * m))
            q_half = ((t & m) == 0) if rev else ((t & m) != 0)
            k_half = ~q_half
            pmask[rev, li] = np.tile((same & q_half[:, None] & k_half[None, :]).astype(np.float32), (1, N_HEADS))
    return jnp.asarray(tsel, BF16), jnp.asarray(pmask, F32)


def _ref_rows(b_ref, m, rev):
    c = HGRN_CHUNK
    off = (m - 1) if rev else m
    row = lambda r, n: jnp.broadcast_to(b_ref[pl.ds(r, 1), :], (n, BRANCH_W))
    if 2 * m >= SUBLANES:
        return jnp.concatenate([row(s + off, 2 * m) for s in range(0, c, 2 * m)], axis=0)
    sub = lax.broadcasted_iota(jnp.int32, (SUBLANES, BRANCH_W), 0)
    tiles = []
    for t0 in range(0, c, SUBLANES):
        cur = row(t0 + off, SUBLANES)
        for s in range(2 * m, SUBLANES, 2 * m):
            cur = jnp.where(sub >= s, row(t0 + s + off, SUBLANES), cur)
        tiles.append(cur)
    return jnp.concatenate(tiles, axis=0)


def _hgrn_chunk(q_raw, v, f_raw, lb, st, tsel, pmask_ref, rev, head_mask, b_ref):
    c = HGRN_CHUNK
    qq = _silu(q_raw)
    f = lb + (1.0 - lb) * _sigmoid(f_raw)
    lf = jnp.log2(jnp.maximum(f, TINY))
    k = 1.0 - f
    ball = _dot01_left(tsel, lf)
    b = ball[:c]
    b_ref[...] = b
    bl = b[0:1] if rev else b[c - 1:c]
    vb = _bf(v)
    v_x = _expand_heads(vb, head_mask)

    o = _dot_nt(_bf(qq * jnp.exp2(b)), _bf(st))

    p = None
    qb, kb = _bf(qq), _bf(k)
    for li, m in enumerate(HGRN_LEVELS):
        if m == 0:
            qe, ke = qb, kb
        else:
            if m in HGRN_MXU_REF_LEVELS:
                slot = 1 + HGRN_MXU_REF_LEVELS.index(m)
                ref = ball[slot * c:(slot + 1) * c]
            else:
                ref = _ref_rows(b_ref, m, rev)
            e = _bf(jnp.exp2(-jnp.abs(b - ref)))
            qe, ke = qb * e, kb * e
        s = _dot_nt(qe, _expand_heads(ke, head_mask)) * pmask_ref[li]
        p = s if p is None else p + s
    o = o + _dot(_bf(p), v_x)

    ke_state = _bf(k * jnp.exp2(bl - b))
    st_new = st * jnp.exp2(bl) + jnp.where(head_mask, _dot_tn(vb, ke_state), 0.0)
    return o, st_new


def _hgrn_finish(o, og, g_ref, ones_ref):
    ms = _dot01_right(o * o, ones_ref[...]) * (1.0 / HEAD_DIM)
    return _bf(o * lax.rsqrt(ms + EPS) * g_ref[...] * _silu(og.astype(F32)))


def _hgrn_kernel(*refs, n_sub, n_blk, has_init):
    qf_ref, vf_ref, ff_ref, qb_ref, vb_ref, fb_ref, ogf_ref, ogb_ref, lb_ref, g_ref, ones_ref = refs[:11]
    n_in = 14 if has_init else 13
    st0_ref = refs[11] if has_init else None
    tsel_ref, pmask_ref = refs[n_in - 2], refs[n_in - 1]
    if n_blk == 1:
        af_ref, fin_ref, st_ref, b_scr, blk_scr = refs[n_in:]
    else:
        af_ref, ab_ref, fin_ref, st_ref, b_scr, blk_scr, keep_scr = refs[n_in:]
    c = HGRN_CHUNK
    step = pl.program_id(1)

    @pl.when(step == 0)
    def _():
        st_ref[...] = st0_ref[0] if has_init else jnp.zeros_like(st_ref)

    head_mask = _head_mask()
    st_f = st_ref[0]
    st_b = st_ref[1]
    for j in range(n_sub):
        rf = slice(j * c, (j + 1) * c)
        rb = slice((n_sub - 1 - j) * c, (n_sub - j) * c)
        o_f, st_f = _hgrn_chunk(qf_ref[rf, :], vf_ref[rf, :], ff_ref[rf, :], lb_ref[0:1, :], st_f,
                                tsel_ref[0], pmask_ref.at[0], False, head_mask, b_scr.at[2 * j])
        o_b, st_b = _hgrn_chunk(qb_ref[rb, :], vb_ref[rb, :], fb_ref[rb, :], lb_ref[1:2, :], st_b,
                                tsel_ref[1], pmask_ref.at[1], True, head_mask, b_scr.at[2 * j + 1])
        blk_scr[0, rf, :] = o_f
        blk_scr[1, rb, :] = o_b
    st_ref[0] = st_f
    st_ref[1] = st_b

    if n_blk == 1:
        af_ref[...] = _hgrn_finish(blk_scr[0] + blk_scr[1], ogf_ref[...], g_ref, ones_ref)
    else:
        half = n_blk // 2

        @pl.when(step < half)
        def _():
            keep_scr[0, step] = blk_scr[0]
            keep_scr[1, step] = blk_scr[1]

        @pl.when(step >= half)
        def _():
            other = n_blk - 1 - step
            af_ref[...] = _hgrn_finish(blk_scr[0] + keep_scr[1, other], ogf_ref[...], g_ref, ones_ref)
            ab_ref[...] = _hgrn_finish(blk_scr[1] + keep_scr[0, other], ogb_ref[...], g_ref, ones_ref)

    @pl.when(step == n_blk - 1)
    def _():
        fin_ref[0, 0] = _collapse_heads(st_f, head_mask)
        fin_ref[0, 1] = _collapse_heads(st_b, head_mask)


def _hgrn_mixer(zf, zr, lb2, onorm_g, st0, batch, seqlen, n_sub):
    n = batch * seqlen
    tb = n_sub * HGRN_CHUNK
    n_blk = seqlen // tb
    assert n_blk == 1 or n_blk % 2 == 0
    half = n_blk // 2
    tsel, pmask = _hgrn_tables()
    has_init = st0 is not None
    fwd = lambda col: pl.BlockSpec((tb, BRANCH_W), lambda b, c: (b * n_blk + c, col))
    bwd = lambda col: pl.BlockSpec((tb, BRANCH_W), lambda b, c: (b * n_blk + n_blk - 1 - c, col))
    in_specs = [fwd(COL_HQ), fwd(COL_HI), fwd(COL_HFF), bwd(COL_HQ), bwd(COL_HI), bwd(COL_HFB), fwd(COL_HOG), bwd(COL_HOG),
                _const_spec((2, BRANCH_W)), _const_spec((1, BRANCH_W)), _const_spec((BRANCH_W, BRANCH_W))]
    args = [zf, zf, zf, zf, zf, zf, zr, zr, lb2, onorm_g.reshape(1, BRANCH_W), _group_ones()]
    if has_init:
        in_specs.append(pl.BlockSpec((1, 2, BRANCH_W, BRANCH_W), lambda b, c: (b, 0, 0, 0)))
        args.append(st0)
    in_specs += [_const_spec(tsel.shape), _const_spec(pmask.shape)]
    args += [tsel, pmask]
    out_specs = [pl.BlockSpec((tb, BRANCH_W), lambda b, c: (b * n_blk + jnp.maximum(c, half), 0))]
    out_shape = [jax.ShapeDtypeStruct((n, BRANCH_W), BF16)]
    scratch = [pltpu.VMEM((2, BRANCH_W, BRANCH_W), F32), pltpu.VMEM((2 * n_sub, HGRN_CHUNK, BRANCH_W), F32),
               pltpu.VMEM((2, tb, BRANCH_W), F32)]
    if n_blk > 1:
        out_specs.append(pl.BlockSpec((tb, BRANCH_W), lambda b, c: (b * n_blk + jnp.minimum(n_blk - 1 - c, half - 1), 0)))
        out_shape.append(jax.ShapeDtypeStruct((n, BRANCH_W), BF16))
        scratch.append(pltpu.VMEM((2, half, tb, BRANCH_W), F32))
    out_specs.append(pl.BlockSpec((1, 2, HEAD_DIM, BRANCH_W), lambda b, c: (b, 0, 0, 0)))
    out_shape.append(jax.ShapeDtypeStruct((batch, 2, HEAD_DIM, BRANCH_W), F32))
    res = pl.pallas_call(
        functools.partial(_hgrn_kernel, n_sub=n_sub, n_blk=n_blk, has_init=has_init),
        grid=(batch, n_blk),
        in_specs=in_specs,
        out_specs=out_specs,
        out_shape=out_shape,
        scratch_shapes=scratch,
        compiler_params=_params("arbitrary", "arbitrary"),
        name="hgrn_mixer",
    )(*args)
    return tuple(res[:-1]), n_blk, tb, res[-1]


def _group_ones():
    g = np.arange(BRANCH_W) // HEAD_DIM
    return jnp.asarray((g[:, None] == g[None, :]).astype(np.float32), BF16)


def _ctx_attn_kernel(*refs, n_q, n_kv, has_sink):
    if has_sink:
        sink_ref, q_ref, k_ref, v_ref, o_ref = refs
    else:
        q_ref, k_ref, v_ref, o_ref = refs
    group = n_q // n_kv
    for slot, hq in enumerate(SWA_HEAD_ORDER if has_sink else range(n_q)):
        hk = hq // group
        q = q_ref[:, slot * HEAD_DIM:(slot + 1) * HEAD_DIM]
        k = k_ref[:, hk * HEAD_DIM:(hk + 1) * HEAD_DIM]
        v = v_ref[:, hk * HEAD_DIM:(hk + 1) * HEAD_DIM]
        s = _dot_nt(q, k)
        m = jnp.max(s, axis=-1, keepdims=True)
        if has_sink:
            sink = sink_ref[hq] * LOG2E
            m = jnp.maximum(m, sink)
        p = jnp.exp2(s - m)
        l = jnp.sum(p, axis=-1, keepdims=True)
        if has_sink:
            l = l + jnp.exp2(sink - m)
        o_ref[:, slot * HEAD_DIM:(slot + 1) * HEAD_DIM] = _bf(_dot(_bf(p), v) / l)


def _ctx_attn(zr, batch, seqlen, q_col, k_col, v_col, kv_width, n_kv, sink):
    n = batch * seqlen
    n_q = N_HEADS
    has_sink = sink is not None
    in_specs = [
        pl.BlockSpec((seqlen, BRANCH_W), lambda b: (b, q_col)),
        pl.BlockSpec((seqlen, kv_width), lambda b: (b, k_col)),
        pl.BlockSpec((seqlen, kv_width), lambda b: (b, v_col)),
    ]
    args = [zr, zr, zr]
    if has_sink:
        in_specs = [pl.BlockSpec(memory_space=pltpu.SMEM)] + in_specs
        args = [sink.astype(F32)] + args
    return pl.pallas_call(
        functools.partial(_ctx_attn_kernel, n_q=n_q, n_kv=n_kv, has_sink=has_sink),
        grid=(batch,),
        in_specs=in_specs,
        out_specs=pl.BlockSpec((seqlen, BRANCH_W), lambda b: (b, 0)),
        out_shape=jax.ShapeDtypeStruct((n, BRANCH_W), BF16),
        compiler_params=_params("arbitrary"),
        name="ctx_attn_sink" if has_sink else "ctx_attn",
    )(*args)


def _na_bias_kernel(rpb_ref, onehot_ref, mask_ref, o_ref):
    o_ref[...] = (_dot01_right(rpb_ref[...], onehot_ref[...]) + mask_ref[...]) * LOG2E


def _na_bias_tables(na_rpb):
    n_dr, n_dc = 2 * NA_ROWS - 1, 2 * NA_COLS - 1
    col = np.arange(GRID_W)
    col_start = np.clip(col - NA_COLS // 2, 0, GRID_W - NA_COLS)
    col_mask = (col[None, :] >= col_start[:, None]) & (col[None, :] < col_start[:, None] + NA_COLS)
    d_col = np.clip(col[None, :] - col[:, None], -(NA_COLS - 1), NA_COLS - 1) + (NA_COLS - 1)
    rows = DEPTH * N_HEADS * n_dr
    assert rows <= LANES and n_dc <= LANES
    onehot = (np.arange(LANES)[:, None] == d_col.reshape(1, -1)).astype(np.float32)
    mask_add = np.where(col_mask.reshape(1, -1), 0.0, -np.inf).astype(np.float32)
    rpb2 = jnp.zeros((LANES, LANES), F32).at[:rows, :n_dc].set(na_rpb.astype(F32).reshape(rows, n_dc))
    tab = pl.pallas_call(
        _na_bias_kernel,
        grid=(1,),
        in_specs=[_const_spec((LANES, LANES)), _const_spec((LANES, GRID_W * GRID_W)), _const_spec((1, GRID_W * GRID_W))],
        out_specs=_const_spec((LANES, GRID_W * GRID_W)),
        out_shape=jax.ShapeDtypeStruct((LANES, GRID_W * GRID_W), F32),
        compiler_params=_params("arbitrary"),
        name="na_bias",
    )(rpb2, jnp.asarray(onehot, BF16), jnp.asarray(mask_add))
    tab = tab[:rows].reshape(DEPTH, N_HEADS, n_dr, GRID_W, GRID_W)
    slabs = [jnp.transpose(tab[:, :, first:first + NA_ROWS], (0, 1, 3, 2, 4)).reshape(DEPTH, N_HEADS * GRID_W, NA_ROWS * GRID_W)
             for first in range(NA_ROWS)]
    return jnp.stack(slabs, axis=1)


def _na_lat_kernel(q_ref, k_ref, v_ref, kc_ref, vc_ref, bias_ref, o_ref, *, rows_per_step, n_rows):
    nk = NA_ROWS * GRID_W
    r0 = pl.program_id(1) * rows_per_step
    head_mask = _head_mask()

    def body(i, carry):
        r = r0 + i
        row_start = jnp.clip(r - NA_ROWS // 2, 0, n_rows - NA_ROWS)
        first = row_start - r + (NA_ROWS - 1)
        k0 = pl.multiple_of(row_start * GRID_W, GRID_W)
        q0 = pl.multiple_of(i * GRID_W, GRID_W)
        qx = _expand_heads(q_ref[pl.ds(q0, GRID_W), :], head_mask)
        s_lat = _dot_nt(qx, k_ref[pl.ds(k0, nk), :]) + bias_ref[first]
        s_ctx = _dot_nt(qx, kc_ref[0])
        m = jnp.maximum(jnp.max(s_lat, axis=-1, keepdims=True), jnp.max(s_ctx, axis=-1, keepdims=True))
        p_lat = jnp.exp2(s_lat - m)
        p_ctx = jnp.exp2(s_ctx - m)
        l = jnp.sum(p_lat, axis=-1, keepdims=True) + jnp.sum(p_ctx, axis=-1, keepdims=True)
        acc = _dot(_bf(p_lat), v_ref[pl.ds(k0, nk), :]) + _dot(_bf(p_ctx), vc_ref[0])
        o_ref[pl.ds(q0, GRID_W), :] = _bf(_collapse_heads(acc / l, head_mask))
        return carry

    lax.fori_loop(0, rows_per_step, body, 0, unroll=8)


def _na_latent(zr, kc, vc, bias_tab, batch, seqlen):
    n = batch * seqlen
    n_rows = seqlen // GRID_W
    assert n_rows >= NA_ROWS
    rows_per_step = 8
    steps = n_rows // rows_per_step
    tq = rows_per_step * GRID_W
    n_ctx = kc.shape[1]
    return pl.pallas_call(
        functools.partial(_na_lat_kernel, rows_per_step=rows_per_step, n_rows=n_rows),
        grid=(batch, steps),
        in_specs=[
            pl.BlockSpec((tq, BRANCH_W), lambda b, j: (b * steps + j, COL_NAQ)),
            pl.BlockSpec((seqlen, BRANCH_W), lambda b, j: (b, COL_NAK)),
            pl.BlockSpec((seqlen, BRANCH_W), lambda b, j: (b, COL_NAV)),
            pl.BlockSpec((1, n_ctx, BRANCH_W), lambda b, j: (b, 0, 0)),
            pl.BlockSpec((1, n_ctx, BRANCH_W), lambda b, j: (b, 0, 0)),
            _const_spec(bias_tab.shape),
        ],
        out_specs=pl.BlockSpec((tq, BRANCH_W), lambda b, j: (b * steps + j, 0)),
        out_shape=jax.ShapeDtypeStruct((n, BRANCH_W), BF16),
        compiler_params=_params("arbitrary", "arbitrary"),
        name="na_latent",
    )(zr, zr, zr, kc, vc, bias_tab)


def _rope_tables(seqlen):
    half = HEAD_DIM // 2
    t = np.arange(seqlen)
    rows = (t // GRID_W).astype(np.float32)
    cols = (t % GRID_W).astype(np.float32)
    inv = (1.0 / (np.float32(ROPE_THETA) ** (np.arange(0, half, 2, dtype=np.float32) / np.float32(half)))).astype(np.float32)
    ang_r = rows[:, None] * inv[None, :]
    ang_c = cols[:, None] * inv[None, :]
    cos = np.concatenate([np.cos(ang_r), np.cos(ang_r), np.cos(ang_c), np.cos(ang_c)], axis=-1)
    sin = np.concatenate([-np.sin(ang_r), np.sin(ang_r), -np.sin(ang_c), np.sin(ang_c)], axis=-1)
    cos = np.tile(cos.astype(np.float32), (1, N_HEADS))
    sin = np.tile(sin.astype(np.float32), (1, N_HEADS))
    return jnp.asarray(cos), jnp.asarray(sin)


def _rope(x, cos, sin_signed):
    w = x.shape[-1]
    lane = lax.broadcasted_iota(jnp.int32, x.shape, 1)
    partner = jnp.where((lane % 32) < 16, pltpu.roll(x, w - 16, 1), pltpu.roll(x, 16, 1))
    return x * cos + partner * sin_signed


def _swa_lat_kernel(sink_ref, q_ref, k_ref, v_ref, kc_ref, vc_ref, cos_ref, sin_ref, band_ref, o_ref, *, seqlen):
    for sub in range(SWA_BLOCKS_PER_STEP):
        rows = slice(sub * SWA_BLOCK, (sub + 1) * SWA_BLOCK)
        _swa_block(pl.program_id(1) * SWA_BLOCKS_PER_STEP + sub, sink_ref, q_ref.at[rows, :], k_ref, v_ref, kc_ref, vc_ref,
                   cos_ref, sin_ref, band_ref, o_ref.at[rows, :], seqlen)


def _swa_block(j, sink_ref, q_ref, k_ref, v_ref, kc_ref, vc_ref, cos_ref, sin_ref, band_ref, o_ref, seqlen):
    blk = SWA_BLOCK
    nwin = 3 * blk
    kvw = SWA_KV_HEADS * HEAD_DIM
    q0 = pl.multiple_of(j * blk, blk)
    k_blk = jnp.clip(j - 1, 0, seqlen // blk - 3)
    k0 = pl.multiple_of(k_blk * blk, blk)
    q = _rope(q_ref[...].astype(F32), cos_ref[pl.ds(q0, blk), :], sin_ref[pl.ds(q0, blk), :])
    kw = _rope(k_ref[pl.ds(k0, nwin), :].astype(F32), cos_ref[pl.ds(k0, nwin), 0:kvw], sin_ref[pl.ds(k0, nwin), 0:kvw])
    kw = _bf(kw)
    vw = v_ref[pl.ds(k0, nwin), :]
    n_slot = N_HEADS
    qb = _bf(q)
    lane_kv = lax.broadcasted_iota(jnp.int32, (blk, kvw), 1) // HEAD_DIM
    qx = jnp.concatenate(
        [jnp.where(lane_kv == (slot % SWA_KV_HEADS), qb[:, (slot // SWA_KV_HEADS) * kvw:(slot // SWA_KV_HEADS + 1) * kvw],
                   jnp.zeros((), BF16)) for slot in range(n_slot)], axis=0)
    row_slot = lax.broadcasted_iota(jnp.int32, (n_slot * blk, 1), 0) // blk
    sink = jnp.zeros((n_slot * blk, 1), F32)
    for slot in range(n_slot):
        sink = jnp.where(row_slot == slot, sink_ref[SWA_HEAD_ORDER[slot]] * LOG2E, sink)
    band = band_ref[j - k_blk]
    s_band = _dot_nt(qx, kw) + jnp.concatenate([band] * n_slot, axis=0)
    s_ctx = _dot_nt(qx, kc_ref[0])
    m = jnp.maximum(jnp.maximum(jnp.max(s_band, axis=-1, keepdims=True), jnp.max(s_ctx, axis=-1, keepdims=True)), sink)
    p_band = jnp.exp2(s_band - m)
    p_ctx = jnp.exp2(s_ctx - m)
    l = jnp.sum(p_band, axis=-1, keepdims=True) + jnp.sum(p_ctx, axis=-1, keepdims=True) + jnp.exp2(sink - m)
    acc = (_dot(_bf(p_band), vw) + _dot(_bf(p_ctx), vc_ref[0])) / l
    halves = []
    for half in range(n_slot // SWA_KV_HEADS):
        r0 = half * SWA_KV_HEADS * blk
        halves.append(jnp.where(lane_kv == 0, acc[r0:r0 + blk], acc[r0 + blk:r0 + 2 * blk]))
    o_ref[...] = _bf(jnp.concatenate(halves, axis=-1))


def _swa_latent(zr, kc, vc, sink, batch, seqlen):
    n = batch * seqlen
    nb = seqlen // SWA_BLOCK
    kvw = SWA_KV_HEADS * HEAD_DIM
    n_ctx = kc.shape[1]
    cos, sin = _rope_tables(seqlen)
    assert nb >= 3 and nb % SWA_BLOCKS_PER_STEP == 0
    steps = nb // SWA_BLOCKS_PER_STEP
    tq = SWA_BLOCKS_PER_STEP * SWA_BLOCK
    a = np.arange(SWA_BLOCK)[:, None]
    c = np.arange(3 * SWA_BLOCK)[None, :]
    band = jnp.asarray(np.stack([np.where(np.abs(c - a - off * SWA_BLOCK) <= SWA_WINDOW, 0.0, -np.inf) for off in range(3)])
                       .astype(np.float32))
    return pl.pallas_call(
        functools.partial(_swa_lat_kernel, seqlen=seqlen),
        grid=(batch, steps),
        in_specs=[
            pl.BlockSpec(memory_space=pltpu.SMEM),
            pl.BlockSpec((tq, BRANCH_W), lambda b, j: (b * steps + j, COL_SQ)),
            pl.BlockSpec((seqlen, kvw), lambda b, j: (b, COL_SK128)),
            pl.BlockSpec((seqlen, kvw), lambda b, j: (b, COL_SV128)),
            pl.BlockSpec((1, n_ctx, kvw), lambda b, j: (b, 0, 0)),
            pl.BlockSpec((1, n_ctx, kvw), lambda b, j: (b, 0, 0)),
            _const_spec(cos.shape),
            _const_spec(sin.shape),
            _const_spec(band.shape),
        ],
        out_specs=pl.BlockSpec((tq, BRANCH_W), lambda b, j: (b * steps + j, 0)),
        out_shape=jax.ShapeDtypeStruct((n, BRANCH_W), BF16),
        compiler_params=_params("arbitrary", "arbitrary"),
        name="swa_latent",
    )(sink.astype(F32), zr, zr, zr, kc, vc, cos, sin, band)


def _merge_kernel(*refs, hgrn_blocks):
    if hgrn_blocks > 1:
        a_hi_ref, a_lo_ref = refs[:2]
        a = jnp.where(pl.program_id(0) % hgrn_blocks >= hgrn_blocks // 2, a_hi_ref[...], a_lo_ref[...])
        refs = refs[2:]
    else:
        a = refs[0][...]
        refs = refs[1:]
    (b_ref, c_ref, d_ref, g0_ref, g1_ref, g2_ref, g3_ref, x_ref, gate_ref, sh_ref, sc_ref, ng_ref, wb_ref, wo_ref, wr_ref,
     br_ref, tri_ref, upper_ref, x1_ref, h_ref, route_ref, seg_ref) = refs
    tm = x_ref.shape[0]
    mix = None
    for br, gt, i in ((a, g0_ref, 0), (b_ref[...], g1_ref, 1), (c_ref[...], g2_ref, 2), (d_ref[...], g3_ref, 3)):
        t = (1.0 + jnp.tanh(gt[...].astype(F32))) * _dot(br, wb_ref[i])
        mix = t if mix is None else mix + t
    x1 = x_ref[...] + gate_ref[0] * _dot(_bf(mix), wo_ref[...])
    x1_ref[...] = x1
    ms = jnp.mean(x1 * x1, axis=-1, keepdims=True)
    h = x1 * lax.rsqrt(ms + EPS) * ng_ref[...]
    h = h * (1.0 + sc_ref[0]) + sh_ref[0]
    h_ref[...] = _bf(h)

    hh = _bf(h)
    hm = _bf(h - hh.astype(F32))
    logits = (_dot(hh, wr_ref[0]) + _dot(hm, wr_ref[0]) + _dot(hh, wr_ref[1])) + br_ref[...]
    lane_i = lax.broadcasted_iota(jnp.int32, logits.shape, 1)
    lane = lane_i.astype(F32)
    lane_grp = (lane_i // EXPERTS_PER_GROUP).astype(F32)
    neg = -jnp.inf
    far = float(4 * N_EXPERTS)
    is_g = (lane_i >= N_EXPERTS) & (lane_i < N_EXPERTS + N_GROUPS)
    gl = jnp.where(is_g, logits, neg)
    gmax = jnp.max(gl, axis=-1, keepdims=True)
    gsum = jnp.sum(jnp.exp(gl - gmax), axis=-1, keepdims=True)
    g_top_p = 1.0 / gsum
    g_idx = jnp.min(jnp.where(is_g & (gl == gmax), lane, far), axis=-1, keepdims=True) - float(N_EXPERTS)
    in_grp = (lane_i < N_EXPERTS) & (lane_grp == g_idx)
    e_l = jnp.where(in_grp, logits, neg)
    e1 = jnp.max(e_l, axis=-1, keepdims=True)
    i1 = jnp.min(jnp.where(in_grp & (e_l == e1), lane, far), axis=-1, keepdims=True)
    e_l2 = jnp.where(lane == i1, neg, e_l)
    e2 = jnp.max(e_l2, axis=-1, keepdims=True)
    i2 = jnp.min(jnp.where(in_grp & (lane != i1) & (e_l2 == e2), lane, far), axis=-1, keepdims=True)
    t2 = jnp.exp(e2 - e1)
    w1 = g_top_p / (1.0 + t2)
    w2 = w1 * t2

    sel = (lane == i1) | (lane == i2)
    sel_f = jnp.where(sel, 1.0, 0.0)
    cum = _dot(tri_ref[...], _bf(sel_f))
    counts = cum[tm - 1:tm, :]
    padded = jnp.floor((counts + (MOE_ROW_ALIGN - 1)) * (1.0 / MOE_ROW_ALIGN)) * MOE_ROW_ALIGN
    seg_start = _dot(_bf(jnp.broadcast_to(padded, (SUBLANES, LANES))), upper_ref[...])[0:1, :]
    slot = seg_start + cum - sel_f
    pos1 = jnp.sum(jnp.where(lane == i1, slot, 0.0), axis=-1, keepdims=True)
    pos2 = jnp.sum(jnp.where(lane == i2, slot, 0.0), axis=-1, keepdims=True)
    route_ref[...] = jnp.where(lane_i == 0, pos1, jnp.where(lane_i == 1, pos2, jnp.where(lane_i == 2, w1,
                               jnp.where(lane_i == 3, w2, 0.0))))
    seg = jnp.where(lane_i[0:1] < N_EXPERTS, seg_start, pltpu.roll(jnp.broadcast_to(padded, (SUBLANES, LANES)), N_EXPERTS, 1)[0:1])
    seg_ref[0] = jnp.where(lane_i[0:1] < 2 * N_EXPERTS, seg, 0.0).astype(jnp.int32)


def _router_tables(w_rg, b_rg, w_re, b_re):
    w = jnp.zeros((D_MODEL, LANES), F32)
    w = w.at[:, :N_EXPERTS].set(w_re.astype(F32)).at[:, N_EXPERTS:N_EXPERTS + N_GROUPS].set(w_rg.astype(F32))
    b = jnp.zeros((1, LANES), F32)
    b = b.at[0, :N_EXPERTS].set(b_re.astype(F32)).at[0, N_EXPERTS:N_EXPERTS + N_GROUPS].set(b_rg.astype(F32))
    return jnp.stack(_split3(w)[:2], axis=0), b


def _merge(a_parts, hgrn_blocks, hgrn_block_rows, branches, zr, x, mods3, mod_row, norm_g, w_branch_bf, w_out_bf, wr3, br,
           layer):
    n = x.shape[0]
    half = hgrn_blocks // 2
    in_batch = lambda i: (i // hgrn_blocks) * hgrn_blocks
    if hgrn_blocks > 1:
        assert hgrn_block_rows == MOE_TILE
        a_specs = [pl.BlockSpec((MOE_TILE, BRANCH_W), lambda i: (in_batch(i) + jnp.maximum(i % hgrn_blocks, half), 0)),
                   pl.BlockSpec((MOE_TILE, BRANCH_W), lambda i: (in_batch(i) + jnp.minimum(i % hgrn_blocks, half - 1), 0))]
    else:
        a_specs = [pl.BlockSpec((MOE_TILE, BRANCH_W), lambda i: (i, 0))]
    per_layer = lambda shape: pl.BlockSpec((None,) + shape, lambda i: (layer,) + (0,) * len(shape))
    tm = MOE_TILE
    t = np.arange(tm)
    tri = jnp.asarray((t[None, :] <= t[:, None]).astype(np.float32), BF16)
    e = np.arange(LANES)
    upper = jnp.asarray((e[:, None] < e[None, :]).astype(np.float32), BF16)
    row = lambda w: pl.BlockSpec((tm, w), lambda i: (i, 0))
    gate = lambda k: pl.BlockSpec((tm, D_MODEL), lambda i: (i, COL_GATES1024 + k))
    mod = lambda k: pl.BlockSpec((1, 1, D_MODEL), lambda i: (mod_row(i, tm), 0, k))
    return pl.pallas_call(
        functools.partial(_merge_kernel, hgrn_blocks=hgrn_blocks),
        grid=(n // tm,),
        in_specs=a_specs + [
            row(BRANCH_W), row(BRANCH_W), row(BRANCH_W),
            gate(0), gate(1), gate(2), gate(3),
            row(D_MODEL),
            mod(2), mod(3), mod(4),
            _const_spec((1, D_MODEL)),
            per_layer((N_BRANCH, BRANCH_W, D_MODEL)),
            per_layer((D_MODEL, D_MODEL)),
            per_layer((2, D_MODEL, LANES)),
            per_layer((1, LANES)),
            _const_spec((tm, tm)),
            _const_spec((LANES, LANES)),
        ],
        out_specs=[row(D_MODEL), row(D_MODEL), row(LANES), pl.BlockSpec((1, 1, LANES), lambda i: (i, 0, 0))],
        out_shape=[
            jax.ShapeDtypeStruct((n, D_MODEL), F32),
            jax.ShapeDtypeStruct((n, D_MODEL), BF16),
            jax.ShapeDtypeStruct((n, LANES), F32),
            jax.ShapeDtypeStruct((n // tm, 1, LANES), jnp.int32),
        ],
        compiler_params=_params("arbitrary"),
        name="merge",
    )(*a_parts, *branches, zr, zr, zr, zr, x, mods3, mods3, mods3, norm_g.reshape(1, D_MODEL), w_branch_bf, w_out_bf, wr3, br,
      tri, upper)


def _moe_kernel(seg_ref, h_ref, route_ref, x1_ref, gate_ref, fg_ref, wg_ref, wu_ref, wd_ref, *outs_and_scratch, final):
    if final:
        x2_ref, y_ref, hs_scr, ys_scr = outs_and_scratch
    else:
        x2_ref, hs_scr, ys_scr = outs_and_scratch
    tm = MOE_TILE
    tile = pl.program_id(0)
    route = route_ref[...]
    route_t = route.T
    pos1_row, pos2_row = route_t[0:1, :], route_t[1:2, :]
    h = h_ref[...]
    for rb in range(MOE_ROWS // MOE_GATHER_BLK):
        r = (rb * MOE_GATHER_BLK + lax.broadcasted_iota(jnp.int32, (MOE_GATHER_BLK, tm), 0)).astype(F32)
        p = jnp.where((r == pos1_row) | (r == pos2_row), 1.0, 0.0)
        hs_scr[rb * MOE_GATHER_BLK:(rb + 1) * MOE_GATHER_BLK, :] = _bf(_dot(_bf(p), h))
    ys_scr[...] = jnp.zeros_like(ys_scr)

    row_in_chunk = lax.broadcasted_iota(jnp.int32, (MOE_CHUNK, D_MODEL), 0)

    def expert_chunk(e, r0, end):
        r0 = pl.multiple_of(r0, MOE_ROW_ALIGN)
        rows = hs_scr[pl.ds(r0, MOE_CHUNK), :]
        a = _silu(_dot(rows, wg_ref[e])) * _dot(rows, wu_ref[e])
        y = _dot(_bf(a), wd_ref[e])
        ys_scr[pl.ds(r0, MOE_CHUNK), :] = jnp.where(row_in_chunk < end - r0, _bf(y), ys_scr[pl.ds(r0, MOE_CHUNK), :])

    starts = [seg_ref[tile * LANES + e] for e in range(N_EXPERTS)]
    ends = [starts[e] + seg_ref[tile * LANES + N_EXPERTS + e] for e in range(N_EXPERTS)]
    for e in range(N_EXPERTS):
        expert_chunk(e, starts[e], ends[e])
    for e in range(N_EXPERTS):
        n_chunks = lax.div(ends[e] - starts[e] + (MOE_CHUNK - 1), MOE_CHUNK)

        def more(c, carry, e=e):
            expert_chunk(e, starts[e] + c * MOE_CHUNK, ends[e])
            return carry

        lax.fori_loop(1, n_chunks, more, 0)

    pos1, pos2, w1, w2 = route[:, 0:1], route[:, 1:2], route[:, 2:3], route[:, 3:4]
    acc = None
    for cb in range(MOE_SEG_ROWS // MOE_SCATTER_BLK):
        r = (cb * MOE_SCATTER_BLK + lax.broadcasted_iota(jnp.int32, (tm, MOE_SCATTER_BLK), 1)).astype(F32)
        q = jnp.where(r == pos1, w1, jnp.where(r == pos2, w2, 0.0))
        part = _dot(_bf(q), ys_scr[cb * MOE_SCATTER_BLK:(cb + 1) * MOE_SCATTER_BLK, :])
        acc = part if acc is None else acc + part
    x2 = x1_ref[...] + gate_ref[0] * acc
    x2_ref[...] = x2
    if final:
        ms = jnp.mean(x2 * x2, axis=-1, keepdims=True)
        y_ref[...] = x2 * lax.rsqrt(ms + EPS) * fg_ref[...]


def _moe(h, route, seg, x1, mods3, mod_row, final_g, wg_bf, wu_bf, wd_bf, layer, final):
    n = x1.shape[0]
    tm = MOE_TILE
    row = lambda w: pl.BlockSpec((tm, w), lambda i, s: (i, 0))
    resident = lambda shape: pl.BlockSpec((None,) + shape, lambda i, s: (layer,) + (0,) * len(shape),
                                          pipeline_mode=pl.Buffered(1))
    out_specs = [row(D_MODEL)]
    out_shape = [jax.ShapeDtypeStruct((n, D_MODEL), F32)]
    if final:
        out_specs.append(row(D_MODEL))
        out_shape.append(jax.ShapeDtypeStruct((n, D_MODEL), F32))
    return pl.pallas_call(
        functools.partial(_moe_kernel, final=final),
        grid_spec=pltpu.PrefetchScalarGridSpec(
            num_scalar_prefetch=1,
            grid=(n // tm,),
            in_specs=[
                row(D_MODEL), row(LANES), row(D_MODEL),
                pl.BlockSpec((1, 1, D_MODEL), lambda i, s: (mod_row(i, tm), 0, 5)),
                pl.BlockSpec((1, D_MODEL), lambda i, s: (0, 0)),
                resident((N_EXPERTS, D_MODEL, EXPERT_FF)),
                resident((N_EXPERTS, D_MODEL, EXPERT_FF)),
                resident((N_EXPERTS, EXPERT_FF, D_MODEL)),
            ],
            out_specs=out_specs,
            scratch_shapes=[pltpu.VMEM((MOE_ROWS, D_MODEL), BF16), pltpu.VMEM((MOE_ROWS, D_MODEL), BF16)],
        ),
        out_shape=out_shape,
        compiler_params=_params("arbitrary"),
        name="moe_final" if final else "moe",
    )(seg.reshape(-1), h, route, x1, mods3, final_g.reshape(1, D_MODEL), wg_bf, wu_bf, wd_bf)


def _state_to_blockdiag_t(s):
    b = s.shape[0]
    st = jnp.swapaxes(s.astype(F32), -1, -2)
    eye = jnp.eye(N_HEADS, dtype=F32)
    full = st[:, :, :, :, None, :] * eye[None, None, :, None, :, None]
    return full.reshape(b, 2, BRANCH_W, BRANCH_W)


def _compact_to_state(fin):
    b = fin.shape[0]
    return jnp.transpose(fin.reshape(b, 2, HEAD_DIM, N_HEADS, HEAD_DIM), (0, 1, 3, 4, 2))


def _layer(x, l, w, mods3, mod_row, batch, seqlen, latent, st0, caches, final):
    proj = _in_proj(x, mods3, mod_row, w["norm1_g"][l], w["w_in"], w["smlp_ws"][l], w["smlp_b"][l], l, IN_TILE, not latent)
    zf, zr, c_out = proj[0], proj[1], proj[2]
    kv = None if latent else proj[3:]
    a_parts, hgrn_blocks, hgrn_block_rows, fin = _hgrn_mixer(
        zf, zr, w["lb"][l], w["onorm_g"][l], st0, batch, seqlen, n_sub=min(HGRN_CHUNKS_PER_STEP, seqlen // HGRN_CHUNK))
    if latent:
        ck_na, cv_na, ck_swa, cv_swa = caches
        b_out = _na_latent(zr, ck_na, cv_na, w["na_bias"][l], batch, seqlen)
        d_out = _swa_latent(zr, ck_swa, cv_swa, w["swa_sink"][l], batch, seqlen)
    else:
        b_out = _ctx_attn(zr, batch, seqlen, COL_NAQ, COL_NAK, COL_NAV, BRANCH_W, N_HEADS, None)
        d_out = _ctx_attn(zr, batch, seqlen, COL_SQ, COL_SK128, COL_SV128, SWA_KV_HEADS * HEAD_DIM, SWA_KV_HEADS,
                          w["swa_sink"][l])
    x1, h2, route, seg = _merge(a_parts, hgrn_blocks, hgrn_block_rows, (b_out, c_out, d_out), zr, x, mods3, mod_row,
                                w["norm2_g"][l], w["w_branch"], w["w_out"], w["wr"], w["br"], l)
    out = _moe(h2, route, seg, x1, mods3, mod_row, w["final_g"], w["wg"], w["wu"], w["wd"], l, final)
    return out, kv, fin


def kernel(x_prompt, x_sample, c, cache_na_k, cache_na_v, cache_swa_k, cache_swa_v, state_hgrn, c_ctx, w_ada, b_ada, norm1_g, norm2_g, w_in, hgrn_lb, hgrn_onorm_g, na_rpb, smlp_ws, smlp_b, swa_sink, w_branch, w_out, router_g_w, router_g_b, router_e_w, router_e_b, moe_w_gate, moe_w_up, moe_w_down, final_g):
    bc, lc, _ = x_prompt.shape
    bl, ll, _ = x_sample.shape
    n_ctx_tok = bc * lc

    cond = jnp.zeros((MOD_ROWS, D_MODEL), F32).at[0].set(c_ctx.astype(F32)).at[1:1 + bl].set(c.astype(F32))
    mods = _ada_mods(cond, w_ada, b_ada)

    lb_soft = jax.nn.softmax(hgrn_lb.astype(F32), axis=0)
    lb_all = jnp.cumsum(lb_soft, axis=0) - lb_soft[0:1]

    col_scale = np.ones((P_IN,), np.float32)
    col_scale[ZF_W + COL_GATES1024 * 1024:] = 0.5
    for q_col in (COL_NAQ, COL_SQ):
        col_scale[ZF_W + q_col * BRANCH_W:ZF_W + (q_col + 1) * BRANCH_W] = QK_PRESCALE
    swa_rows = jnp.concatenate([w_branch[:, N_BRANCH - 1, h * HEAD_DIM:(h + 1) * HEAD_DIM] for h in SWA_HEAD_ORDER], axis=1)
    routers = [_router_tables(router_g_w[l], router_g_b[l], router_e_w[l], router_e_b[l]) for l in range(DEPTH)]
    w = dict(
        norm1_g=norm1_g, norm2_g=norm2_g, lb=lb_all, onorm_g=hgrn_onorm_g, smlp_ws=smlp_ws, smlp_b=smlp_b,
        swa_sink=swa_sink, final_g=final_g,
        w_in=_bf(w_in * jnp.asarray(col_scale)[None, None, :]),
        w_branch=_bf(w_branch.at[:, N_BRANCH - 1].set(swa_rows)),
        w_out=_bf(0.5 * w_out),
        wr=jnp.stack([r[0] for r in routers]), br=jnp.stack([r[1] for r in routers]),
        wg=_bf(moe_w_gate), wu=_bf(moe_w_up), wd=_bf(moe_w_down),
        na_bias=_na_bias_tables(na_rpb),
    )

    ctx_row = lambda i, tm: 0
    lat_row = lambda i, tm: 1 + i // (ll // tm)
    xp = x_prompt.reshape(n_ctx_tok, D_MODEL)
    kvs, states = [], []
    y_prompt = None
    for l in range(DEPTH):
        final = l == DEPTH - 1
        out, kv, fin = _layer(xp, l, w, mods[l].reshape(MOD_ROWS, 1, -1), ctx_row, bc, lc, False, None, None, final)
        if final:
            xp, y_prompt = out
        else:
            xp = out[0]
        kvs.append(kv)
        states.append(_compact_to_state(fin))
    cache_out = [jnp.stack([kvs[l][k].reshape(bc, lc, -1, HEAD_DIM) for l in range(DEPTH)], axis=1) for k in range(4)]

    xs = x_sample.reshape(bl * ll, D_MODEL)
    y_sample = None
    n_past = cache_na_k.shape[2]
    for l in range(DEPTH):
        caches = (_bf(cache_na_k[:, l]).reshape(bl, n_past, BRANCH_W), _bf(cache_na_v[:, l]).reshape(bl, n_past, BRANCH_W),
                  _bf(cache_swa_k[:, l]).reshape(bl, n_past, SWA_KV_HEADS * HEAD_DIM),
                  _bf(cache_swa_v[:, l]).reshape(bl, n_past, SWA_KV_HEADS * HEAD_DIM))
        final = l == DEPTH - 1
        out, _, _ = _layer(xs, l, w, mods[l].reshape(MOD_ROWS, 1, -1), lat_row, bl, ll, True,
                           _state_to_blockdiag_t(state_hgrn[:, l]), caches, final)
        if final:
            xs, y_sample = out
        else:
            xs = out[0]

    return (y_prompt.reshape(bc, lc, D_MODEL), y_sample.reshape(bl, ll, D_MODEL), *cache_out, jnp.stack(states, axis=1))
```

```python
import functools

import numpy as np
import jax
import jax.numpy as jnp
from jax import lax
from jax.experimental import pallas as pl
from jax.experimental.pallas import tpu as pltpu

D_MODEL = 1024
DEPTH = 2
GRID_W = 64
HEAD_DIM = 64
N_BRANCH = 4
BRANCH_W = 256
N_HEADS = 4
HGRN_CHUNK = 64
HGRN_CHUNKS_PER_STEP = 8
NA_ROWS = 8
NA_COLS = 16
SMLP_GROUPS = 4
SMLP_CHUNK = 128
SWA_KV_HEADS = 2
SWA_HEAD_ORDER = (0, 2, 1, 3)
SWA_WINDOW = 128
SWA_BLOCK = 128
SWA_BLOCKS_PER_STEP = 4
ROPE_THETA = 10000.0
N_GROUPS = 4
EXPERTS_PER_GROUP = 4
N_EXPERTS = 16
EXPERT_FF = 256
ADA_CHUNKS = 6
EPS = 1e-6
TINY = 1e-30
P_IN = 7168
ATT_SCALE = HEAD_DIM ** -0.5
LOG2E = 1.4426950408889634
QK_PRESCALE = ATT_SCALE * LOG2E

ZF_W = 1024
ZR_W = P_IN - ZF_W
COL_HQ, COL_HI, COL_HFF, COL_HFB = 0, 1, 2, 3
COL_HOG, COL_NAQ, COL_NAK, COL_NAV = 0, 1, 2, 3
COL_MU, COL_MV, COL_SQ = 4, 5, 6
COL_SK128, COL_SV128 = 14, 15
COL_GATES1024 = 2
IN_COL_CHUNK = 1024
IN_TILE = 512

MOD_ROWS = 16
VMEM_LIMIT = 56 * 1024 * 1024

F32 = jnp.float32
BF16 = jnp.bfloat16

MOE_TILE = 512
MOE_ROW_ALIGN = 16
MOE_CHUNK = 96
MOE_GATHER_BLK = 128
MOE_SCATTER_BLK = 256
MOE_SEG_ROWS = -(-(2 * MOE_TILE + N_EXPERTS * (MOE_ROW_ALIGN - 1)) // MOE_SCATTER_BLK) * MOE_SCATTER_BLK
MOE_ROWS = -(-(2 * MOE_TILE + N_EXPERTS * (MOE_ROW_ALIGN - 1) + MOE_CHUNK) // MOE_GATHER_BLK) * MOE_GATHER_BLK
HGRN_LEVELS = (0, 1, 2, 4, 8, 16, 32)
SUBLANES = 8
LANES = 128
HGRN_MXU_REF_LEVELS = ()


def _bf(x):
    return x.astype(BF16)


def _dot(a, b):
    return jnp.dot(a, b, preferred_element_type=F32)


def _dot_nt(a, b):
    return lax.dot_general(a, b, (((1,), (1,)), ((), ())), preferred_element_type=F32)


def _dot_tn(a, b):
    return lax.dot_general(a, b, (((0,), (0,)), ((), ())), preferred_element_type=F32)


def _split3(x):
    hi = _bf(x)
    r1 = x - hi.astype(F32)
    mid = _bf(r1)
    lo = _bf(r1 - mid.astype(F32))
    return hi, mid, lo


def _dot01_left(m01, x):
    hi, mid, lo = _split3(x)
    return _dot(m01, hi) + _dot(m01, mid) + _dot(m01, lo)


def _dot01_right(x, m01):
    hi, mid, lo = _split3(x)
    return _dot(hi, m01) + _dot(mid, m01) + _dot(lo, m01)


def _sigmoid(x):
    return 0.5 * jnp.tanh(0.5 * x) + 0.5


def _silu(x):
    return x * _sigmoid(x)


def _params(*sem):
    return pltpu.CompilerParams(dimension_semantics=sem, vmem_limit_bytes=VMEM_LIMIT)


def _const_spec(shape):
    n = len(shape)
    return pl.BlockSpec(shape, lambda *_: (0,) * n)


def _head_mask():
    row = lax.broadcasted_iota(jnp.int32, (N_HEADS * HEAD_DIM, BRANCH_W), 0)
    lane = lax.broadcasted_iota(jnp.int32, (N_HEADS * HEAD_DIM, BRANCH_W), 1)
    return (row // HEAD_DIM) == (lane // HEAD_DIM)


def _expand_heads(x, head_mask):
    return jnp.where(head_mask, jnp.concatenate([x] * N_HEADS, axis=0), jnp.zeros((), x.dtype))


def _collapse_heads(r, head_mask):
    r = jnp.where(head_mask, r, 0.0)
    n = HEAD_DIM
    return (r[0:n] + r[n:2 * n]) + (r[2 * n:3 * n] + r[3 * n:4 * n])


def _ada_kernel(cond_ref, w_ref, b_ref, o_ref):
    s = _silu(cond_ref[...])
    o_ref[0] = _dot(_bf(s), _bf(w_ref[0])) + b_ref[0]


def _ada_mods(cond, w_ada, b_ada):
    tn = 1536
    n = ADA_CHUNKS * D_MODEL
    return pl.pallas_call(
        _ada_kernel,
        grid=(DEPTH, n // tn),
        in_specs=[
            pl.BlockSpec((MOD_ROWS, D_MODEL), lambda l, j: (0, 0)),
            pl.BlockSpec((1, D_MODEL, tn), lambda l, j: (l, 0, j)),
            pl.BlockSpec((1, 1, tn), lambda l, j: (l, 0, j)),
        ],
        out_specs=pl.BlockSpec((1, MOD_ROWS, tn), lambda l, j: (l, 0, j)),
        out_shape=jax.ShapeDtypeStruct((DEPTH, MOD_ROWS, n), F32),
        compiler_params=_params("arbitrary", "arbitrary"),
        name="ada_mods",
    )(cond, w_ada, b_ada.reshape(DEPTH, 1, n))


def _smlp_tile(u, v, ws_ref, bias_ref, ones_ref):
    ms = _dot01_right(v * v, ones_ref[...]) * (1.0 / HEAD_DIM)
    vn = _bf(v * lax.rsqrt(ms + EPS))
    lane_g = lax.broadcasted_iota(jnp.int32, (SMLP_CHUNK, BRANCH_W), 1) // HEAD_DIM
    outs = []
    for ci in range(u.shape[0] // SMLP_CHUNK):
        rows = slice(ci * SMLP_CHUNK, (ci + 1) * SMLP_CHUNK)
        mixed = bias_ref[...]
        for g in range(SMLP_GROUPS):
            mixed = mixed + jnp.where(lane_g == g, _dot(ws_ref[g], vn[rows]), 0.0)
        outs.append(u[rows] * mixed)
    return jnp.concatenate(outs, axis=0)


def _in_kernel(x_ref, sh_ref, sc_ref, g_ref, w_ref, ws_ref, bias_ref, ones_ref, zf_ref, zr_ref, c_ref, *rest, want_kv):
    x = x_ref[...]
    ms = jnp.mean(x * x, axis=-1, keepdims=True)
    h = x * lax.rsqrt(ms + EPS) * g_ref[...]
    h = _bf(h * (1.0 + sc_ref[0]) + sh_ref[0])
    tn = IN_COL_CHUNK
    kvw = SWA_KV_HEADS * HEAD_DIM
    for j in range(P_IN // tn):
        acc = _dot(h, w_ref[:, j * tn:(j + 1) * tn])
        if j == 0:
            zf_ref[...] = acc
            continue
        c0 = (j - 1) * tn
        sq0 = COL_SQ * BRANCH_W - c0
        if 0 <= sq0 < tn:
            heads = [acc[:, sq0 + hd * HEAD_DIM:sq0 + (hd + 1) * HEAD_DIM] for hd in SWA_HEAD_ORDER]
            zr_ref[:, c0:c0 + tn] = _bf(jnp.concatenate([acc[:, :sq0]] + heads + [acc[:, sq0 + BRANCH_W:]], axis=1))
        else:
            zr_ref[:, c0:c0 + tn] = _bf(acc)
        mu0 = COL_MU * BRANCH_W - c0
        if 0 <= mu0 and mu0 + 2 * BRANCH_W <= tn:
            assert COL_MV == COL_MU + 1
            c_ref[...] = _bf(_smlp_tile(acc[:, mu0:mu0 + BRANCH_W], acc[:, mu0 + BRANCH_W:mu0 + 2 * BRANCH_W],
                                        ws_ref, bias_ref, ones_ref))
        if want_kv:
            nak_ref, nav_ref, sk_ref, sv_ref = rest
            for ref, col, width in ((nak_ref, COL_NAK * BRANCH_W, BRANCH_W), (nav_ref, COL_NAV * BRANCH_W, BRANCH_W),
                                    (sk_ref, COL_SK128 * kvw, kvw), (sv_ref, COL_SV128 * kvw, kvw)):
                if c0 <= col < c0 + tn:
                    ref[...] = acc[:, col - c0:col - c0 + width]


def _in_proj(x, mods3, mod_row, norm_g, w_in_bf, smlp_ws, smlp_b, layer, tm, want_kv):
    n = x.shape[0]
    assert ZF_W == IN_COL_CHUNK and tm % SMLP_CHUNK == 0
    kvw = SWA_KV_HEADS * HEAD_DIM
    bias = jnp.repeat(smlp_b.astype(F32).T, BRANCH_W // SMLP_GROUPS, axis=1)
    row = lambda w: pl.BlockSpec((tm, w), lambda i: (i, 0))
    out_specs = [row(ZF_W), row(ZR_W), row(BRANCH_W)]
    out_shape = [jax.ShapeDtypeStruct((n, ZF_W), F32), jax.ShapeDtypeStruct((n, ZR_W), BF16),
                 jax.ShapeDtypeStruct((n, BRANCH_W), BF16)]
    if want_kv:
        for width in (BRANCH_W, BRANCH_W, kvw, kvw):
            out_specs.append(row(width))
            out_shape.append(jax.ShapeDtypeStruct((n, width), F32))
    return pl.pallas_call(
        functools.partial(_in_kernel, want_kv=want_kv),
        grid=(n // tm,),
        in_specs=[
            row(D_MODEL),
            pl.BlockSpec((1, 1, D_MODEL), lambda i: (mod_row(i, tm), 0, 0)),
            pl.BlockSpec((1, 1, D_MODEL), lambda i: (mod_row(i, tm), 0, 1)),
            _const_spec((1, D_MODEL)),
            pl.BlockSpec((None, D_MODEL, P_IN), lambda i: (layer, 0, 0), pipeline_mode=pl.Buffered(1)),
            _const_spec((SMLP_GROUPS, SMLP_CHUNK, SMLP_CHUNK)),
            _const_spec((SMLP_CHUNK, BRANCH_W)),
            _const_spec((BRANCH_W, BRANCH_W)),
        ],
        out_specs=out_specs,
        out_shape=out_shape,
        compiler_params=_params("arbitrary"),
        name="in_proj_kv" if want_kv else "in_proj",
    )(x, mods3, mods3, norm_g.reshape(1, D_MODEL), w_in_bf, _bf(smlp_ws), bias, _group_ones())


def _hgrn_tables():
    c = HGRN_CHUNK
    t = np.arange(c)
    tsel = np.zeros((2, (1 + len(HGRN_MXU_REF_LEVELS)) * c, c), np.float32)
    pmask = np.zeros((2, len(HGRN_LEVELS), c, N_HEADS * c), np.float32)
    for rev in (0, 1):
        cum = (t[None, :] >= t[:, None]) if rev else (t[None, :] <= t[:, None])
        tsel[rev, :c] = cum
        pmask[rev, 0] = np.tile(np.eye(c, dtype=np.float32), (1, N_HEADS))
        for li, m in enumerate(HGRN_LEVELS[1:], start=1):
            if m in HGRN_MXU_REF_LEVELS:
                slot = 1 + HGRN_MXU_REF_LEVELS.index(m)
                tsel[rev, slot * c:(slot + 1) * c] = cum[(t // (2 * m)) * (2 * m) + (m - 1 if rev else m)]
            same = (t[:, None] // (2 * m)) == (t[None, :] // (2 * m))
            q_half = ((t & m) == 0) if rev else ((t & m) != 0)
            k_half = ~q_half
            pmask[rev, li] = np.tile((same & q_half[:, None] & k_half[None, :]).astype(np.float32), (1, N_HEADS))
    return jnp.asarray(tsel, BF16), jnp.asarray(pmask, F32)


def _ref_rows(b_ref, m, rev):
    c = HGRN_CHUNK
    off = (m - 1) if rev else m
    row = lambda r, n: jnp.broadcast_to(b_ref[pl.ds(r, 1), :], (n, BRANCH_W))
    if 2 * m >= SUBLANES:
        return jnp.concatenate([row(s + off, 2 * m) for s in range(0, c, 2 * m)], axis=0)
    sub = lax.broadcasted_iota(jnp.int32, (SUBLANES, BRANCH_W), 0)
    tiles = []
    for t0 in range(0, c, SUBLANES):
        cur = row(t0 + off, SUBLANES)
        for s in range(2 * m, SUBLANES, 2 * m):
            cur = jnp.where(sub >= s, row(t0 + s + off, SUBLANES), cur)
        tiles.append(cur)
    return jnp.concatenate(tiles, axis=0)


def _hgrn_chunk(q_raw, v, f_raw, lb, st, tsel, pmask_ref, rev, head_mask, b_ref):
    c = HGRN_CHUNK
    qq = _silu(q_raw)
    f = lb + (1.0 - lb) * _sigmoid(f_raw)
    lf = jnp.log2(jnp.maximum(f, TINY))
    k = 1.0 - f
    ball = _dot01_left(tsel, lf)
    b = ball[:c]
    b_ref[...] = b
    bl = b[0:1] if rev else b[c - 1:c]
    vb = _bf(v)
    v_x = _expand_heads(vb, head_mask)

    o = _dot_nt(_bf(qq * jnp.exp2(b)), _bf(st))

    p = None
    qb, kb = _bf(qq), _bf(k)
    for li, m in enumerate(HGRN_LEVELS):
        if m == 0:
            qe, ke = qb, kb
        else:
            if m in HGRN_MXU_REF_LEVELS:
                slot = 1 + HGRN_MXU_REF_LEVELS.index(m)
                ref = ball[slot * c:(slot + 1) * c]
            else:
                ref = _ref_rows(b_ref, m, rev)
            e = _bf(jnp.exp2(-jnp.abs(b - ref)))
            qe, ke = qb * e, kb * e
        s = _dot_nt(qe, _expand_heads(ke, head_mask)) * pmask_ref[li]
        p = s if p is None else p + s
    o = o + _dot(_bf(p), v_x)

    ke_state = _bf(k * jnp.exp2(bl - b))
    st_new = st * jnp.exp2(bl) + jnp.where(head_mask, _dot_tn(vb, ke_state), 0.0)
    return o, st_new


def _hgrn_finish(o, og, g_ref, ones_ref):
    ms = _dot01_right(o * o, ones_ref[...]) * (1.0 / HEAD_DIM)
    return _bf(o * lax.rsqrt(ms + EPS) * g_ref[...] * _silu(og.astype(F32)))


def _hgrn_kernel(*refs, n_sub, n_blk, has_init):
    qf_ref, vf_ref, ff_ref, qb_ref, vb_ref, fb_ref, ogf_ref, ogb_ref, lb_ref, g_ref, ones_ref = refs[:11]
    n_in = 14 if has_init else 13
    st0_ref = refs[11] if has_init else None
    tsel_ref, pmask_ref = refs[n_in - 2], refs[n_in - 1]
    if n_blk == 1:
        af_ref, fin_ref, st_ref, b_scr, blk_scr = refs[n_in:]
    else:
        af_ref, fin_ref, st_ref, b_scr, blk_scr, keep_scr = refs[n_in:]
    c = HGRN_CHUNK
    step = pl.program_id(1)

    @pl.when(step == 0)
    def _():
        st_ref[...] = st0_ref[0] if has_init else jnp.zeros_like(st_ref)

    head_mask = _head_mask()
    st_f = st_ref[0]
    st_b = st_ref[1]
    for j in range(n_sub):
        rf = slice(j * c, (j + 1) * c)
        rb = slice((n_sub - 1 - j) * c, (n_sub - j) * c)
        o_f, st_f = _hgrn_chunk(qf_ref[rf, :], vf_ref[rf, :], ff_ref[rf, :], lb_ref[0:1, :], st_f,
                                tsel_ref[0], pmask_ref.at[0], False, head_mask, b_scr.at[2 * j])
        o_b, st_b = _hgrn_chunk(qb_ref[rb, :], vb_ref[rb, :], fb_ref[rb, :], lb_ref[1:2, :], st_b,
                                tsel_ref[1], pmask_ref.at[1], True, head_mask, b_scr.at[2 * j + 1])
        blk_scr[0, rf, :] = o_f
        blk_scr[1, rb, :] = o_b
    st_ref[0] = st_f
    st_ref[1] = st_b

    if n_blk == 1:
        af_ref[...] = _hgrn_finish(blk_scr[0] + blk_scr[1], ogf_ref[...], g_ref, ones_ref)
    else:
        half = n_blk // 2

        @pl.when(step < half)
        def _():
            keep_scr[0, step] = blk_scr[0]
            keep_scr[1, step] = blk_scr[1]

        @pl.when(step >= half)
        def _():
            other = n_blk - 1 - step
            af_ref[0, 0, 1] = _hgrn_finish(blk_scr[0] + keep_scr[1, other], ogf_ref[...], g_ref, ones_ref)
            af_ref[0, 0, 0] = _hgrn_finish(blk_scr[1] + keep_scr[0, other], ogb_ref[...], g_ref, ones_ref)

    @pl.when(step == n_blk - 1)
    def _():
        fin_ref[0, 0] = _collapse_heads(st_f, head_mask)
        fin_ref[0, 1] = _collapse_heads(st_b, head_mask)


def _hgrn_mixer(zf, zr, lb2, onorm_g, st0, batch, seqlen, n_sub):
    n = batch * seqlen
    tb = n_sub * HGRN_CHUNK
    n_blk = seqlen // tb
    assert n_blk == 1 or n_blk % 2 == 0
    half = n_blk // 2
    tsel, pmask = _hgrn_tables()
    has_init = st0 is not None
    fwd = lambda col: pl.BlockSpec((tb, BRANCH_W), lambda b, c: (b * n_blk + c, col))
    bwd = lambda col: pl.BlockSpec((tb, BRANCH_W), lambda b, c: (b * n_blk + n_blk - 1 - c, col))
    in_specs = [fwd(COL_HQ), fwd(COL_HI), fwd(COL_HFF), bwd(COL_HQ), bwd(COL_HI), bwd(COL_HFB), fwd(COL_HOG), bwd(COL_HOG),
                _const_spec((2, BRANCH_W)), _const_spec((1, BRANCH_W)), _const_spec((BRANCH_W, BRANCH_W))]
    args = [zf, zf, zf, zf, zf, zf, zr, zr, lb2, onorm_g.reshape(1, BRANCH_W), _group_ones()]
    if has_init:
        in_specs.append(pl.BlockSpec((1, 2, BRANCH_W, BRANCH_W), lambda b, c: (b, 0, 0, 0)))
        args.append(st0)
    in_specs += [_const_spec(tsel.shape), _const_spec(pmask.shape)]
    args += [tsel, pmask]
    scratch = [pltpu.VMEM((2, BRANCH_W, BRANCH_W), F32), pltpu.VMEM((2 * n_sub, HGRN_CHUNK, BRANCH_W), F32),
               pltpu.VMEM((2, tb, BRANCH_W), F32)]
    if n_blk == 1:
        out_specs = [pl.BlockSpec((tb, BRANCH_W), lambda b, c: (b, 0))]
        out_shape = [jax.ShapeDtypeStruct((n, BRANCH_W), BF16)]
    else:
        out_specs = [pl.BlockSpec((1, 1, 2, tb, BRANCH_W), lambda b, c: (b, jnp.maximum(c - half, 0), 0, 0, 0))]
        out_shape = [jax.ShapeDtypeStruct((batch, half, 2, tb, BRANCH_W), BF16)]
        scratch.append(pltpu.VMEM((2, half, tb, BRANCH_W), F32))
    out_specs.append(pl.BlockSpec((1, 2, HEAD_DIM, BRANCH_W), lambda b, c: (b, 0, 0, 0)))
    out_shape.append(jax.ShapeDtypeStruct((batch, 2, HEAD_DIM, BRANCH_W), F32))
    res = pl.pallas_call(
        functools.partial(_hgrn_kernel, n_sub=n_sub, n_blk=n_blk, has_init=has_init),
        grid=(batch, n_blk),
        in_specs=in_specs,
        out_specs=out_specs,
        out_shape=out_shape,
        scratch_shapes=scratch,
        compiler_params=_params("arbitrary", "arbitrary"),
        name="hgrn_mixer",
    )(*args)
    return res[0], n_blk, tb, res[1]


def _group_ones():
    g = np.arange(BRANCH_W) // HEAD_DIM
    return jnp.asarray((g[:, None] == g[None, :]).astype(np.float32), BF16)


def _ctx_attn_kernel(*refs, n_q, n_kv, has_sink):
    if has_sink:
        sink_ref, q_ref, k_ref, v_ref, o_ref = refs
    else:
        q_ref, k_ref, v_ref, o_ref = refs
    group = n_q // n_kv
    for slot, hq in enumerate(SWA_HEAD_ORDER if has_sink else range(n_q)):
        hk = hq // group
        q = q_ref[:, slot * HEAD_DIM:(slot + 1) * HEAD_DIM]
        k = k_ref[:, hk * HEAD_DIM:(hk + 1) * HEAD_DIM]
        v = v_ref[:, hk * HEAD_DIM:(hk + 1) * HEAD_DIM]
        s = _dot_nt(q, k)
        m = jnp.max(s, axis=-1, keepdims=True)
        if has_sink:
            sink = sink_ref[hq] * LOG2E
            m = jnp.maximum(m, sink)
        p = jnp.exp2(s - m)
        l = jnp.sum(p, axis=-1, keepdims=True)
        if has_sink:
            l = l + jnp.exp2(sink - m)
        o_ref[:, slot * HEAD_DIM:(slot + 1) * HEAD_DIM] = _bf(_dot(_bf(p), v) / l)


def _ctx_attn(zr, batch, seqlen, q_col, k_col, v_col, kv_width, n_kv, sink):
    n = batch * seqlen
    n_q = N_HEADS
    has_sink = sink is not None
    in_specs = [
        pl.BlockSpec((seqlen, BRANCH_W), lambda b: (b, q_col)),
        pl.BlockSpec((seqlen, kv_width), lambda b: (b, k_col)),
        pl.BlockSpec((seqlen, kv_width), lambda b: (b, v_col)),
    ]
    args = [zr, zr, zr]
    if has_sink:
        in_specs = [pl.BlockSpec(memory_space=pltpu.SMEM)] + in_specs
        args = [sink.astype(F32)] + args
    return pl.pallas_call(
        functools.partial(_ctx_attn_kernel, n_q=n_q, n_kv=n_kv, has_sink=has_sink),
        grid=(batch,),
        in_specs=in_specs,
        out_specs=pl.BlockSpec((seqlen, BRANCH_W), lambda b: (b, 0)),
        out_shape=jax.ShapeDtypeStruct((n, BRANCH_W), BF16),
        compiler_params=_params("arbitrary"),
        name="ctx_attn_sink" if has_sink else "ctx_attn",
    )(*args)


def _na_bias_kernel(rpb_ref, onehot_ref, mask_ref, o_ref):
    o_ref[...] = (_dot01_right(rpb_ref[...], onehot_ref[...]) + mask_ref[...]) * LOG2E


def _na_bias_tables(na_rpb):
    n_dr, n_dc = 2 * NA_ROWS - 1, 2 * NA_COLS - 1
    col = np.arange(GRID_W)
    col_start = np.clip(col - NA_COLS // 2, 0, GRID_W - NA_COLS)
    col_mask = (col[None, :] >= col_start[:, None]) & (col[None, :] < col_start[:, None] + NA_COLS)
    d_col = np.clip(col[None, :] - col[:, None], -(NA_COLS - 1), NA_COLS - 1) + (NA_COLS - 1)
    rows = DEPTH * N_HEADS * n_dr
    assert rows <= LANES and n_dc <= LANES
    onehot = (np.arange(LANES)[:, None] == d_col.reshape(1, -1)).astype(np.float32)
    mask_add = np.where(col_mask.reshape(1, -1), 0.0, -np.inf).astype(np.float32)
    rpb2 = jnp.zeros((LANES, LANES), F32).at[:rows, :n_dc].set(na_rpb.astype(F32).reshape(rows, n_dc))
    tab = pl.pallas_call(
        _na_bias_kernel,
        grid=(1,),
        in_specs=[_const_spec((LANES, LANES)), _const_spec((LANES, GRID_W * GRID_W)), _const_spec((1, GRID_W * GRID_W))],
        out_specs=_const_spec((LANES, GRID_W * GRID_W)),
        out_shape=jax.ShapeDtypeStruct((LANES, GRID_W * GRID_W), F32),
        compiler_params=_params("arbitrary"),
        name="na_bias",
    )(rpb2, jnp.asarray(onehot, BF16), jnp.asarray(mask_add))
    tab = tab[:rows].reshape(DEPTH, N_HEADS, n_dr, GRID_W, GRID_W)
    slabs = [jnp.transpose(tab[:, :, first:first + NA_ROWS], (0, 1, 3, 2, 4)).reshape(DEPTH, N_HEADS * GRID_W, NA_ROWS * GRID_W)
             for first in range(NA_ROWS)]
    return jnp.stack(slabs, axis=1)


def _na_lat_kernel(q_ref, k_ref, v_ref, kc_ref, vc_ref, bias_ref, o_ref, *, rows_per_step, n_rows):
    nk = NA_ROWS * GRID_W
    r0 = pl.program_id(1) * rows_per_step
    head_mask = _head_mask()

    def body(i, carry):
        r = r0 + i
        row_start = jnp.clip(r - NA_ROWS // 2, 0, n_rows - NA_ROWS)
        first = row_start - r + (NA_ROWS - 1)
        k0 = pl.multiple_of(row_start * GRID_W, GRID_W)
        q0 = pl.multiple_of(i * GRID_W, GRID_W)
        qx = _expand_heads(q_ref[pl.ds(q0, GRID_W), :], head_mask)
        s_lat = _dot_nt(qx, k_ref[pl.ds(k0, nk), :]) + bias_ref[first]
        s_ctx = _dot_nt(qx, kc_ref[0])
        m = jnp.maximum(jnp.max(s_lat, axis=-1, keepdims=True), jnp.max(s_ctx, axis=-1, keepdims=True))
        p_lat = jnp.exp2(s_lat - m)
        p_ctx = jnp.exp2(s_ctx - m)
        l = jnp.sum(p_lat, axis=-1, keepdims=True) + jnp.sum(p_ctx, axis=-1, keepdims=True)
        acc = _dot(_bf(p_lat), v_ref[pl.ds(k0, nk), :]) + _dot(_bf(p_ctx), vc_ref[0])
        o_ref[pl.ds(q0, GRID_W), :] = _bf(_collapse_heads(acc / l, head_mask))
        return carry

    lax.fori_loop(0, rows_per_step, body, 0, unroll=8)


def _na_latent(zr, kc, vc, bias_tab, batch, seqlen):
    n = batch * seqlen
    n_rows = seqlen // GRID_W
    assert n_rows >= NA_ROWS
    rows_per_step = 8
    steps = n_rows // rows_per_step
    tq = rows_per_step * GRID_W
    n_ctx = kc.shape[1]
    return pl.pallas_call(
        functools.partial(_na_lat_kernel, rows_per_step=rows_per_step, n_rows=n_rows),
        grid=(batch, steps),
        in_specs=[
            pl.BlockSpec((tq, BRANCH_W), lambda b, j: (b * steps + j, COL_NAQ)),
            pl.BlockSpec((seqlen, BRANCH_W), lambda b, j: (b, COL_NAK)),
            pl.BlockSpec((seqlen, BRANCH_W), lambda b, j: (b, COL_NAV)),
            pl.BlockSpec((1, n_ctx, BRANCH_W), lambda b, j: (b, 0, 0)),
            pl.BlockSpec((1, n_ctx, BRANCH_W), lambda b, j: (b, 0, 0)),
            _const_spec(bias_tab.shape),
        ],
        out_specs=pl.BlockSpec((tq, BRANCH_W), lambda b, j: (b * steps + j, 0)),
        out_shape=jax.ShapeDtypeStruct((n, BRANCH_W), BF16),
        compiler_params=_params("arbitrary", "arbitrary"),
        name="na_latent",
    )(zr, zr, zr, kc, vc, bias_tab)


def _rope_tables(seqlen):
    half = HEAD_DIM // 2
    t = np.arange(seqlen)
    rows = (t // GRID_W).astype(np.float32)
    cols = (t % GRID_W).astype(np.float32)
    inv = (1.0 / (np.float32(ROPE_THETA) ** (np.arange(0, half, 2, dtype=np.float32) / np.float32(half)))).astype(np.float32)
    ang_r = rows[:, None] * inv[None, :]
    ang_c = cols[:, None] * inv[None, :]
    cos = np.concatenate([np.cos(ang_r), np.cos(ang_r), np.cos(ang_c), np.cos(ang_c)], axis=-1)
    sin = np.concatenate([-np.sin(ang_r), np.sin(ang_r), -np.sin(ang_c), np.sin(ang_c)], axis=-1)
    cos = np.tile(cos.astype(np.float32), (1, N_HEADS))
    sin = np.tile(sin.astype(np.float32), (1, N_HEADS))
    return jnp.asarray(cos), jnp.asarray(sin)


def _rope(x, cos, sin_signed):
    w = x.shape[-1]
    lane = lax.broadcasted_iota(jnp.int32, x.shape, 1)
    partner = jnp.where((lane % 32) < 16, pltpu.roll(x, w - 16, 1), pltpu.roll(x, 16, 1))
    return x * cos + partner * sin_signed


def _swa_lat_kernel(sink_ref, q_ref, k_ref, v_ref, kc_ref, vc_ref, cos_ref, sin_ref, band_ref, o_ref, *, seqlen):
    for sub in range(SWA_BLOCKS_PER_STEP):
        rows = slice(sub * SWA_BLOCK, (sub + 1) * SWA_BLOCK)
        _swa_block(pl.program_id(1) * SWA_BLOCKS_PER_STEP + sub, sink_ref, q_ref.at[rows, :], k_ref, v_ref, kc_ref, vc_ref,
                   cos_ref, sin_ref, band_ref, o_ref.at[rows, :], seqlen)


def _swa_block(j, sink_ref, q_ref, k_ref, v_ref, kc_ref, vc_ref, cos_ref, sin_ref, band_ref, o_ref, seqlen):
    blk = SWA_BLOCK
    nwin = 3 * blk
    kvw = SWA_KV_HEADS * HEAD_DIM
    q0 = pl.multiple_of(j * blk, blk)
    k_blk = jnp.clip(j - 1, 0, seqlen // blk - 3)
    k0 = pl.multiple_of(k_blk * blk, blk)
    q = _rope(q_ref[...].astype(F32), cos_ref[pl.ds(q0, blk), :], sin_ref[pl.ds(q0, blk), :])
    kw = _rope(k_ref[pl.ds(k0, nwin), :].astype(F32), cos_ref[pl.ds(k0, nwin), 0:kvw], sin_ref[pl.ds(k0, nwin), 0:kvw])
    kw = _bf(kw)
    vw = v_ref[pl.ds(k0, nwin), :]
    n_slot = N_HEADS
    qb = _bf(q)
    lane_kv = lax.broadcasted_iota(jnp.int32, (blk, kvw), 1) // HEAD_DIM
    qx = jnp.concatenate(
        [jnp.where(lane_kv == (slot % SWA_KV_HEADS), qb[:, (slot // SWA_KV_HEADS) * kvw:(slot // SWA_KV_HEADS + 1) * kvw],
                   jnp.zeros((), BF16)) for slot in range(n_slot)], axis=0)
    row_slot = lax.broadcasted_iota(jnp.int32, (n_slot * blk, 1), 0) // blk
    sink = jnp.zeros((n_slot * blk, 1), F32)
    for slot in range(n_slot):
        sink = jnp.where(row_slot == slot, sink_ref[SWA_HEAD_ORDER[slot]] * LOG2E, sink)
    band = band_ref[j - k_blk]
    s_band = _dot_nt(qx, kw) + jnp.concatenate([band] * n_slot, axis=0)
    s_ctx = _dot_nt(qx, kc_ref[0])
    m = jnp.maximum(jnp.maximum(jnp.max(s_band, axis=-1, keepdims=True), jnp.max(s_ctx, axis=-1, keepdims=True)), sink)
    p_band = jnp.exp2(s_band - m)
    p_ctx = jnp.exp2(s_ctx - m)
    l = jnp.sum(p_band, axis=-1, keepdims=True) + jnp.sum(p_ctx, axis=-1, keepdims=True) + jnp.exp2(sink - m)
    acc = (_dot(_bf(p_band), vw) + _dot(_bf(p_ctx), vc_ref[0])) / l
    halves = []
    for half in range(n_slot // SWA_KV_HEADS):
        r0 = half * SWA_KV_HEADS * blk
        halves.append(jnp.where(lane_kv == 0, acc[r0:r0 + blk], acc[r0 + blk:r0 + 2 * blk]))
    o_ref[...] = _bf(jnp.concatenate(halves, axis=-1))


def _swa_latent(zr, kc, vc, sink, batch, seqlen):
    n = batch * seqlen
    nb = seqlen // SWA_BLOCK
    kvw = SWA_KV_HEADS * HEAD_DIM
    n_ctx = kc.shape[1]
    cos, sin = _rope_tables(seqlen)
    assert nb >= 3 and nb % SWA_BLOCKS_PER_STEP == 0
    steps = nb // SWA_BLOCKS_PER_STEP
    tq = SWA_BLOCKS_PER_STEP * SWA_BLOCK
    a = np.arange(SWA_BLOCK)[:, None]
    c = np.arange(3 * SWA_BLOCK)[None, :]
    band = jnp.asarray(np.stack([np.where(np.abs(c - a - off * SWA_BLOCK) <= SWA_WINDOW, 0.0, -np.inf) for off in range(3)])
                       .astype(np.float32))
    return pl.pallas_call(
        functools.partial(_swa_lat_kernel, seqlen=seqlen),
        grid=(batch, steps),
        in_specs=[
            pl.BlockSpec(memory_space=pltpu.SMEM),
            pl.BlockSpec((tq, BRANCH_W), lambda b, j: (b * steps + j, COL_SQ)),
            pl.BlockSpec((seqlen, kvw), lambda b, j: (b, COL_SK128)),
            pl.BlockSpec((seqlen, kvw), lambda b, j: (b, COL_SV128)),
            pl.BlockSpec((1, n_ctx, kvw), lambda b, j: (b, 0, 0)),
            pl.BlockSpec((1, n_ctx, kvw), lambda b, j: (b, 0, 0)),
            _const_spec(cos.shape),
            _const_spec(sin.shape),
            _const_spec(band.shape),
        ],
        out_specs=pl.BlockSpec((tq, BRANCH_W), lambda b, j: (b * steps + j, 0)),
        out_shape=jax.ShapeDtypeStruct((n, BRANCH_W), BF16),
        compiler_params=_params("arbitrary", "arbitrary"),
        name="swa_latent",
    )(sink.astype(F32), zr, zr, zr, kc, vc, cos, sin, band)


def _merge_kernel(a_ref, b_ref, c_ref, d_ref, g0_ref, g1_ref, g2_ref, g3_ref, x_ref, gate_ref, sh_ref, sc_ref,
                  ng_ref, wb_ref, wo_ref, wr_ref, br_ref, tri_ref, upper_ref, x1_ref, h_ref, route_ref, seg_ref):
    tm = x_ref.shape[0]
    mix = None
    for br, gt, i in ((a_ref, g0_ref, 0), (b_ref, g1_ref, 1), (c_ref, g2_ref, 2), (d_ref, g3_ref, 3)):
        t = (1.0 + jnp.tanh(gt[...].astype(F32))) * _dot(br[...], wb_ref[i])
        mix = t if mix is None else mix + t
    x1 = x_ref[...] + gate_ref[0] * _dot(_bf(mix), wo_ref[...])
    x1_ref[...] = x1
    ms = jnp.mean(x1 * x1, axis=-1, keepdims=True)
    h = x1 * lax.rsqrt(ms + EPS) * ng_ref[...]
    h = h * (1.0 + sc_ref[0]) + sh_ref[0]
    h_ref[...] = _bf(h)

    hh = _bf(h)
    hm = _bf(h - hh.astype(F32))
    logits = (_dot(hh, wr_ref[0]) + _dot(hm, wr_ref[0]) + _dot(hh, wr_ref[1])) + br_ref[...]
    lane_i = lax.broadcasted_iota(jnp.int32, logits.shape, 1)
    lane = lane_i.astype(F32)
    lane_grp = (lane_i // EXPERTS_PER_GROUP).astype(F32)
    neg = -jnp.inf
    far = float(4 * N_EXPERTS)
    is_g = (lane_i >= N_EXPERTS) & (lane_i < N_EXPERTS + N_GROUPS)
    gl = jnp.where(is_g, logits, neg)
    gmax = jnp.max(gl, axis=-1, keepdims=True)
    gsum = jnp.sum(jnp.exp(gl - gmax), axis=-1, keepdims=True)
    g_top_p = 1.0 / gsum
    g_idx = jnp.min(jnp.where(is_g & (gl == gmax), lane, far), axis=-1, keepdims=True) - float(N_EXPERTS)
    in_grp = (lane_i < N_EXPERTS) & (lane_grp == g_idx)
    e_l = jnp.where(in_grp, logits, neg)
    e1 = jnp.max(e_l, axis=-1, keepdims=True)
    i1 = jnp.min(jnp.where(in_grp & (e_l == e1), lane, far), axis=-1, keepdims=True)
    e_l2 = jnp.where(lane == i1, neg, e_l)
    e2 = jnp.max(e_l2, axis=-1, keepdims=True)
    i2 = jnp.min(jnp.where(in_grp & (lane != i1) & (e_l2 == e2), lane, far), axis=-1, keepdims=True)
    t2 = jnp.exp(e2 - e1)
    w1 = g_top_p / (1.0 + t2)
    w2 = w1 * t2

    sel = (lane == i1) | (lane == i2)
    sel_f = jnp.where(sel, 1.0, 0.0)
    cum = _dot(tri_ref[...], _bf(sel_f))
    counts = cum[tm - 1:tm, :]
    padded = jnp.floor((counts + (MOE_ROW_ALIGN - 1)) * (1.0 / MOE_ROW_ALIGN)) * MOE_ROW_ALIGN
    seg_start = _dot(_bf(jnp.broadcast_to(padded, (SUBLANES, LANES))), upper_ref[...])[0:1, :]
    slot = seg_start + cum - sel_f
    pos1 = jnp.sum(jnp.where(lane == i1, slot, 0.0), axis=-1, keepdims=True)
    pos2 = jnp.sum(jnp.where(lane == i2, slot, 0.0), axis=-1, keepdims=True)
    route_ref[...] = jnp.where(lane_i == 0, pos1, jnp.where(lane_i == 1, pos2, jnp.where(lane_i == 2, w1,
                               jnp.where(lane_i == 3, w2, 0.0))))
    seg = jnp.where(lane_i[0:1] < N_EXPERTS, seg_start, pltpu.roll(jnp.broadcast_to(padded, (SUBLANES, LANES)), N_EXPERTS, 1)[0:1])
    seg_ref[0] = jnp.where(lane_i[0:1] < 2 * N_EXPERTS, seg, 0.0).astype(jnp.int32)


def _router_tables(w_rg, b_rg, w_re, b_re):
    w = jnp.zeros((D_MODEL, LANES), F32)
    w = w.at[:, :N_EXPERTS].set(w_re.astype(F32)).at[:, N_EXPERTS:N_EXPERTS + N_GROUPS].set(w_rg.astype(F32))
    b = jnp.zeros((1, LANES), F32)
    b = b.at[0, :N_EXPERTS].set(b_re.astype(F32)).at[0, N_EXPERTS:N_EXPERTS + N_GROUPS].set(b_rg.astype(F32))
    return jnp.stack(_split3(w)[:2], axis=0), b


def _merge(a, hgrn_blocks, hgrn_block_rows, branches, zr, x, mods3, mod_row, norm_g, w_branch_bf, w_out_bf, wr3, br,
           layer):
    n = x.shape[0]
    half = hgrn_blocks // 2
    if hgrn_blocks > 1:
        assert hgrn_block_rows == MOE_TILE

        def pair_index(i):
            k = i % hgrn_blocks
            upper = k >= half
            return (i // hgrn_blocks, jnp.where(upper, k - half, half - 1 - k), jnp.where(upper, 1, 0), 0, 0)

        a_spec = pl.BlockSpec((None, None, None, MOE_TILE, BRANCH_W), pair_index)
    else:
        a_spec = pl.BlockSpec((MOE_TILE, BRANCH_W), lambda i: (i, 0))
    per_layer = lambda shape: pl.BlockSpec((None,) + shape, lambda i: (layer,) + (0,) * len(shape))
    tm = MOE_TILE
    t = np.arange(tm)
    tri = jnp.asarray((t[None, :] <= t[:, None]).astype(np.float32), BF16)
    e = np.arange(LANES)
    upper = jnp.asarray((e[:, None] < e[None, :]).astype(np.float32), BF16)
    row = lambda w: pl.BlockSpec((tm, w), lambda i: (i, 0))
    gate = lambda k: pl.BlockSpec((tm, D_MODEL), lambda i: (i, COL_GATES1024 + k))
    mod = lambda k: pl.BlockSpec((1, 1, D_MODEL), lambda i: (mod_row(i, tm), 0, k))
    return pl.pallas_call(
        _merge_kernel,
        grid=(n // tm,),
        in_specs=[
            a_spec, row(BRANCH_W), row(BRANCH_W), row(BRANCH_W),
            gate(0), gate(1), gate(2), gate(3),
            row(D_MODEL),
            mod(2), mod(3), mod(4),
            _const_spec((1, D_MODEL)),
            per_layer((N_BRANCH, BRANCH_W, D_MODEL)),
            per_layer((D_MODEL, D_MODEL)),
            per_layer((2, D_MODEL, LANES)),
            per_layer((1, LANES)),
            _const_spec((tm, tm)),
            _const_spec((LANES, LANES)),
        ],
        out_specs=[row(D_MODEL), row(D_MODEL), row(LANES), pl.BlockSpec((1, 1, LANES), lambda i: (i, 0, 0))],
        out_shape=[
            jax.ShapeDtypeStruct((n, D_MODEL), F32),
            jax.ShapeDtypeStruct((n, D_MODEL), BF16),
            jax.ShapeDtypeStruct((n, LANES), F32),
            jax.ShapeDtypeStruct((n // tm, 1, LANES), jnp.int32),
        ],
        compiler_params=_params("arbitrary"),
        name="merge",
    )(a, *branches, zr, zr, zr, zr, x, mods3, mods3, mods3, norm_g.reshape(1, D_MODEL), w_branch_bf, w_out_bf, wr3, br,
      tri, upper)


def _moe_kernel(seg_ref, h_ref, route_ref, x1_ref, gate_ref, fg_ref, wg_ref, wu_ref, wd_ref, *outs_and_scratch, final):
    if final:
        x2_ref, y_ref, hs_scr, ys_scr = outs_and_scratch
    else:
        x2_ref, hs_scr, ys_scr = outs_and_scratch
    tm = MOE_TILE
    tile = pl.program_id(0)
    route = route_ref[...]
    route_t = route.T
    pos1_row, pos2_row = route_t[0:1, :], route_t[1:2, :]
    h = h_ref[...]
    for rb in range(MOE_ROWS // MOE_GATHER_BLK):
        r = (rb * MOE_GATHER_BLK + lax.broadcasted_iota(jnp.int32, (MOE_GATHER_BLK, tm), 0)).astype(F32)
        p = jnp.where((r == pos1_row) | (r == pos2_row), 1.0, 0.0)
        hs_scr[rb * MOE_GATHER_BLK:(rb + 1) * MOE_GATHER_BLK, :] = _bf(_dot(_bf(p), h))
    ys_scr[...] = jnp.zeros_like(ys_scr)

    row_in_chunk = lax.broadcasted_iota(jnp.int32, (MOE_CHUNK, D_MODEL), 0)

    def expert_chunk(e, r0, end):
        r0 = pl.multiple_of(r0, MOE_ROW_ALIGN)
        rows = hs_scr[pl.ds(r0, MOE_CHUNK), :]
        a = _silu(_dot(rows, wg_ref[e])) * _dot(rows, wu_ref[e])
        y = _dot(_bf(a), wd_ref[e])
        ys_scr[pl.ds(r0, MOE_CHUNK), :] = jnp.where(row_in_chunk < end - r0, _bf(y), ys_scr[pl.ds(r0, MOE_CHUNK), :])

    starts = [seg_ref[tile * LANES + e] for e in range(N_EXPERTS)]
    ends = [starts[e] + seg_ref[tile * LANES + N_EXPERTS + e] for e in range(N_EXPERTS)]
    for e in range(N_EXPERTS):
        expert_chunk(e, starts[e], ends[e])
    for e in range(N_EXPERTS):
        n_chunks = lax.div(ends[e] - starts[e] + (MOE_CHUNK - 1), MOE_CHUNK)

        def more(c, carry, e=e):
            expert_chunk(e, starts[e] + c * MOE_CHUNK, ends[e])
            return carry

        lax.fori_loop(1, n_chunks, more, 0)

    pos1, pos2, w1, w2 = route[:, 0:1], route[:, 1:2], route[:, 2:3], route[:, 3:4]
    acc = None
    for cb in range(MOE_SEG_ROWS // MOE_SCATTER_BLK):
        r = (cb * MOE_SCATTER_BLK + lax.broadcasted_iota(jnp.int32, (tm, MOE_SCATTER_BLK), 1)).astype(F32)
        q = jnp.where(r == pos1, w1, jnp.where(r == pos2, w2, 0.0))
        part = _dot(_bf(q), ys_scr[cb * MOE_SCATTER_BLK:(cb + 1) * MOE_SCATTER_BLK, :])
        acc = part if acc is None else acc + part
    x2 = x1_ref[...] + gate_ref[0] * acc
    x2_ref[...] = x2
    if final:
        ms = jnp.mean(x2 * x2, axis=-1, keepdims=True)
        y_ref[...] = x2 * lax.rsqrt(ms + EPS) * fg_ref[...]


def _moe(h, route, seg, x1, mods3, mod_row, final_g, wg_bf, wu_bf, wd_bf, layer, final):
    n = x1.shape[0]
    tm = MOE_TILE
    row = lambda w: pl.BlockSpec((tm, w), lambda i, s: (i, 0))
    resident = lambda shape: pl.BlockSpec((None,) + shape, lambda i, s: (layer,) + (0,) * len(shape),
                                          pipeline_mode=pl.Buffered(1))
    out_specs = [row(D_MODEL)]
    out_shape = [jax.ShapeDtypeStruct((n, D_MODEL), F32)]
    if final:
        out_specs.append(row(D_MODEL))
        out_shape.append(jax.ShapeDtypeStruct((n, D_MODEL), F32))
    return pl.pallas_call(
        functools.partial(_moe_kernel, final=final),
        grid_spec=pltpu.PrefetchScalarGridSpec(
            num_scalar_prefetch=1,
            grid=(n // tm,),
            in_specs=[
                row(D_MODEL), row(LANES), row(D_MODEL),
                pl.BlockSpec((1, 1, D_MODEL), lambda i, s: (mod_row(i, tm), 0, 5)),
                pl.BlockSpec((1, D_MODEL), lambda i, s: (0, 0)),
                resident((N_EXPERTS, D_MODEL, EXPERT_FF)),
                resident((N_EXPERTS, D_MODEL, EXPERT_FF)),
                resident((N_EXPERTS, EXPERT_FF, D_MODEL)),
            ],
            out_specs=out_specs,
            scratch_shapes=[pltpu.VMEM((MOE_ROWS, D_MODEL), BF16), pltpu.VMEM((MOE_ROWS, D_MODEL), BF16)],
        ),
        out_shape=out_shape,
        compiler_params=_params("arbitrary"),
        name="moe_final" if final else "moe",
    )(seg.reshape(-1), h, route, x1, mods3, final_g.reshape(1, D_MODEL), wg_bf, wu_bf, wd_bf)


def _state_to_blockdiag_t(s):
    b = s.shape[0]
    st = jnp.swapaxes(s.astype(F32), -1, -2)
    eye = jnp.eye(N_HEADS, dtype=F32)
    full = st[:, :, :, :, None, :] * eye[None, None, :, None, :, None]
    return full.reshape(b, 2, BRANCH_W, BRANCH_W)


def _compact_to_state(fin):
    b = fin.shape[0]
    return jnp.transpose(fin.reshape(b, 2, HEAD_DIM, N_HEADS, HEAD_DIM), (0, 1, 3, 4, 2))


def _layer(x, l, w, mods3, mod_row, batch, seqlen, latent, st0, caches, final):
    proj = _in_proj(x, mods3, mod_row, w["norm1_g"][l], w["w_in"], w["smlp_ws"][l], w["smlp_b"][l], l, IN_TILE, not latent)
    zf, zr, c_out = proj[0], proj[1], proj[2]
    kv = None if latent else proj[3:]
    a_parts, hgrn_blocks, hgrn_block_rows, fin = _hgrn_mixer(
        zf, zr, w["lb"][l], w["onorm_g"][l], st0, batch, seqlen, n_sub=min(HGRN_CHUNKS_PER_STEP, seqlen // HGRN_CHUNK))
    if latent:
        ck_na, cv_na, ck_swa, cv_swa = caches
        b_out = _na_latent(zr, ck_na, cv_na, w["na_bias"][l], batch, seqlen)
        d_out = _swa_latent(zr, ck_swa, cv_swa, w["swa_sink"][l], batch, seqlen)
    else:
        b_out = _ctx_attn(zr, batch, seqlen, COL_NAQ, COL_NAK, COL_NAV, BRANCH_W, N_HEADS, None)
        d_out = _ctx_attn(zr, batch, seqlen, COL_SQ, COL_SK128, COL_SV128, SWA_KV_HEADS * HEAD_DIM, SWA_KV_HEADS,
                          w["swa_sink"][l])
    x1, h2, route, seg = _merge(a_parts, hgrn_blocks, hgrn_block_rows, (b_out, c_out, d_out), zr, x, mods3, mod_row,
                                w["norm2_g"][l], w["w_branch"], w["w_out"], w["wr"], w["br"], l)
    out = _moe(h2, route, seg, x1, mods3, mod_row, w["final_g"], w["wg"], w["wu"], w["wd"], l, final)
    return out, kv, fin


def kernel(x_prompt, x_sample, c, cache_na_k, cache_na_v, cache_swa_k, cache_swa_v, state_hgrn, c_ctx, w_ada, b_ada, norm1_g, norm2_g, w_in, hgrn_lb, hgrn_onorm_g, na_rpb, smlp_ws, smlp_b, swa_sink, w_branch, w_out, router_g_w, router_g_b, router_e_w, router_e_b, moe_w_gate, moe_w_up, moe_w_down, final_g):
    bc, lc, _ = x_prompt.shape
    bl, ll, _ = x_sample.shape
    n_ctx_tok = bc * lc

    cond = jnp.zeros((MOD_ROWS, D_MODEL), F32).at[0].set(c_ctx.astype(F32)).at[1:1 + bl].set(c.astype(F32))
    mods = _ada_mods(cond, w_ada, b_ada)

    lb_soft = jax.nn.softmax(hgrn_lb.astype(F32), axis=0)
    lb_all = jnp.cumsum(lb_soft, axis=0) - lb_soft[0:1]

    col_scale = np.ones((P_IN,), np.float32)
    col_scale[ZF_W + COL_GATES1024 * 1024:] = 0.5
    for q_col in (COL_NAQ, COL_SQ):
        col_scale[ZF_W + q_col * BRANCH_W:ZF_W + (q_col + 1) * BRANCH_W] = QK_PRESCALE
    swa_rows = jnp.concatenate([w_branch[:, N_BRANCH - 1, h * HEAD_DIM:(h + 1) * HEAD_DIM] for h in SWA_HEAD_ORDER], axis=1)
    routers = [_router_tables(router_g_w[l], router_g_b[l], router_e_w[l], router_e_b[l]) for l in range(DEPTH)]
    w = dict(
        norm1_g=norm1_g, norm2_g=norm2_g, lb=lb_all, onorm_g=hgrn_onorm_g, smlp_ws=smlp_ws, smlp_b=smlp_b,
        swa_sink=swa_sink, final_g=final_g,
        w_in=_bf(w_in * jnp.asarray(col_scale)[None, None, :]),
        w_branch=_bf(w_branch.at[:, N_BRANCH - 1].set(swa_rows)),
        w_out=_bf(0.5 * w_out),
        wr=jnp.stack([r[0] for r in routers]), br=jnp.stack([r[1] for r in routers]),
        wg=_bf(moe_w_gate), wu=_bf(moe_w_up), wd=_bf(moe_w_down),
        na_bias=_na_bias_tables(na_rpb),
    )

    ctx_row = lambda i, tm: 0
    lat_row = lambda i, tm: 1 + i // (ll // tm)
    xp = x_prompt.reshape(n_ctx_tok, D_MODEL)
    kvs, states = [], []
    y_prompt = None
    for l in range(DEPTH):
        final = l == DEPTH - 1
        out, kv, fin = _layer(xp, l, w, mods[l].reshape(MOD_ROWS, 1, -1), ctx_row, bc, lc, False, None, None, final)
        if final:
            xp, y_prompt = out
        else:
            xp = out[0]
        kvs.append(kv)
        states.append(_compact_to_state(fin))
    cache_out = [jnp.stack([kvs[l][k].reshape(bc, lc, -1, HEAD_DIM) for l in range(DEPTH)], axis=1) for k in range(4)]

    xs = x_sample.reshape(bl * ll, D_MODEL)
    y_sample = None
    n_past = cache_na_k.shape[2]
    for l in range(DEPTH):
        caches = (_bf(cache_na_k[:, l]).reshape(bl, n_past, BRANCH_W), _bf(cache_na_v[:, l]).reshape(bl, n_past, BRANCH_W),
                  _bf(cache_swa_k[:, l]).reshape(bl, n_past, SWA_KV_HEADS * HEAD_DIM),
                  _bf(cache_swa_v[:, l]).reshape(bl, n_past, SWA_KV_HEADS * HEAD_DIM))
        final = l == DEPTH - 1
        out, _, _ = _layer(xs, l, w, mods[l].reshape(MOD_ROWS, 1, -1), lat_row, bl, ll, True,
                           _state_to_blockdiag_t(state_hgrn[:, l]), caches, final)
        if final:
            xs, y_sample = out
        else:
            xs = out[0]

    return (y_prompt.reshape(bc, lc, D_MODEL), y_sample.reshape(bl, ll, D_MODEL), *cache_out, jnp.stack(states, axis=1))
```

```python
import functools

import numpy as np
import jax
import jax.numpy as jnp
from jax import lax
from jax.experimental import pallas as pl
from jax.experimental.pallas import tpu as pltpu

D_MODEL = 1024
DEPTH = 2
GRID_W = 64
HEAD_DIM = 64
N_BRANCH = 4
BRANCH_W = 256
N_HEADS = 4
HGRN_CHUNK = 64
HGRN_CHUNKS_PER_STEP = 8
NA_ROWS = 8
NA_COLS = 16
SMLP_GROUPS = 4
SMLP_CHUNK = 128
SWA_KV_HEADS = 2
SWA_HEAD_ORDER = (0, 2, 1, 3)
SWA_WINDOW = 128
SWA_BLOCK = 128
SWA_BLOCKS_PER_STEP = 4
ROPE_THETA = 10000.0
N_GROUPS = 4
EXPERTS_PER_GROUP = 4
N_EXPERTS = 16
EXPERT_FF = 256
ADA_CHUNKS = 6
EPS = 1e-6
TINY = 1e-30
P_IN = 7168
ATT_SCALE = HEAD_DIM ** -0.5
LOG2E = 1.4426950408889634
QK_PRESCALE = ATT_SCALE * LOG2E

ZF_W = 1024
ZR_W = P_IN - ZF_W
COL_HQ, COL_HI, COL_HFF, COL_HFB = 0, 1, 2, 3
COL_HOG, COL_NAQ, COL_NAK, COL_NAV = 0, 1, 2, 3
COL_MU, COL_MV, COL_SQ = 4, 5, 6
COL_SK128, COL_SV128 = 14, 15
COL_GATES1024 = 2
IN_COL_CHUNK = 1024
IN_TILE = 512

MOD_ROWS = 16
VMEM_LIMIT = 56 * 1024 * 1024

F32 = jnp.float32
BF16 = jnp.bfloat16

MOE_TILE = 512
MOE_ROW_ALIGN = 16
MOE_CHUNK = 96
MOE_GATHER_BLK = 128
MOE_SCATTER_BLK = 256
MOE_SEG_ROWS = -(-(2 * MOE_TILE + N_EXPERTS * (MOE_ROW_ALIGN - 1)) // MOE_SCATTER_BLK) * MOE_SCATTER_BLK
MOE_ROWS = -(-(2 * MOE_TILE + N_EXPERTS * (MOE_ROW_ALIGN - 1) + MOE_CHUNK) // MOE_GATHER_BLK) * MOE_GATHER_BLK
HGRN_LEVELS = (0, 1, 2, 4, 8, 16, 32)
SUBLANES = 8
LANES = 128
HGRN_MXU_REF_LEVELS = ()


def _bf(x):
    return x.astype(BF16)


def _dot(a, b):
    return jnp.dot(a, b, preferred_element_type=F32)


def _dot_nt(a, b):
    return lax.dot_general(a, b, (((1,), (1,)), ((), ())), preferred_element_type=F32)


def _dot_tn(a, b):
    return lax.dot_general(a, b, (((0,), (0,)), ((), ())), preferred_element_type=F32)


def _split3(x):
    hi = _bf(x)
    r1 = x - hi.astype(F32)
    mid = _bf(r1)
    lo = _bf(r1 - mid.astype(F32))
    return hi, mid, lo


def _dot01_left(m01, x):
    hi, mid, lo = _split3(x)
    return _dot(m01, hi) + _dot(m01, mid) + _dot(m01, lo)


def _dot01_right(x, m01):
    hi, mid, lo = _split3(x)
    return _dot(hi, m01) + _dot(mid, m01) + _dot(lo, m01)


def _sigmoid(x):
    return 0.5 * jnp.tanh(0.5 * x) + 0.5


def _silu(x):
    return x * _sigmoid(x)


def _params(*sem):
    return pltpu.CompilerParams(dimension_semantics=sem, vmem_limit_bytes=VMEM_LIMIT)


def _const_spec(shape):
    n = len(shape)
    return pl.BlockSpec(shape, lambda *_: (0,) * n)


def _head_mask():
    row = lax.broadcasted_iota(jnp.int32, (N_HEADS * HEAD_DIM, BRANCH_W), 0)
    lane = lax.broadcasted_iota(jnp.int32, (N_HEADS * HEAD_DIM, BRANCH_W), 1)
    return (row // HEAD_DIM) == (lane // HEAD_DIM)


def _expand_heads(x, head_mask):
    return jnp.where(head_mask, jnp.concatenate([x] * N_HEADS, axis=0), jnp.zeros((), x.dtype))


def _collapse_heads(r, head_mask):
    r = jnp.where(head_mask, r, 0.0)
    n = HEAD_DIM
    return (r[0:n] + r[n:2 * n]) + (r[2 * n:3 * n] + r[3 * n:4 * n])


def _ada_kernel(cond_ref, w_ref, b_ref, o_ref):
    s = _silu(cond_ref[...])
    o_ref[0] = _dot(_bf(s), _bf(w_ref[0])) + b_ref[0]


def _ada_mods(cond, w_ada, b_ada):
    tn = 1536
    n = ADA_CHUNKS * D_MODEL
    return pl.pallas_call(
        _ada_kernel,
        grid=(DEPTH, n // tn),
        in_specs=[
            pl.BlockSpec((MOD_ROWS, D_MODEL), lambda l, j: (0, 0)),
            pl.BlockSpec((1, D_MODEL, tn), lambda l, j: (l, 0, j)),
            pl.BlockSpec((1, 1, tn), lambda l, j: (l, 0, j)),
        ],
        out_specs=pl.BlockSpec((1, MOD_ROWS, tn), lambda l, j: (l, 0, j)),
        out_shape=jax.ShapeDtypeStruct((DEPTH, MOD_ROWS, n), F32),
        compiler_params=_params("arbitrary", "arbitrary"),
        name="ada_mods",
    )(cond, w_ada, b_ada.reshape(DEPTH, 1, n))


def _smlp_tile(u, v, ws_ref, bias_ref, ones_ref):
    ms = _dot01_right(v * v, ones_ref[...]) * (1.0 / HEAD_DIM)
    vn = _bf(v * lax.rsqrt(ms + EPS))
    lane_g = lax.broadcasted_iota(jnp.int32, (SMLP_CHUNK, BRANCH_W), 1) // HEAD_DIM
    outs = []
    for ci in range(u.shape[0] // SMLP_CHUNK):
        rows = slice(ci * SMLP_CHUNK, (ci + 1) * SMLP_CHUNK)
        mixed = bias_ref[...]
        for g in range(SMLP_GROUPS):
            mixed = mixed + jnp.where(lane_g == g, _dot(ws_ref[g], vn[rows]), 0.0)
        outs.append(u[rows] * mixed)
    return jnp.concatenate(outs, axis=0)


def _in_kernel(x_ref, sh_ref, sc_ref, g_ref, w_ref, ws_ref, bias_ref, ones_ref, zf_ref, zr_ref, c_ref, *rest, want_kv):
    x = x_ref[...]
    ms = jnp.mean(x * x, axis=-1, keepdims=True)
    h = x * lax.rsqrt(ms + EPS) * g_ref[...]
    h = _bf(h * (1.0 + sc_ref[0]) + sh_ref[0])
    tn = IN_COL_CHUNK
    kvw = SWA_KV_HEADS * HEAD_DIM
    for j in range(P_IN // tn):
        acc = _dot(h, w_ref[:, j * tn:(j + 1) * tn])
        if j == 0:
            zf_ref[...] = acc
            continue
        c0 = (j - 1) * tn
        sq0 = COL_SQ * BRANCH_W - c0
        if 0 <= sq0 < tn:
            heads = [acc[:, sq0 + hd * HEAD_DIM:sq0 + (hd + 1) * HEAD_DIM] for hd in SWA_HEAD_ORDER]
            zr_ref[:, c0:c0 + tn] = _bf(jnp.concatenate([acc[:, :sq0]] + heads + [acc[:, sq0 + BRANCH_W:]], axis=1))
        else:
            zr_ref[:, c0:c0 + tn] = _bf(acc)
        mu0 = COL_MU * BRANCH_W - c0
        if 0 <= mu0 and mu0 + 2 * BRANCH_W <= tn:
            assert COL_MV == COL_MU + 1
            c_ref[...] = _bf(_smlp_tile(acc[:, mu0:mu0 + BRANCH_W], acc[:, mu0 + BRANCH_W:mu0 + 2 * BRANCH_W],
                                        ws_ref, bias_ref, ones_ref))
        if want_kv:
            nak_ref, nav_ref, sk_ref, sv_ref = rest
            for ref, col, width in ((nak_ref, COL_NAK * BRANCH_W, BRANCH_W), (nav_ref, COL_NAV * BRANCH_W, BRANCH_W),
                                    (sk_ref, COL_SK128 * kvw, kvw), (sv_ref, COL_SV128 * kvw, kvw)):
                if c0 <= col < c0 + tn:
                    ref[...] = acc[:, col - c0:col - c0 + width]


def _in_proj(x, mods3, mod_row, norm_g, w_in_bf, smlp_ws, smlp_b, layer, tm, want_kv):
    n = x.shape[0]
    assert ZF_W == IN_COL_CHUNK and tm % SMLP_CHUNK == 0
    kvw = SWA_KV_HEADS * HEAD_DIM
    bias = jnp.repeat(smlp_b.astype(F32).T, BRANCH_W // SMLP_GROUPS, axis=1)
    row = lambda w: pl.BlockSpec((tm, w), lambda i: (i, 0))
    out_specs = [row(ZF_W), row(ZR_W), row(BRANCH_W)]
    out_shape = [jax.ShapeDtypeStruct((n, ZF_W), F32), jax.ShapeDtypeStruct((n, ZR_W), BF16),
                 jax.ShapeDtypeStruct((n, BRANCH_W), BF16)]
    if want_kv:
        for width in (BRANCH_W, BRANCH_W, kvw, kvw):
            out_specs.append(row(width))
            out_shape.append(jax.ShapeDtypeStruct((n, width), F32))
    return pl.pallas_call(
        functools.partial(_in_kernel, want_kv=want_kv),
        grid=(n // tm,),
        in_specs=[
            row(D_MODEL),
            pl.BlockSpec((1, 1, D_MODEL), lambda i: (mod_row(i, tm), 0, 0)),
            pl.BlockSpec((1, 1, D_MODEL), lambda i: (mod_row(i, tm), 0, 1)),
            _const_spec((1, D_MODEL)),
            pl.BlockSpec((None, D_MODEL, P_IN), lambda i: (layer, 0, 0), pipeline_mode=pl.Buffered(1)),
            _const_spec((SMLP_GROUPS, SMLP_CHUNK, SMLP_CHUNK)),
            _const_spec((SMLP_CHUNK, BRANCH_W)),
            _const_spec((BRANCH_W, BRANCH_W)),
        ],
        out_specs=out_specs,
        out_shape=out_shape,
        compiler_params=_params("arbitrary"),
        name="in_proj_kv" if want_kv else "in_proj",
    )(x, mods3, mods3, norm_g.reshape(1, D_MODEL), w_in_bf, _bf(smlp_ws), bias, _group_ones())


def _hgrn_tables():
    c = HGRN_CHUNK
    t = np.arange(c)
    tsel = np.zeros((2, (1 + len(HGRN_MXU_REF_LEVELS)) * c, c), np.float32)
    pmask = np.zeros((2, len(HGRN_LEVELS), c, N_HEADS * c), np.float32)
    for rev in (0, 1):
        cum = (t[None, :] >= t[:, None]) if rev else (t[None, :] <= t[:, None])
        tsel[rev, :c] = cum
        pmask[rev, 0] = np.tile(np.eye(c, dtype=np.float32), (1, N_HEADS))
        for li, m in enumerate(HGRN_LEVELS[1:], start=1):
            if m in HGRN_MXU_REF_LEVELS:
                slot = 1 + HGRN_MXU_REF_LEVELS.index(m)
                tsel[rev, slot * c:(slot + 1) * c] = cum[(t // (2 * m)) * (2 * m) + (m - 1 if rev else m)]
            same = (t[:, None] // (2 * m)) == (t[None, :] // (2 * m))
            q_half = ((t & m) == 0) if rev else ((t & m) != 0)
            k_half = ~q_half
            pmask[rev, li] = np.tile((same & q_half[:, None] & k_half[None, :]).astype(np.float32), (1, N_HEADS))
    return jnp.asarray(tsel, BF16), jnp.asarray(pmask, F32)


def _ref_rows(b_ref, m, rev):
    c = HGRN_CHUNK
    off = (m - 1) if rev else m
    row = lambda r, n: jnp.broadcast_to(b_ref[pl.ds(r, 1), :], (n, BRANCH_W))
    if 2 * m >= SUBLANES:
        return jnp.concatenate([row(s + off, 2 * m) for s in range(0, c, 2 * m)], axis=0)
    sub = lax.broadcasted_iota(jnp.int32, (SUBLANES, BRANCH_W), 0)
    tiles = []
    for t0 in range(0, c, SUBLANES):
        cur = row(t0 + off, SUBLANES)
        for s in range(2 * m, SUBLANES, 2 * m):
            cur = jnp.where(sub >= s, row(t0 + s + off, SUBLANES), cur)
        tiles.append(cur)
    return jnp.concatenate(tiles, axis=0)


def _hgrn_chunk(q_raw, v, f_raw, lb, st, tsel, pmask_ref, rev, head_mask, b_ref):
    c = HGRN_CHUNK
    qq = _silu(q_raw)
    f = lb + (1.0 - lb) * _sigmoid(f_raw)
    lf = jnp.log2(jnp.maximum(f, TINY))
    k = 1.0 - f
    ball = _dot01_left(tsel, lf)
    b = ball[:c]
    b_ref[...] = b
    bl = b[0:1] if rev else b[c - 1:c]
    vb = _bf(v)
    v_x = _expand_heads(vb, head_mask)

    o = _dot_nt(_bf(qq * jnp.exp2(b)), _bf(st))

    p = None
    qb, kb = _bf(qq), _bf(k)
    for li, m in enumerate(HGRN_LEVELS):
        if m == 0:
            qe, ke = qb, kb
        else:
            if m in HGRN_MXU_REF_LEVELS:
                slot = 1 + HGRN_MXU_REF_LEVELS.index(m)
                ref = ball[slot * c:(slot + 1) * c]
            else:
                ref = _ref_rows(b_ref, m, rev)
            e = _bf(jnp.exp2(-jnp.abs(b - ref)))
            qe, ke = qb * e, kb * e
        s = _dot_nt(qe, _expand_heads(ke, head_mask)) * pmask_ref[li]
        p = s if p is None else p + s
    o = o + _dot(_bf(p), v_x)

    ke_state = _bf(k * jnp.exp2(bl - b))
    st_new = st * jnp.exp2(bl) + jnp.where(head_mask, _dot_tn(vb, ke_state), 0.0)
    return o, st_new


def _hgrn_finish(o, og, g_ref, ones_ref):
    ms = _dot01_right(o * o, ones_ref[...]) * (1.0 / HEAD_DIM)
    return _bf(o * lax.rsqrt(ms + EPS) * g_ref[...] * _silu(og.astype(F32)))


def _hgrn_kernel(*refs, n_sub, n_blk, has_init):
    qf_ref, vf_ref, ff_ref, qb_ref, vb_ref, fb_ref, ogf_ref, ogb_ref, lb_ref, g_ref, ones_ref = refs[:11]
    n_in = 14 if has_init else 13
    st0_ref = refs[11] if has_init else None
    tsel_ref, pmask_ref = refs[n_in - 2], refs[n_in - 1]
    if n_blk == 1:
        af_ref, fin_ref, st_ref, b_scr, blk_scr = refs[n_in:]
    else:
        af_ref, fin_ref, st_ref, b_scr, blk_scr, keep_scr = refs[n_in:]
    c = HGRN_CHUNK
    step = pl.program_id(1)

    @pl.when(step == 0)
    def _():
        st_ref[...] = st0_ref[0] if has_init else jnp.zeros_like(st_ref)

    head_mask = _head_mask()
    st_f = st_ref[0]
    st_b = st_ref[1]
    for j in range(n_sub):
        rf = slice(j * c, (j + 1) * c)
        rb = slice((n_sub - 1 - j) * c, (n_sub - j) * c)
        o_f, st_f = _hgrn_chunk(qf_ref[rf, :], vf_ref[rf, :], ff_ref[rf, :], lb_ref[0:1, :], st_f,
                                tsel_ref[0], pmask_ref.at[0], False, head_mask, b_scr.at[2 * j])
        o_b, st_b = _hgrn_chunk(qb_ref[rb, :], vb_ref[rb, :], fb_ref[rb, :], lb_ref[1:2, :], st_b,
                                tsel_ref[1], pmask_ref.at[1], True, head_mask, b_scr.at[2 * j + 1])
        blk_scr[0, rf, :] = o_f
        blk_scr[1, rb, :] = o_b
    st_ref[0] = st_f
    st_ref[1] = st_b

    if n_blk == 1:
        af_ref[...] = _hgrn_finish(blk_scr[0] + blk_scr[1], ogf_ref[...], g_ref, ones_ref)
    else:
        half = n_blk // 2

        @pl.when(step < half)
        def _():
            keep_scr[0, step] = blk_scr[0]
            keep_scr[1, step] = blk_scr[1]

        @pl.when(step >= half)
        def _():
            other = n_blk - 1 - step
            af_ref[0, 0, 1] = _hgrn_finish(blk_scr[0] + keep_scr[1, other], ogf_ref[...], g_ref, ones_ref)
            af_ref[0, 0, 0] = _hgrn_finish(blk_scr[1] + keep_scr[0, other], ogb_ref[...], g_ref, ones_ref)

    @pl.when(step == n_blk - 1)
    def _():
        fin_ref[0, 0] = _collapse_heads(st_f, head_mask)
        fin_ref[0, 1] = _collapse_heads(st_b, head_mask)


def _hgrn_mixer(zf, zr, lb2, onorm_g, st0, batch, seqlen, n_sub):
    n = batch * seqlen
    tb = n_sub * HGRN_CHUNK
    n_blk = seqlen // tb
    assert n_blk == 1 or n_blk % 2 == 0
    half = n_blk // 2
    tsel, pmask = _hgrn_tables()
    has_init = st0 is not None
    fwd = lambda col: pl.BlockSpec((tb, BRANCH_W), lambda b, c: (b * n_blk + c, col))
    bwd = lambda col: pl.BlockSpec((tb, BRANCH_W), lambda b, c: (b * n_blk + n_blk - 1 - c, col))
    in_specs = [fwd(COL_HQ), fwd(COL_HI), fwd(COL_HFF), bwd(COL_HQ), bwd(COL_HI), bwd(COL_HFB), fwd(COL_HOG), bwd(COL_HOG),
                _const_spec((2, BRANCH_W)), _const_spec((1, BRANCH_W)), _const_spec((BRANCH_W, BRANCH_W))]
    args = [zf, zf, zf, zf, zf, zf, zr, zr, lb2, onorm_g.reshape(1, BRANCH_W), _group_ones()]
    if has_init:
        in_specs.append(pl.BlockSpec((1, 2, BRANCH_W, BRANCH_W), lambda b, c: (b, 0, 0, 0)))
        args.append(st0)
    in_specs += [_const_spec(tsel.shape), _const_spec(pmask.shape)]
    args += [tsel, pmask]
    scratch = [pltpu.VMEM((2, BRANCH_W, BRANCH_W), F32), pltpu.VMEM((2 * n_sub, HGRN_CHUNK, BRANCH_W), F32),
               pltpu.VMEM((2, tb, BRANCH_W), F32)]
    if n_blk == 1:
        out_specs = [pl.BlockSpec((tb, BRANCH_W), lambda b, c: (b, 0))]
        out_shape = [jax.ShapeDtypeStruct((n, BRANCH_W), BF16)]
    else:
        out_specs = [pl.BlockSpec((1, 1, 2, tb, BRANCH_W), lambda b, c: (b, jnp.maximum(c - half, 0), 0, 0, 0))]
        out_shape = [jax.ShapeDtypeStruct((batch, half, 2, tb, BRANCH_W), BF16)]
        scratch.append(pltpu.VMEM((2, half, tb, BRANCH_W), F32))
    out_specs.append(pl.BlockSpec((1, 2, HEAD_DIM, BRANCH_W), lambda b, c: (b, 0, 0, 0)))
    out_shape.append(jax.ShapeDtypeStruct((batch, 2, HEAD_DIM, BRANCH_W), F32))
    res = pl.pallas_call(
        functools.partial(_hgrn_kernel, n_sub=n_sub, n_blk=n_blk, has_init=has_init),
        grid=(batch, n_blk),
        in_specs=in_specs,
        out_specs=out_specs,
        out_shape=out_shape,
        scratch_shapes=scratch,
        compiler_params=_params("arbitrary", "arbitrary"),
        name="hgrn_mixer",
    )(*args)
    return res[0], n_blk, tb, res[1]


def _group_ones():
    g = np.arange(BRANCH_W) // HEAD_DIM
    return jnp.asarray((g[:, None] == g[None, :]).astype(np.float32), BF16)


def _ctx_attn_kernel(*refs, n_q, n_kv, has_sink):
    if has_sink:
        sink_ref, q_ref, k_ref, v_ref, o_ref = refs
    else:
        q_ref, k_ref, v_ref, o_ref = refs
    group = n_q // n_kv
    for slot, hq in enumerate(SWA_HEAD_ORDER if has_sink else range(n_q)):
        hk = hq // group
        q = q_ref[:, slot * HEAD_DIM:(slot + 1) * HEAD_DIM]
        k = k_ref[:, hk * HEAD_DIM:(hk + 1) * HEAD_DIM]
        v = v_ref[:, hk * HEAD_DIM:(hk + 1) * HEAD_DIM]
        s = _dot_nt(q, k)
        m = jnp.max(s, axis=-1, keepdims=True)
        if has_sink:
            sink = sink_ref[hq] * LOG2E
            m = jnp.maximum(m, sink)
        p = jnp.exp2(s - m)
        l = jnp.sum(p, axis=-1, keepdims=True)
        if has_sink:
            l = l + jnp.exp2(sink - m)
        o_ref[:, slot * HEAD_DIM:(slot + 1) * HEAD_DIM] = _bf(_dot(_bf(p), v) / l)


def _ctx_attn(zr, batch, seqlen, q_col, k_col, v_col, kv_width, n_kv, sink):
    n = batch * seqlen
    n_q = N_HEADS
    has_sink = sink is not None
    in_specs = [
        pl.BlockSpec((seqlen, BRANCH_W), lambda b: (b, q_col)),
        pl.BlockSpec((seqlen, kv_width), lambda b: (b, k_col)),
        pl.BlockSpec((seqlen, kv_width), lambda b: (b, v_col)),
    ]
    args = [zr, zr, zr]
    if has_sink:
        in_specs = [pl.BlockSpec(memory_space=pltpu.SMEM)] + in_specs
        args = [sink.astype(F32)] + args
    return pl.pallas_call(
        functools.partial(_ctx_attn_kernel, n_q=n_q, n_kv=n_kv, has_sink=has_sink),
        grid=(batch,),
        in_specs=in_specs,
        out_specs=pl.BlockSpec((seqlen, BRANCH_W), lambda b: (b, 0)),
        out_shape=jax.ShapeDtypeStruct((n, BRANCH_W), BF16),
        compiler_params=_params("arbitrary"),
        name="ctx_attn_sink" if has_sink else "ctx_attn",
    )(*args)


def _na_bias_kernel(rpb_ref, onehot_ref, mask_ref, o_ref):
    o_ref[...] = (_dot01_right(rpb_ref[...], onehot_ref[...]) + mask_ref[...]) * LOG2E


def _na_bias_tables(na_rpb):
    n_dr, n_dc = 2 * NA_ROWS - 1, 2 * NA_COLS - 1
    col = np.arange(GRID_W)
    col_start = np.clip(col - NA_COLS // 2, 0, GRID_W - NA_COLS)
    col_mask = (col[None, :] >= col_start[:, None]) & (col[None, :] < col_start[:, None] + NA_COLS)
    d_col = np.clip(col[None, :] - col[:, None], -(NA_COLS - 1), NA_COLS - 1) + (NA_COLS - 1)
    rows = DEPTH * N_HEADS * n_dr
    assert rows <= LANES and n_dc <= LANES
    onehot = (np.arange(LANES)[:, None] == d_col.reshape(1, -1)).astype(np.float32)
    mask_add = np.where(col_mask.reshape(1, -1), 0.0, -np.inf).astype(np.float32)
    rpb2 = jnp.zeros((LANES, LANES), F32).at[:rows, :n_dc].set(na_rpb.astype(F32).reshape(rows, n_dc))
    tab = pl.pallas_call(
        _na_bias_kernel,
        grid=(1,),
        in_specs=[_const_spec((LANES, LANES)), _const_spec((LANES, GRID_W * GRID_W)), _const_spec((1, GRID_W * GRID_W))],
        out_specs=_const_spec((LANES, GRID_W * GRID_W)),
        out_shape=jax.ShapeDtypeStruct((LANES, GRID_W * GRID_W), F32),
        compiler_params=_params("arbitrary"),
        name="na_bias",
    )(rpb2, jnp.asarray(onehot, BF16), jnp.asarray(mask_add))
    tab = tab[:rows].reshape(DEPTH, N_HEADS, n_dr, GRID_W, GRID_W)
    slabs = [jnp.transpose(tab[:, :, first:first + NA_ROWS], (0, 1, 3, 2, 4)).reshape(DEPTH, N_HEADS * GRID_W, NA_ROWS * GRID_W)
             for first in range(NA_ROWS)]
    return jnp.stack(slabs, axis=1)


def _na_lat_kernel(q_ref, k_ref, v_ref, kc_ref, vc_ref, bias_ref, o_ref, *, rows_per_step, n_rows):
    nk = NA_ROWS * GRID_W
    r0 = pl.program_id(1) * rows_per_step
    head_mask = _head_mask()

    def body(i, carry):
        r = r0 + i
        row_start = jnp.clip(r - NA_ROWS // 2, 0, n_rows - NA_ROWS)
        first = row_start - r + (NA_ROWS - 1)
        k0 = pl.multiple_of(row_start * GRID_W, GRID_W)
        q0 = pl.multiple_of(i * GRID_W, GRID_W)
        qx = _expand_heads(q_ref[pl.ds(q0, GRID_W), :], head_mask)
        s_lat = _dot_nt(qx, k_ref[pl.ds(k0, nk), :]) + bias_ref[first]
        s_ctx = _dot_nt(qx, kc_ref[0])
        m = jnp.maximum(jnp.max(s_lat, axis=-1, keepdims=True), jnp.max(s_ctx, axis=-1, keepdims=True))
        p_lat = jnp.exp2(s_lat - m)
        p_ctx = jnp.exp2(s_ctx - m)
        l = jnp.sum(p_lat, axis=-1, keepdims=True) + jnp.sum(p_ctx, axis=-1, keepdims=True)
        acc = _dot(_bf(p_lat), v_ref[pl.ds(k0, nk), :]) + _dot(_bf(p_ctx), vc_ref[0])
        o_ref[pl.ds(q0, GRID_W), :] = _bf(_collapse_heads(acc / l, head_mask))
        return carry

    lax.fori_loop(0, rows_per_step, body, 0, unroll=8)


def _na_latent(zr, kc, vc, bias_tab, batch, seqlen):
    n = batch * seqlen
    n_rows = seqlen // GRID_W
    assert n_rows >= NA_ROWS
    rows_per_step = n_rows
    steps = n_rows // rows_per_step
    tq = rows_per_step * GRID_W
    n_ctx = kc.shape[1]
    return pl.pallas_call(
        functools.partial(_na_lat_kernel, rows_per_step=rows_per_step, n_rows=n_rows),
        grid=(batch, steps),
        in_specs=[
            pl.BlockSpec((tq, BRANCH_W), lambda b, j: (b * steps + j, COL_NAQ)),
            pl.BlockSpec((seqlen, BRANCH_W), lambda b, j: (b, COL_NAK)),
            pl.BlockSpec((seqlen, BRANCH_W), lambda b, j: (b, COL_NAV)),
            pl.BlockSpec((1, n_ctx, BRANCH_W), lambda b, j: (b, 0, 0)),
            pl.BlockSpec((1, n_ctx, BRANCH_W), lambda b, j: (b, 0, 0)),
            _const_spec(bias_tab.shape),
        ],
        out_specs=pl.BlockSpec((tq, BRANCH_W), lambda b, j: (b * steps + j, 0)),
        out_shape=jax.ShapeDtypeStruct((n, BRANCH_W), BF16),
        compiler_params=_params("arbitrary", "arbitrary"),
        name="na_latent",
    )(zr, zr, zr, kc, vc, bias_tab)


def _rope_tables(seqlen):
    half = HEAD_DIM // 2
    t = np.arange(seqlen)
    rows = (t // GRID_W).astype(np.float32)
    cols = (t % GRID_W).astype(np.float32)
    inv = (1.0 / (np.float32(ROPE_THETA) ** (np.arange(0, half, 2, dtype=np.float32) / np.float32(half)))).astype(np.float32)
    ang_r = rows[:, None] * inv[None, :]
    ang_c = cols[:, None] * inv[None, :]
    cos = np.concatenate([np.cos(ang_r), np.cos(ang_r), np.cos(ang_c), np.cos(ang_c)], axis=-1)
    sin = np.concatenate([-np.sin(ang_r), np.sin(ang_r), -np.sin(ang_c), np.sin(ang_c)], axis=-1)
    cos = np.tile(cos.astype(np.float32), (1, N_HEADS))
    sin = np.tile(sin.astype(np.float32), (1, N_HEADS))
    return jnp.asarray(cos), jnp.asarray(sin)


def _rope(x, cos, sin_signed):
    w = x.shape[-1]
    lane = lax.broadcasted_iota(jnp.int32, x.shape, 1)
    partner = jnp.where((lane % 32) < 16, pltpu.roll(x, w - 16, 1), pltpu.roll(x, 16, 1))
    return x * cos + partner * sin_signed


def _swa_lat_kernel(sink_ref, q_ref, k_ref, v_ref, kc_ref, vc_ref, cos_ref, sin_ref, band_ref, o_ref, *, seqlen):
    for sub in range(SWA_BLOCKS_PER_STEP):
        rows = slice(sub * SWA_BLOCK, (sub + 1) * SWA_BLOCK)
        _swa_block(pl.program_id(1) * SWA_BLOCKS_PER_STEP + sub, sink_ref, q_ref.at[rows, :], k_ref, v_ref, kc_ref, vc_ref,
                   cos_ref, sin_ref, band_ref, o_ref.at[rows, :], seqlen)


def _swa_block(j, sink_ref, q_ref, k_ref, v_ref, kc_ref, vc_ref, cos_ref, sin_ref, band_ref, o_ref, seqlen):
    blk = SWA_BLOCK
    nwin = 3 * blk
    kvw = SWA_KV_HEADS * HEAD_DIM
    q0 = pl.multiple_of(j * blk, blk)
    k_blk = jnp.clip(j - 1, 0, seqlen // blk - 3)
    k0 = pl.multiple_of(k_blk * blk, blk)
    q = _rope(q_ref[...].astype(F32), cos_ref[pl.ds(q0, blk), :], sin_ref[pl.ds(q0, blk), :])
    kw = _rope(k_ref[pl.ds(k0, nwin), :].astype(F32), cos_ref[pl.ds(k0, nwin), 0:kvw], sin_ref[pl.ds(k0, nwin), 0:kvw])
    kw = _bf(kw)
    vw = v_ref[pl.ds(k0, nwin), :]
    n_slot = N_HEADS
    qb = _bf(q)
    lane_kv = lax.broadcasted_iota(jnp.int32, (blk, kvw), 1) // HEAD_DIM
    qx = jnp.concatenate(
        [jnp.where(lane_kv == (slot % SWA_KV_HEADS), qb[:, (slot // SWA_KV_HEADS) * kvw:(slot // SWA_KV_HEADS + 1) * kvw],
                   jnp.zeros((), BF16)) for slot in range(n_slot)], axis=0)
    row_slot = lax.broadcasted_iota(jnp.int32, (n_slot * blk, 1), 0) // blk
    sink = jnp.zeros((n_slot * blk, 1), F32)
    for slot in range(n_slot):
        sink = jnp.where(row_slot == slot, sink_ref[SWA_HEAD_ORDER[slot]] * LOG2E, sink)
    band = band_ref[j - k_blk]
    s_band = _dot_nt(qx, kw) + jnp.concatenate([band] * n_slot, axis=0)
    s_ctx = _dot_nt(qx, kc_ref[0])
    m = jnp.maximum(jnp.maximum(jnp.max(s_band, axis=-1, keepdims=True), jnp.max(s_ctx, axis=-1, keepdims=True)), sink)
    p_band = jnp.exp2(s_band - m)
    p_ctx = jnp.exp2(s_ctx - m)
    l = jnp.sum(p_band, axis=-1, keepdims=True) + jnp.sum(p_ctx, axis=-1, keepdims=True) + jnp.exp2(sink - m)
    acc = (_dot(_bf(p_band), vw) + _dot(_bf(p_ctx), vc_ref[0])) / l
    halves = []
    for half in range(n_slot // SWA_KV_HEADS):
        r0 = half * SWA_KV_HEADS * blk
        halves.append(jnp.where(lane_kv == 0, acc[r0:r0 + blk], acc[r0 + blk:r0 + 2 * blk]))
    o_ref[...] = _bf(jnp.concatenate(halves, axis=-1))


def _swa_latent(zr, kc, vc, sink, batch, seqlen):
    n = batch * seqlen
    nb = seqlen // SWA_BLOCK
    kvw = SWA_KV_HEADS * HEAD_DIM
    n_ctx = kc.shape[1]
    cos, sin = _rope_tables(seqlen)
    assert nb >= 3 and nb % SWA_BLOCKS_PER_STEP == 0
    steps = nb // SWA_BLOCKS_PER_STEP
    tq = SWA_BLOCKS_PER_STEP * SWA_BLOCK
    a = np.arange(SWA_BLOCK)[:, None]
    c = np.arange(3 * SWA_BLOCK)[None, :]
    band = jnp.asarray(np.stack([np.where(np.abs(c - a - off * SWA_BLOCK) <= SWA_WINDOW, 0.0, -np.inf) for off in range(3)])
                       .astype(np.float32))
    return pl.pallas_call(
        functools.partial(_swa_lat_kernel, seqlen=seqlen),
        grid=(batch, steps),
        in_specs=[
            pl.BlockSpec(memory_space=pltpu.SMEM),
            pl.BlockSpec((tq, BRANCH_W), lambda b, j: (b * steps + j, COL_SQ)),
            pl.BlockSpec((seqlen, kvw), lambda b, j: (b, COL_SK128)),
            pl.BlockSpec((seqlen, kvw), lambda b, j: (b, COL_SV128)),
            pl.BlockSpec((1, n_ctx, kvw), lambda b, j: (b, 0, 0)),
            pl.BlockSpec((1, n_ctx, kvw), lambda b, j: (b, 0, 0)),
            _const_spec(cos.shape),
            _const_spec(sin.shape),
            _const_spec(band.shape),
        ],
        out_specs=pl.BlockSpec((tq, BRANCH_W), lambda b, j: (b * steps + j, 0)),
        out_shape=jax.ShapeDtypeStruct((n, BRANCH_W), BF16),
        compiler_params=_params("arbitrary", "arbitrary"),
        name="swa_latent",
    )(sink.astype(F32), zr, zr, zr, kc, vc, cos, sin, band)


def _merge_kernel(a_ref, b_ref, c_ref, d_ref, g0_ref, g1_ref, g2_ref, g3_ref, x_ref, gate_ref, sh_ref, sc_ref,
                  ng_ref, wb_ref, wo_ref, wr_ref, br_ref, tri_ref, upper_ref, x1_ref, h_ref, route_ref, seg_ref):
    tm = x_ref.shape[0]
    mix = None
    for br, gt, i in ((a_ref, g0_ref, 0), (b_ref, g1_ref, 1), (c_ref, g2_ref, 2), (d_ref, g3_ref, 3)):
        t = (1.0 + jnp.tanh(gt[...].astype(F32))) * _dot(br[...], wb_ref[i])
        mix = t if mix is None else mix + t
    x1 = x_ref[...] + gate_ref[0] * _dot(_bf(mix), wo_ref[...])
    x1_ref[...] = x1
    ms = jnp.mean(x1 * x1, axis=-1, keepdims=True)
    h = x1 * lax.rsqrt(ms + EPS) * ng_ref[...]
    h = h * (1.0 + sc_ref[0]) + sh_ref[0]
    h_ref[...] = _bf(h)

    hh = _bf(h)
    hm = _bf(h - hh.astype(F32))
    logits = (_dot(hh, wr_ref[0]) + _dot(hm, wr_ref[0]) + _dot(hh, wr_ref[1])) + br_ref[...]
    lane_i = lax.broadcasted_iota(jnp.int32, logits.shape, 1)
    lane = lane_i.astype(F32)
    lane_grp = (lane_i // EXPERTS_PER_GROUP).astype(F32)
    neg = -jnp.inf
    far = float(4 * N_EXPERTS)
    is_g = (lane_i >= N_EXPERTS) & (lane_i < N_EXPERTS + N_GROUPS)
    gl = jnp.where(is_g, logits, neg)
    gmax = jnp.max(gl, axis=-1, keepdims=True)
    gsum = jnp.sum(jnp.exp(gl - gmax), axis=-1, keepdims=True)
    g_top_p = 1.0 / gsum
    g_idx = jnp.min(jnp.where(is_g & (gl == gmax), lane, far), axis=-1, keepdims=True) - float(N_EXPERTS)
    in_grp = (lane_i < N_EXPERTS) & (lane_grp == g_idx)
    e_l = jnp.where(in_grp, logits, neg)
    e1 = jnp.max(e_l, axis=-1, keepdims=True)
    i1 = jnp.min(jnp.where(in_grp & (e_l == e1), lane, far), axis=-1, keepdims=True)
    e_l2 = jnp.where(lane == i1, neg, e_l)
    e2 = jnp.max(e_l2, axis=-1, keepdims=True)
    i2 = jnp.min(jnp.where(in_grp & (lane != i1) & (e_l2 == e2), lane, far), axis=-1, keepdims=True)
    t2 = jnp.exp(e2 - e1)
    w1 = g_top_p / (1.0 + t2)
    w2 = w1 * t2

    sel = (lane == i1) | (lane == i2)
    sel_f = jnp.where(sel, 1.0, 0.0)
    cum = _dot(tri_ref[...], _bf(sel_f))
    counts = cum[tm - 1:tm, :]
    padded = jnp.floor((counts + (MOE_ROW_ALIGN - 1)) * (1.0 / MOE_ROW_ALIGN)) * MOE_ROW_ALIGN
    seg_start = _dot(_bf(jnp.broadcast_to(padded, (SUBLANES, LANES))), upper_ref[...])[0:1, :]
    slot = seg_start + cum - sel_f
    pos1 = jnp.sum(jnp.where(lane == i1, slot, 0.0), axis=-1, keepdims=True)
    pos2 = jnp.sum(jnp.where(lane == i2, slot, 0.0), axis=-1, keepdims=True)
    route_ref[...] = jnp.where(lane_i == 0, pos1, jnp.where(lane_i == 1, pos2, jnp.where(lane_i == 2, w1,
                               jnp.where(lane_i == 3, w2, 0.0))))
    seg = jnp.where(lane_i[0:1] < N_EXPERTS, seg_start, pltpu.roll(jnp.broadcast_to(padded, (SUBLANES, LANES)), N_EXPERTS, 1)[0:1])
    seg_ref[0] = jnp.where(lane_i[0:1] < 2 * N_EXPERTS, seg, 0.0).astype(jnp.int32)


def _router_tables(w_rg, b_rg, w_re, b_re):
    w = jnp.zeros((D_MODEL, LANES), F32)
    w = w.at[:, :N_EXPERTS].set(w_re.astype(F32)).at[:, N_EXPERTS:N_EXPERTS + N_GROUPS].set(w_rg.astype(F32))
    b = jnp.zeros((1, LANES), F32)
    b = b.at[0, :N_EXPERTS].set(b_re.astype(F32)).at[0, N_EXPERTS:N_EXPERTS + N_GROUPS].set(b_rg.astype(F32))
    return jnp.stack(_split3(w)[:2], axis=0), b


def _merge(a, hgrn_blocks, hgrn_block_rows, branches, zr, x, mods3, mod_row, norm_g, w_branch_bf, w_out_bf, wr3, br,
           layer):
    n = x.shape[0]
    half = hgrn_blocks // 2
    if hgrn_blocks > 1:
        assert hgrn_block_rows == MOE_TILE

        def pair_index(i):
            k = i % hgrn_blocks
            upper = k >= half
            return (i // hgrn_blocks, jnp.where(upper, k - half, half - 1 - k), jnp.where(upper, 1, 0), 0, 0)

        a_spec = pl.BlockSpec((None, None, None, MOE_TILE, BRANCH_W), pair_index)
    else:
        a_spec = pl.BlockSpec((MOE_TILE, BRANCH_W), lambda i: (i, 0))
    per_layer = lambda shape: pl.BlockSpec((None,) + shape, lambda i: (layer,) + (0,) * len(shape))
    tm = MOE_TILE
    t = np.arange(tm)
    tri = jnp.asarray((t[None, :] <= t[:, None]).astype(np.float32), BF16)
    e = np.arange(LANES)
    upper = jnp.asarray((e[:, None] < e[None, :]).astype(np.float32), BF16)
    row = lambda w: pl.BlockSpec((tm, w), lambda i: (i, 0))
    gate = lambda k: pl.BlockSpec((tm, D_MODEL), lambda i: (i, COL_GATES1024 + k))
    mod = lambda k: pl.BlockSpec((1, 1, D_MODEL), lambda i: (mod_row(i, tm), 0, k))
    return pl.pallas_call(
        _merge_kernel,
        grid=(n // tm,),
        in_specs=[
            a_spec, row(BRANCH_W), row(BRANCH_W), row(BRANCH_W),
            gate(0), gate(1), gate(2), gate(3),
            row(D_MODEL),
            mod(2), mod(3), mod(4),
            _const_spec((1, D_MODEL)),
            per_layer((N_BRANCH, BRANCH_W, D_MODEL)),
            per_layer((D_MODEL, D_MODEL)),
            per_layer((2, D_MODEL, LANES)),
            per_layer((1, LANES)),
            _const_spec((tm, tm)),
            _const_spec((LANES, LANES)),
        ],
        out_specs=[row(D_MODEL), row(D_MODEL), row(LANES), pl.BlockSpec((1, 1, LANES), lambda i: (i, 0, 0))],
        out_shape=[
            jax.ShapeDtypeStruct((n, D_MODEL), F32),
            jax.ShapeDtypeStruct((n, D_MODEL), BF16),
            jax.ShapeDtypeStruct((n, LANES), F32),
            jax.ShapeDtypeStruct((n // tm, 1, LANES), jnp.int32),
        ],
        compiler_params=_params("arbitrary"),
        name="merge",
    )(a, *branches, zr, zr, zr, zr, x, mods3, mods3, mods3, norm_g.reshape(1, D_MODEL), w_branch_bf, w_out_bf, wr3, br,
      tri, upper)


def _moe_kernel(seg_ref, h_ref, route_ref, x1_ref, gate_ref, fg_ref, wg_ref, wu_ref, wd_ref, *outs_and_scratch, final):
    if final:
        x2_ref, y_ref, hs_scr, ys_scr = outs_and_scratch
    else:
        x2_ref, hs_scr, ys_scr = outs_and_scratch
    tm = MOE_TILE
    tile = pl.program_id(0)
    route = route_ref[...]
    route_t = route.T
    pos1_row, pos2_row = route_t[0:1, :], route_t[1:2, :]
    h = h_ref[...]
    for rb in range(MOE_ROWS // MOE_GATHER_BLK):
        r = (rb * MOE_GATHER_BLK + lax.broadcasted_iota(jnp.int32, (MOE_GATHER_BLK, tm), 0)).astype(F32)
        p = jnp.where((r == pos1_row) | (r == pos2_row), 1.0, 0.0)
        hs_scr[rb * MOE_GATHER_BLK:(rb + 1) * MOE_GATHER_BLK, :] = _bf(_dot(_bf(p), h))
    ys_scr[...] = jnp.zeros_like(ys_scr)

    row_in_chunk = lax.broadcasted_iota(jnp.int32, (MOE_CHUNK, D_MODEL), 0)

    def expert_chunk(e, r0, end):
        r0 = pl.multiple_of(r0, MOE_ROW_ALIGN)
        rows = hs_scr[pl.ds(r0, MOE_CHUNK), :]
        a = _silu(_dot(rows, wg_ref[e])) * _dot(rows, wu_ref[e])
        y = _dot(_bf(a), wd_ref[e])
        ys_scr[pl.ds(r0, MOE_CHUNK), :] = jnp.where(row_in_chunk < end - r0, _bf(y), ys_scr[pl.ds(r0, MOE_CHUNK), :])

    starts = [seg_ref[tile * LANES + e] for e in range(N_EXPERTS)]
    ends = [starts[e] + seg_ref[tile * LANES + N_EXPERTS + e] for e in range(N_EXPERTS)]
    for e in range(N_EXPERTS):
        expert_chunk(e, starts[e], ends[e])
    for e in range(N_EXPERTS):
        n_chunks = lax.div(ends[e] - starts[e] + (MOE_CHUNK - 1), MOE_CHUNK)

        def more(c, carry, e=e):
            expert_chunk(e, starts[e] + c * MOE_CHUNK, ends[e])
            return carry

        lax.fori_loop(1, n_chunks, more, 0)

    pos1, pos2, w1, w2 = route[:, 0:1], route[:, 1:2], route[:, 2:3], route[:, 3:4]
    acc = None
    for cb in range(MOE_SEG_ROWS // MOE_SCATTER_BLK):
        r = (cb * MOE_SCATTER_BLK + lax.broadcasted_iota(jnp.int32, (tm, MOE_SCATTER_BLK), 1)).astype(F32)
        q = jnp.where(r == pos1, w1, jnp.where(r == pos2, w2, 0.0))
        part = _dot(_bf(q), ys_scr[cb * MOE_SCATTER_BLK:(cb + 1) * MOE_SCATTER_BLK, :])
        acc = part if acc is None else acc + part
    x2 = x1_ref[...] + gate_ref[0] * acc
    x2_ref[...] = x2
    if final:
        ms = jnp.mean(x2 * x2, axis=-1, keepdims=True)
        y_ref[...] = x2 * lax.rsqrt(ms + EPS) * fg_ref[...]


def _moe(h, route, seg, x1, mods3, mod_row, final_g, wg_bf, wu_bf, wd_bf, layer, final):
    n = x1.shape[0]
    tm = MOE_TILE
    row = lambda w: pl.BlockSpec((tm, w), lambda i, s: (i, 0))
    resident = lambda shape: pl.BlockSpec((None,) + shape, lambda i, s: (layer,) + (0,) * len(shape),
                                          pipeline_mode=pl.Buffered(1))
    out_specs = [row(D_MODEL)]
    out_shape = [jax.ShapeDtypeStruct((n, D_MODEL), F32)]
    if final:
        out_specs.append(row(D_MODEL))
        out_shape.append(jax.ShapeDtypeStruct((n, D_MODEL), F32))
    return pl.pallas_call(
        functools.partial(_moe_kernel, final=final),
        grid_spec=pltpu.PrefetchScalarGridSpec(
            num_scalar_prefetch=1,
            grid=(n // tm,),
            in_specs=[
                row(D_MODEL), row(LANES), row(D_MODEL),
                pl.BlockSpec((1, 1, D_MODEL), lambda i, s: (mod_row(i, tm), 0, 5)),
                pl.BlockSpec((1, D_MODEL), lambda i, s: (0, 0)),
                resident((N_EXPERTS, D_MODEL, EXPERT_FF)),
                resident((N_EXPERTS, D_MODEL, EXPERT_FF)),
                resident((N_EXPERTS, EXPERT_FF, D_MODEL)),
            ],
            out_specs=out_specs,
            scratch_shapes=[pltpu.VMEM((MOE_ROWS, D_MODEL), BF16), pltpu.VMEM((MOE_ROWS, D_MODEL), BF16)],
        ),
        out_shape=out_shape,
        compiler_params=_params("arbitrary"),
        name="moe_final" if final else "moe",
    )(seg.reshape(-1), h, route, x1, mods3, final_g.reshape(1, D_MODEL), wg_bf, wu_bf, wd_bf)


def _state_to_blockdiag_t(s):
    b = s.shape[0]
    st = jnp.swapaxes(s.astype(F32), -1, -2)
    eye = jnp.eye(N_HEADS, dtype=F32)
    full = st[:, :, :, :, None, :] * eye[None, None, :, None, :, None]
    return full.reshape(b, 2, BRANCH_W, BRANCH_W)


def _compact_to_state(fin):
    b = fin.shape[0]
    return jnp.transpose(fin.reshape(b, 2, HEAD_DIM, N_HEADS, HEAD_DIM), (0, 1, 3, 4, 2))


def _layer(x, l, w, mods3, mod_row, batch, seqlen, latent, st0, caches, final):
    proj = _in_proj(x, mods3, mod_row, w["norm1_g"][l], w["w_in"], w["smlp_ws"][l], w["smlp_b"][l], l, IN_TILE, not latent)
    zf, zr, c_out = proj[0], proj[1], proj[2]
    kv = None if latent else proj[3:]
    a_parts, hgrn_blocks, hgrn_block_rows, fin = _hgrn_mixer(
        zf, zr, w["lb"][l], w["onorm_g"][l], st0, batch, seqlen, n_sub=min(HGRN_CHUNKS_PER_STEP, seqlen // HGRN_CHUNK))
    if latent:
        ck_na, cv_na, ck_swa, cv_swa = caches
        b_out = _na_latent(zr, ck_na, cv_na, w["na_bias"][l], batch, seqlen)
        d_out = _swa_latent(zr, ck_swa, cv_swa, w["swa_sink"][l], batch, seqlen)
    else:
        b_out = _ctx_attn(zr, batch, seqlen, COL_NAQ, COL_NAK, COL_NAV, BRANCH_W, N_HEADS, None)
        d_out = _ctx_attn(zr, batch, seqlen, COL_SQ, COL_SK128, COL_SV128, SWA_KV_HEADS * HEAD_DIM, SWA_KV_HEADS,
                          w["swa_sink"][l])
    x1, h2, route, seg = _merge(a_parts, hgrn_blocks, hgrn_block_rows, (b_out, c_out, d_out), zr, x, mods3, mod_row,
                                w["norm2_g"][l], w["w_branch"], w["w_out"], w["wr"], w["br"], l)
    out = _moe(h2, route, seg, x1, mods3, mod_row, w["final_g"], w["wg"], w["wu"], w["wd"], l, final)
    return out, kv, fin


def kernel(x_prompt, x_sample, c, cache_na_k, cache_na_v, cache_swa_k, cache_swa_v, state_hgrn, c_ctx, w_ada, b_ada, norm1_g, norm2_g, w_in, hgrn_lb, hgrn_onorm_g, na_rpb, smlp_ws, smlp_b, swa_sink, w_branch, w_out, router_g_w, router_g_b, router_e_w, router_e_b, moe_w_gate, moe_w_up, moe_w_down, final_g):
    bc, lc, _ = x_prompt.shape
    bl, ll, _ = x_sample.shape
    n_ctx_tok = bc * lc

    cond = jnp.zeros((MOD_ROWS, D_MODEL), F32).at[0].set(c_ctx.astype(F32)).at[1:1 + bl].set(c.astype(F32))
    mods = _ada_mods(cond, w_ada, b_ada)

    lb_soft = jax.nn.softmax(hgrn_lb.astype(F32), axis=0)
    lb_all = jnp.cumsum(lb_soft, axis=0) - lb_soft[0:1]

    col_scale = np.ones((P_IN,), np.float32)
    col_scale[ZF_W + COL_GATES1024 * 1024:] = 0.5
    for q_col in (COL_NAQ, COL_SQ):
        col_scale[ZF_W + q_col * BRANCH_W:ZF_W + (q_col + 1) * BRANCH_W] = QK_PRESCALE
    swa_rows = jnp.concatenate([w_branch[:, N_BRANCH - 1, h * HEAD_DIM:(h + 1) * HEAD_DIM] for h in SWA_HEAD_ORDER], axis=1)
    routers = [_router_tables(router_g_w[l], router_g_b[l], router_e_w[l], router_e_b[l]) for l in range(DEPTH)]
    w = dict(
        norm1_g=norm1_g, norm2_g=norm2_g, lb=lb_all, onorm_g=hgrn_onorm_g, smlp_ws=smlp_ws, smlp_b=smlp_b,
        swa_sink=swa_sink, final_g=final_g,
        w_in=_bf(w_in * jnp.asarray(col_scale)[None, None, :]),
        w_branch=_bf(w_branch.at[:, N_BRANCH - 1].set(swa_rows)),
        w_out=_bf(0.5 * w_out),
        wr=jnp.stack([r[0] for r in routers]), br=jnp.stack([r[1] for r in routers]),
        wg=_bf(moe_w_gate), wu=_bf(moe_w_up), wd=_bf(moe_w_down),
        na_bias=_na_bias_tables(na_rpb),
    )

    ctx_row = lambda i, tm: 0
    lat_row = lambda i, tm: 1 + i // (ll // tm)
    xp = x_prompt.reshape(n_ctx_tok, D_MODEL)
    kvs, states = [], []
    y_prompt = None
    for l in range(DEPTH):
        final = l == DEPTH - 1
        out, kv, fin = _layer(xp, l, w, mods[l].reshape(MOD_ROWS, 1, -1), ctx_row, bc, lc, False, None, None, final)
        if final:
            xp, y_prompt = out
        else:
            xp = out[0]
        kvs.append(kv)
        states.append(_compact_to_state(fin))
    cache_out = [jnp.stack([kvs[l][k].reshape(bc, lc, -1, HEAD_DIM) for l in range(DEPTH)], axis=1) for k in range(4)]

    xs = x_sample.reshape(bl * ll, D_MODEL)
    y_sample = None
    n_past = cache_na_k.shape[2]
    for l in range(DEPTH):
        caches = (_bf(cache_na_k[:, l]).reshape(bl, n_past, BRANCH_W), _bf(cache_na_v[:, l]).reshape(bl, n_past, BRANCH_W),
                  _bf(cache_swa_k[:, l]).reshape(bl, n_past, SWA_KV_HEADS * HEAD_DIM),
                  _bf(cache_swa_v[:, l]).reshape(bl, n_past, SWA_KV_HEADS * HEAD_DIM))
        final = l == DEPTH - 1
        out, _, _ = _layer(xs, l, w, mods[l].reshape(MOD_ROWS, 1, -1), lat_row, bl, ll, True,
                           _state_to_blockdiag_t(state_hgrn[:, l]), caches, final)
        if final:
            xs, y_sample = out
        else:
            xs = out[0]

    return (y_prompt.reshape(bc, lc, D_MODEL), y_sample.reshape(bl, ll, D_MODEL), *cache_out, jnp.stack(states, axis=1))
```
